```python
import math
import jax, jax.numpy as jnp
from jax import lax
import numpy as np

D_MODEL = 1024
BATCH = 4
SEQ = 4096
DEPTH = 2

CHUNK = 64
N_MIXERS = 2
N_RWKV = (DEPTH + 1) // 2
N_SSM = DEPTH // 2
RWKV_HEAD = 64
RWKV_HEADS = D_MODEL // RWKV_HEAD
DECAY_LORA = 64
AAA_LORA = 64
GATE_LORA = 160
GN_EPS = 64e-5
SSM_GROUP = 16
SSM_GROUPS = D_MODEL // SSM_GROUP
SSM_STATE = 64
D_FF = 4 * D_MODEL
DN_ALPHA = (2.0 * DEPTH) ** 0.25
DN_BETA = (8.0 * DEPTH) ** -0.25
LN_EPS = 1e-5

kernel_name = "rwkv7_s5_interleaved_deepnorm_trunk"


def layer_norm(x, g, b):
    xf = x.astype(jnp.float32)
    mu = jnp.mean(xf, axis=-1, keepdims=True)
    var = jnp.mean(jnp.square(xf - mu), axis=-1, keepdims=True)
    return ((xf - mu) * lax.rsqrt(var + LN_EPS) * g + b).astype(x.dtype)


def cmul(ar, ai, br, bi):
    return ar * br - ai * bi, ar * bi + ai * br


def rwkv7_time_mix(x, mu, w0, w1, w2, a0, a1, a2, g1, g2, k_k, k_a, r_k,
                   wr, wk, wv, wo, lnx_g, lnx_b):
    bsz, seq, d = x.shape
    f32 = jnp.float32
    xx = jnp.pad(x, ((0, 0), (1, 0), (0, 0)))[:, :-1] - x
    xr = x + xx * mu[0]
    xw = x + xx * mu[1]
    xk = x + xx * mu[2]
    xv = x + xx * mu[3]
    xa = x + xx * mu[4]
    xg = x + xx * mu[5]
    r = xr @ wr
    w_pre = (w0 + jnp.tanh(xw @ w1) @ w2).astype(f32)
    decay = jnp.exp(-jnp.exp(-jax.nn.softplus(-w_pre) - 0.5))
    k = xk @ wk
    v = xv @ wv
    a = jax.nn.sigmoid(a0 + (xa @ a1) @ a2)
    g = jax.nn.sigmoid(xg @ g1) @ g2

    def heads(t):
        return t.astype(f32).reshape(bsz, seq, RWKV_HEADS, RWKV_HEAD)

    kk = heads(k * k_k)
    kk = kk / jnp.maximum(jnp.sqrt(jnp.sum(kk * kk, axis=-1, keepdims=True)), 1e-12)
    k = k * (1.0 + (a - 1.0) * k_a)
    rh, kh, vh, ah, wh = heads(r), heads(k), heads(v), heads(a), heads(decay)
    seq_in = tuple(jnp.moveaxis(t, 1, 0) for t in (rh, wh, kh, vh, -kk, kk * ah))

    def step(state, inp):
        r_t, w_t, k_t, v_t, a_t, b_t = inp
        sa = jnp.einsum('bhvk,bhk->bhv', state, a_t)
        state = (state * w_t[:, :, None, :] + sa[..., None] * b_t[:, :, None, :]
                 + v_t[..., None] * k_t[:, :, None, :])
        return state, jnp.einsum('bhvk,bhk->bhv', state, r_t)

    s0 = jnp.zeros((bsz, RWKV_HEADS, RWKV_HEAD, RWKV_HEAD), f32)
    _, o = lax.scan(step, s0, seq_in)
    o = jnp.moveaxis(o, 0, 1)
    om = jnp.mean(o, axis=-1, keepdims=True)
    ov = jnp.mean(jnp.square(o - om), axis=-1, keepdims=True)
    o = ((o - om) * lax.rsqrt(ov + GN_EPS)).reshape(bsz, seq, d) * lnx_g + lnx_b
    bonus = jnp.sum(rh * kh * r_k.astype(f32), axis=-1, keepdims=True) * vh
    o = o + bonus.reshape(bsz, seq, d)
    return (o.astype(x.dtype) * g) @ wo


def s5_mix(x, a_re, a_im, log_dt, b_re, b_im, c_re, c_im, d_skip, w_glu):
    bsz, seq, d = x.shape
    f32 = jnp.float32
    xf = x.astype(f32)
    dt = jnp.exp(log_dt.astype(f32))[:, None]
    lam_re = jnp.minimum(a_re.astype(f32), -1e-4)
    lam_im = a_im.astype(f32)
    mag = jnp.exp(dt * lam_re)
    abar_re = mag * jnp.cos(dt * lam_im)
    abar_im = mag * jnp.sin(dt * lam_im)
    den = lam_re * lam_re + lam_im * lam_im
    nr, ni = abar_re - 1.0, abar_im
    coef_re = (nr * lam_re + ni * lam_im) / den
    coef_im = (ni * lam_re - nr * lam_im) / den
    bbar_re, bbar_im = cmul(coef_re[..., None], coef_im[..., None],
                            b_re.astype(f32), b_im.astype(f32))
    cr_w, ci_w = c_re.astype(f32), c_im.astype(f32)

    n_chunks = seq // CHUNK
    u = xf.reshape(bsz, n_chunks, CHUNK, SSM_GROUPS, SSM_GROUP).transpose(1, 2, 0, 3, 4)
    a_ch_re = jnp.broadcast_to(abar_re, (CHUNK, 1, SSM_GROUPS, SSM_STATE))
    a_ch_im = jnp.broadcast_to(abar_im, (CHUNK, 1, SSM_GROUPS, SSM_STATE))

    def combine(e1, e2):
        a1r, a1i, b1r, b1i = e1
        a2r, a2i, b2r, b2i = e2
        ar, ai = cmul(a2r, a2i, a1r, a1i)
        br, bi = cmul(a2r, a2i, b1r, b1i)
        return ar, ai, br + b2r, bi + b2i

    def chunk_step(h, u_c):
        h_re, h_im = h
        bu_re = jnp.einsum('tbgc,gpc->tbgp', u_c, bbar_re)
        bu_im = jnp.einsum('tbgc,gpc->tbgp', u_c, bbar_im)
        pr, pi, sr, si = lax.associative_scan(combine, (a_ch_re, a_ch_im, bu_re, bu_im), axis=0)
        carry_re, carry_im = cmul(pr, pi, h_re[None], h_im[None])
        s_re = sr + carry_re
        s_im = si + carry_im
        y = (jnp.einsum('tbgp,gcp->tbgc', s_re, cr_w)
             - jnp.einsum('tbgp,gcp->tbgc', s_im, ci_w))
        return (s_re[-1], s_im[-1]), y

    h0 = jnp.zeros((bsz, SSM_GROUPS, SSM_STATE), f32)
    _, y = lax.scan(chunk_step, (h0, h0), u)
    y = y.transpose(2, 0, 1, 3, 4).reshape(bsz, seq, d) + xf * d_skip.astype(f32)
    y = jax.nn.gelu(y).astype(x.dtype)
    z = y @ w_glu
    return z[..., :d] * jax.nn.sigmoid(z[..., d:])


def sq_relu_mlp(x, w1, w2):
    return jnp.square(jax.nn.relu(x @ w1)) @ w2


def setup_inputs(seed: int = 0) -> dict:
    key = jax.random.key(seed)
    ks = list(jax.random.split(key, 40))
    f32 = jnp.float32

    def nrm(i, shape, scale):
        return scale * jax.random.normal(ks[i], shape, f32)

    def uni(i, shape, lo, hi):
        return jax.random.uniform(ks[i], shape, f32, lo, hi)

    C, H, N = D_MODEL, RWKV_HEADS, RWKV_HEAD
    G, P, S = SSM_GROUPS, SSM_STATE, SSM_GROUP
    nr_, ns_ = N_RWKV, N_SSM
    glu = nrm(30, (ns_, C, 2 * C), C ** -0.5)
    glu = glu * jnp.concatenate([jnp.full((C,), DN_BETA, f32), jnp.ones((C,), f32)])
    a_im = jnp.pi * jnp.arange(P, dtype=f32)[None, None, :] + nrm(22, (ns_, G, P), 0.01)
    return {
        "x": nrm(0, (BATCH, SEQ, C), 1.0),
        "ln_g": 1.0 + nrm(1, (2 * DEPTH, C), 0.02),
        "ln_b": nrm(2, (2 * DEPTH, C), 0.02),
        "rw_mu": uni(3, (nr_, 6, C), 0.0, 1.0),
        "rw_w0": uni(4, (nr_, C), -6.0, -1.0),
        "rw_w1": nrm(5, (nr_, C, DECAY_LORA), C ** -0.5),
        "rw_w2": nrm(6, (nr_, DECAY_LORA, C), 0.1 * DECAY_LORA ** -0.5),
        "rw_a0": nrm(7, (nr_, C), 0.1),
        "rw_a1": nrm(8, (nr_, C, AAA_LORA), C ** -0.5),
        "rw_a2": nrm(9, (nr_, AAA_LORA, C), 0.1 * AAA_LORA ** -0.5),
        "rw_g1": nrm(10, (nr_, C, GATE_LORA), C ** -0.5),
        "rw_g2": nrm(11, (nr_, GATE_LORA, C), GATE_LORA ** -0.5),
        "rw_k_k": 0.85 + nrm(12, (nr_, C), 0.02),
        "rw_k_a": 1.0 + nrm(13, (nr_, C), 0.02),
        "rw_r_k": -0.04 + nrm(14, (nr_, H, N), 0.02),
        "rw_wr": nrm(15, (nr_, C, C), C ** -0.5),
        "rw_wk": nrm(16, (nr_, C, C), C ** -0.5),
        "rw_wv": nrm(17, (nr_, C, C), C ** -0.5),
        "rw_wo": nrm(18, (nr_, C, C), DN_BETA * C ** -0.5),
        "rw_lnx_g": 1.0 + nrm(19, (nr_, C), 0.02),
        "rw_lnx_b": nrm(20, (nr_, C), 0.02),
        "s5_a_re": -0.5 + nrm(21, (ns_, G, P), 0.01),
        "s5_a_im": a_im,
        "s5_log_dt": uni(23, (ns_, G), math.log(1e-3), math.log(1e-1)),
        "s5_b_re": nrm(24, (ns_, G, P, S), (2.0 * S) ** -0.5),
        "s5_b_im": nrm(25, (ns_, G, P, S), (2.0 * S) ** -0.5),
        "s5_c_re": nrm(26, (ns_, G, S, P), (2.0 * P) ** -0.5),
        "s5_c_im": nrm(27, (ns_, G, S, P), (2.0 * P) ** -0.5),
        "s5_d": nrm(28, (ns_, C), 1.0),
        "s5_w_glu": glu,
        "mlp_w1": nrm(31, (DEPTH, C, D_FF), C ** -0.5),
        "mlp_w2": nrm(32, (DEPTH, D_FF, C), DN_BETA * D_FF ** -0.5),
    }


def reference(x, ln_g, ln_b, rw_mu, rw_w0, rw_w1, rw_w2, rw_a0, rw_a1, rw_a2, rw_g1, rw_g2,
              rw_k_k, rw_k_a, rw_r_k, rw_wr, rw_wk, rw_wv, rw_wo, rw_lnx_g, rw_lnx_b,
              s5_a_re, s5_a_im, s5_log_dt, s5_b_re, s5_b_im, s5_c_re, s5_c_im, s5_d, s5_w_glu,
              mlp_w1, mlp_w2):
    h = x
    for i in range(DEPTH):
        j = i // N_MIXERS
        if i % N_MIXERS == 0:
            mix = rwkv7_time_mix(h, rw_mu[j], rw_w0[j], rw_w1[j], rw_w2[j], rw_a0[j], rw_a1[j],
                                 rw_a2[j], rw_g1[j], rw_g2[j], rw_k_k[j], rw_k_a[j], rw_r_k[j],
                                 rw_wr[j], rw_wk[j], rw_wv[j], rw_wo[j], rw_lnx_g[j], rw_lnx_b[j])
        else:
            mix = s5_mix(h, s5_a_re[j], s5_a_im[j], s5_log_dt[j], s5_b_re[j], s5_b_im[j],
                         s5_c_re[j], s5_c_im[j], s5_d[j], s5_w_glu[j])
        h = layer_norm(DN_ALPHA * h + mix, ln_g[2 * i], ln_b[2 * i])
        h = layer_norm(DN_ALPHA * h + sq_relu_mlp(h, mlp_w1[i], mlp_w2[i]),
                       ln_g[2 * i + 1], ln_b[2 * i + 1])
    return h
```

```python
import functools
import math

import jax
import jax.numpy as jnp
from jax import lax
from jax.experimental import pallas as pl
from jax.experimental.pallas import tpu as pltpu

F32 = jnp.float32
BF16 = jnp.bfloat16

DEPTH = 2
DN_ALPHA = (2.0 * DEPTH) ** 0.25
LN_EPS = 1e-5
GN_EPS = 64e-5

LANES = 128
HEAD = 64
PAIR = 2 * HEAD
SLAB = 256
RW_CHUNK = 64
S5_CHUNK = 8
VMEM_LIMIT = 56 * 1024 * 1024


def _cparams(sem):
    return pltpu.CompilerParams(dimension_semantics=sem, vmem_limit_bytes=VMEM_LIMIT)


def _layer_norm(z, g, b):
    mu = jnp.mean(z, axis=-1, keepdims=True)
    d = z - mu
    var = jnp.mean(d * d, axis=-1, keepdims=True)
    return d * lax.rsqrt(var + LN_EPS) * g + b


def _dot(a, b):
    return jnp.dot(a.astype(BF16), b.astype(BF16), preferred_element_type=F32)


def _dot_t(a, b):
    return lax.dot_general(a.astype(BF16), b.astype(BF16), (((1,), (1,)), ((), ())),
                           preferred_element_type=F32)


def _tdot(a, b):
    return lax.dot_general(a.astype(BF16), b.astype(BF16), (((0,), (0,)), ((), ())),
                           preferred_element_type=F32)


def _mlp_kernel(x_ref, w1_ref, w2_ref, g_ref, b_ref, o_ref, *, ff_chunk):
    x = x_ref[...]
    xb = x.astype(BF16)
    d_ff = w1_ref.shape[1]
    acc = jnp.zeros(x.shape, F32)
    for c in range(d_ff // ff_chunk):
        h = jnp.dot(xb, w1_ref[:, c * ff_chunk:(c + 1) * ff_chunk], preferred_element_type=F32)
        h = jnp.square(jnp.maximum(h, 0.0))
        acc = acc + jnp.dot(h.astype(BF16), w2_ref[c * ff_chunk:(c + 1) * ff_chunk, :],
                            preferred_element_type=F32)
    o_ref[...] = _layer_norm(DN_ALPHA * x + acc, g_ref[...], b_ref[...])


def _mlp_ln(x2, w1, w2, g, b, *, tm=512):
    n, c = x2.shape
    d_ff = w1.shape[1]
    tm = min(tm, n)
    const = lambda i: (0, 0)
    return pl.pallas_call(
        functools.partial(_mlp_kernel, ff_chunk=min(1024, d_ff)),
        grid=(n // tm,),
        in_specs=[
            pl.BlockSpec((tm, c), lambda i: (i, 0)),
            pl.BlockSpec((c, d_ff), const, pipeline_mode=pl.Buffered(1)),
            pl.BlockSpec((d_ff, c), const, pipeline_mode=pl.Buffered(1)),
            pl.BlockSpec((1, c), const),
            pl.BlockSpec((1, c), const),
        ],
        out_specs=pl.BlockSpec((tm, c), lambda i: (i, 0)),
        out_shape=jax.ShapeDtypeStruct((n, c), F32),
        compiler_params=_cparams(("parallel",)),
        name="mlp_ln",
    )(x2, w1.astype(BF16), w2.astype(BF16), g.reshape(1, c), b.reshape(1, c))


_V_MU, _V_W0, _V_A0, _V_KK, _V_KA = 0, 6, 7, 8, 9


def _head_sum(x, j_ref):
    parts = [_dot(x[:, s:s + SLAB], j_ref[...]) for s in range(0, x.shape[1], SLAB)]
    return jnp.concatenate(parts, axis=1)


def _rw_proj_kernel(x_ref, xp_ref, vec_ref, wr_ref, wk_ref, wv_ref, w1_ref, w2_ref, a1_ref, a2_ref,
                    g1_ref, g2_ref, j_ref, r_ref, k_ref, v_ref, lw_ref, an_ref, bn_ref, g_ref):
    x = x_ref[0]
    tm = x.shape[0]
    row = lax.broadcasted_iota(jnp.int32, (tm, 1), 0)
    prev_last = jnp.where(pl.program_id(1) == 0, 0.0, xp_ref[0][7:8, :])
    xprev = jnp.where(row == 0, prev_last, pltpu.roll(x, 1, axis=0))
    xx = xprev - x

    def mix(i):
        return x + xx * vec_ref[_V_MU + i:_V_MU + i + 1, :]

    r = _dot(mix(0), wr_ref[...])
    w_pre = vec_ref[_V_W0:_V_W0 + 1, :] + _dot(jnp.tanh(_dot(mix(1), w1_ref[...])), w2_ref[...])
    k = _dot(mix(2), wk_ref[...])
    v = _dot(mix(3), wv_ref[...])
    a = jax.nn.sigmoid(vec_ref[_V_A0:_V_A0 + 1, :] + _dot(_dot(mix(4), a1_ref[...]), a2_ref[...]))
    g = _dot(jax.nn.sigmoid(_dot(mix(5), g1_ref[...])), g2_ref[...])

    kk = k * vec_ref[_V_KK:_V_KK + 1, :]
    kk = kk / jnp.maximum(jnp.sqrt(_head_sum(kk * kk, j_ref)), 1e-12)
    r_ref[0] = r
    k_ref[0] = k * (1.0 + (a - 1.0) * vec_ref[_V_KA:_V_KA + 1, :])
    v_ref[0] = v
    lw_ref[0] = (-math.exp(-0.5)) * jax.nn.sigmoid(w_pre)
    an_ref[0] = -kk
    bn_ref[0] = kk * a
    g_ref[0] = g


def _rw_proj(x, vecs, wr, wk, wv, w1, w2, a1, a2, g1, g2, jmat, *, tm=256):
    bsz, seq, c = x.shape
    tm = min(tm, seq)
    const = lambda b, i: (0, 0)
    tile = pl.BlockSpec((1, tm, c), lambda b, i: (b, i, 0))
    prev = pl.BlockSpec((1, 8, c), lambda b, i: (b, jnp.maximum(i * (tm // 8) - 1, 0), 0))

    def full(w):
        return pl.BlockSpec(w.shape, const)

    ws = [w.astype(BF16) for w in (wr, wk, wv, w1, w2, a1, a2, g1, g2)]
    out = jax.ShapeDtypeStruct((bsz, seq, c), F32)
    return pl.pallas_call(
        _rw_proj_kernel,
        grid=(bsz, seq // tm),
        in_specs=[tile, prev, full(vecs)] + [full(w) for w in ws] + [full(jmat)],
        out_specs=[tile] * 7,
        out_shape=[out] * 7,
        compiler_params=_cparams(("parallel", "arbitrary")),
        name="rwkv_proj",
    )(x, x, vecs, *ws, jmat)


def _rw_scan_kernel(r_ref, k_ref, v_ref, lw_ref, an_ref, bn_ref, vec_ref, j_ref, y_ref,
                    state_ref, cs_ref, o_ref):
    tm = r_ref.shape[1]
    c = r_ref.shape[2]
    n_pairs = c // PAIR
    L = RW_CHUNK

    @pl.when(pl.program_id(1) == 0)
    def _():
        state_ref[...] = jnp.zeros_like(state_ref)

    row = lax.broadcasted_iota(jnp.int32, (tm, 1), 0) % L
    cs = lw_ref[0]
    d = 1
    while d < L:
        cs = cs + jnp.where(row >= d, pltpu.roll(cs, d, axis=0), 0.0)
        d *= 2
    cs_ref[...] = cs

    ri = lax.broadcasted_iota(jnp.int32, (PAIR, PAIR), 0)
    ci = lax.broadcasted_iota(jnp.int32, (PAIR, PAIR), 1)
    same_head = (ri // L) == (ci // L)
    mask_strict = same_head & ((ci % L) < (ri % L))
    mask_incl = same_head & ((ci % L) <= (ri % L))
    eye = (ri == ci).astype(F32)
    lane = lax.broadcasted_iota(jnp.int32, (L, PAIR), 1)
    head0 = lane < HEAD

    def level_mask(dd):
        return same_head & ((ri // (2 * dd)) == (ci // (2 * dd))) & ((ri % (2 * dd)) >= dd) & ((ci % (2 * dd)) < dd)

    def sel(res):
        return jnp.where(head0, res[:L], res[L:])

    def dup(x):
        return jnp.concatenate([x, x], axis=0)

    def chunk_body(ch, carry):
        r0 = pl.multiple_of(ch * L, L)
        rows = pl.ds(r0, L)
        for p in range(n_pairs):
            lanes = slice(p * PAIR, (p + 1) * PAIR)
            r = r_ref[0, rows, lanes]
            k = k_ref[0, rows, lanes]
            v = v_ref[0, rows, lanes]
            a = an_ref[0, rows, lanes]
            b = bn_ref[0, rows, lanes]
            lw = lw_ref[0, rows, lanes]
            cs_c = cs_ref[rows, lanes]
            cs_last = cs_c[L - 1:L, :]
            a_t = a * jnp.exp(cs_c - lw)
            r_t = r * jnp.exp(cs_c)
            e_m = jnp.exp(-cs_c)
            b_t = b * e_m
            k_t = k * e_m
            e_d = jnp.exp(cs_last - cs_c)
            b_d = b * e_d
            k_d = k * e_d

            lhs = jnp.concatenate([jnp.where(head0, a_t, 0.0), jnp.where(head0, 0.0, a_t),
                                   jnp.where(head0, r_t, 0.0), jnp.where(head0, 0.0, r_t)], axis=0)
            rhs = jnp.concatenate([b_t, b_t, k_t, k_t], axis=0)
            sc = _dot_t(lhs, rhs)
            n_ab = jnp.where(mask_strict, sc[:PAIR, :PAIR], 0.0)
            n_ak = jnp.where(mask_strict, sc[:PAIR, PAIR:], 0.0)
            n_rb = jnp.where(mask_incl, sc[PAIR:, :PAIR], 0.0)
            n_rk = jnp.where(mask_incl, sc[PAIR:, PAIR:], 0.0)

            t_inv = eye + jnp.where(level_mask(1), n_ab, 0.0)
            dd = 2
            while dd < L:
                nd = jnp.where(level_mask(dd), n_ab, 0.0)
                t_inv = t_inv + _dot(t_inv, _dot(nd, t_inv))
                dd *= 2

            s_prev = state_ref[p]
            ars = _dot_t(jnp.concatenate([a_t, r_t], axis=0), s_prev)
            vv = dup(v)
            w = ars[:L] + sel(_dot(n_ak, vv))
            u = sel(_dot(t_inv, dup(w)))
            o = ars[L:] + sel(_dot(jnp.concatenate([n_rb, n_rk], axis=1),
                                   jnp.concatenate([dup(u), vv], axis=0)))
            o_ref[rows, lanes] = o
            upd = _tdot(jnp.concatenate([u, v], axis=0), jnp.concatenate([b_d, k_d], axis=0))
            state_ref[p] = s_prev * jnp.exp(cs_last) + jnp.where(same_head, upd, 0.0)
        return carry

    lax.fori_loop(0, tm // L, chunk_body, 0)

    o = o_ref[...]
    mean = _head_sum(o, j_ref) * (1.0 / HEAD)
    dlt = o - mean
    var = _head_sum(dlt * dlt, j_ref) * (1.0 / HEAD)
    bonus = _head_sum(r_ref[0] * k_ref[0] * vec_ref[0:1, :], j_ref) * v_ref[0]
    y_ref[0] = dlt * lax.rsqrt(var + GN_EPS) * vec_ref[1:2, :] + vec_ref[2:3, :] + bonus


def _rw_scan(r, k, v, lw, an, bn, vecs, jmat, *, tm=256):
    bsz, seq, c = r.shape
    tm = min(tm, seq)
    tile = pl.BlockSpec((1, tm, c), lambda b, i: (b, i, 0))
    const = lambda b, i: (0, 0)
    return pl.pallas_call(
        _rw_scan_kernel,
        grid=(bsz, seq // tm),
        in_specs=[tile] * 6 + [pl.BlockSpec(vecs.shape, const), pl.BlockSpec(jmat.shape, const)],
        out_specs=tile,
        out_shape=jax.ShapeDtypeStruct((bsz, seq, c), F32),
        scratch_shapes=[pltpu.VMEM((c // PAIR, PAIR, PAIR), F32),
                        pltpu.VMEM((tm, c), F32),
                        pltpu.VMEM((tm, c), F32)],
        compiler_params=_cparams(("parallel", "arbitrary")),
        name="rwkv_scan",
    )(r, k, v, lw, an, bn, vecs, jmat)


def _gate_proj_ln_kernel(x_ref, y_ref, gate_ref, w_ref, g_ref, b_ref, o_ref):
    x = x_ref[...]
    mix = _dot(y_ref[...] * gate_ref[...], w_ref[...])
    o_ref[...] = _layer_norm(DN_ALPHA * x + mix, g_ref[...], b_ref[...])


def _gate_proj_ln(x2, y2, gate2, w, g, b, *, tm=512):
    n, c = x2.shape
    tm = min(tm, n)
    tile = pl.BlockSpec((tm, c), lambda i: (i, 0))
    const = lambda i: (0, 0)
    return pl.pallas_call(
        _gate_proj_ln_kernel,
        grid=(n // tm,),
        in_specs=[tile, tile, tile, pl.BlockSpec(w.shape, const),
                  pl.BlockSpec((1, c), const), pl.BlockSpec((1, c), const)],
        out_specs=tile,
        out_shape=jax.ShapeDtypeStruct((n, c), F32),
        compiler_params=_cparams(("parallel",)),
        name="gate_proj_ln",
    )(x2, y2, gate2, w.astype(BF16), g.reshape(1, c), b.reshape(1, c))


def _s5_tables(a_re, a_im, log_dt, b_re, b_im, c_re, c_im, n_rows):
    g, p = a_re.shape
    s = b_re.shape[-1]
    L = S5_CHUNK
    dt = jnp.exp(log_dt.astype(F32))[:, None]
    lam_re = jnp.minimum(a_re.astype(F32), -1e-4)
    lam_im = a_im.astype(F32)

    def apow(n):
        n = jnp.asarray(n, F32)[:, None, None]
        mag = jnp.exp(n * dt * lam_re)
        return mag * jnp.cos(n * dt * lam_im), mag * jnp.sin(n * dt * lam_im)

    ab_re, ab_im = apow([1])
    ab_re, ab_im = ab_re[0], ab_im[0]
    den = lam_re * lam_re + lam_im * lam_im
    nr, ni = ab_re - 1.0, ab_im
    coef_re = (nr * lam_re + ni * lam_im) / den
    coef_im = (ni * lam_re - nr * lam_im) / den
    bb_re = coef_re[..., None] * b_re - coef_im[..., None] * b_im
    bb_im = coef_re[..., None] * b_im + coef_im[..., None] * b_re
    cr, ci = c_re.astype(F32), c_im.astype(F32)

    pw_re, pw_im = apow(jnp.arange(L))
    cb_re = (jnp.einsum('gcp,tgp,gpd->gtcd', cr, pw_re, bb_re) - jnp.einsum('gcp,tgp,gpd->gtcd', cr, pw_im, bb_im)
             - jnp.einsum('gcp,tgp,gpd->gtcd', ci, pw_re, bb_im) - jnp.einsum('gcp,tgp,gpd->gtcd', ci, pw_im, bb_re))
    gs = SLAB // s
    k5 = cb_re.reshape(g // gs, gs, L, s, s)
    w_lag = jnp.einsum('sgtcd,gh->stgdhc', k5, jnp.eye(gs, dtype=F32)).reshape(g // gs, L * SLAB, SLAB)

    gt = LANES // s
    n_t = g // gt
    eye_t = jnp.eye(gt, dtype=F32)
    rv_re, rv_im = pw_re[::-1], pw_im[::-1]
    in_re = rv_re[..., None] * bb_re[None] - rv_im[..., None] * bb_im[None]
    in_im = rv_re[..., None] * bb_im[None] + rv_im[..., None] * bb_re[None]

    def lay_in(m):
        m5 = m.reshape(L, n_t, gt, p, s)
        return jnp.einsum('stgpc,gh->tsgchp', m5, eye_t).reshape(n_t, L * LANES, gt * p)

    b_cat = jnp.concatenate([lay_in(in_re), lay_in(in_im)], axis=-1)
    p1_re, p1_im = apow(jnp.arange(1, L + 1))
    out_re = cr[None] * p1_re[:, :, None, :] - ci[None] * p1_im[:, :, None, :]
    out_im = cr[None] * p1_im[:, :, None, :] + ci[None] * p1_re[:, :, None, :]

    def lay_out(m):
        m5 = m.reshape(L, n_t, gt, s, p)
        return jnp.einsum('stgcp,gh->tgpshc', m5, eye_t).reshape(n_t, gt * p, L * LANES)

    c_cat = jnp.concatenate([lay_out(out_re), lay_out(-out_im)], axis=1)

    def lay_vec(re, im):
        n = re.shape[0]
        f = lambda m: m.reshape(n, n_t, gt * p).transpose(1, 0, 2)
        return jnp.concatenate([f(re), f(im)], axis=-1)

    n_steps = max(1, int(math.log2(n_rows)))
    step_pow = lay_vec(*apow([L * (2 ** i) for i in range(n_steps)]))
    carry_pow = lay_vec(*apow(L * jnp.arange(1, n_rows + 1)))
    return w_lag.astype(BF16), b_cat.astype(BF16), c_cat.astype(BF16), step_pow, carry_pow


def _cmul(ar, ai, br, bi):
    return ar * br - ai * bi, ar * bi + ai * br


def _s5_kernel(u_ref, wlag_ref, bcat_ref, ccat_ref, spow_ref, cpow_ref, dskip_ref, y_ref,
               carry_ref, ut_ref, zt_ref):
    tm = u_ref.shape[1]
    L = S5_CHUNK
    n_rows = tm // L
    half = bcat_ref.shape[2] // 2

    @pl.when(pl.program_id(2) == 0)
    def _():
        carry_ref[...] = jnp.zeros_like(carry_ref)

    u = u_ref[0]
    row = lax.broadcasted_iota(jnp.int32, (tm, 1), 0) % L
    shifted = [u.astype(BF16)]
    for tau in range(1, L):
        shifted.append(jnp.where(row >= tau, pltpu.roll(u, tau, axis=0), 0.0).astype(BF16))
    y_lag = jnp.dot(jnp.concatenate(shifted, axis=1), wlag_ref[0], preferred_element_type=F32)

    jrow = lax.broadcasted_iota(jnp.int32, (n_rows, 1), 0)
    for t in range(SLAB // LANES):
        ut_ref[t] = u[:, t * LANES:(t + 1) * LANES]
        ucat = jnp.concatenate([ut_ref[t, pl.ds(s, n_rows, stride=L), :] for s in range(L)], axis=1)
        x = _dot(ucat, bcat_ref[t])
        hr, hi = x[:, :half], x[:, half:]
        d, i = 1, 0
        while d < n_rows:
            pr, pi = spow_ref[t, i:i + 1, :half], spow_ref[t, i:i + 1, half:]
            sr, si = _cmul(pr, pi, pltpu.roll(hr, d, axis=0), pltpu.roll(hi, d, axis=0))
            hr = hr + jnp.where(jrow >= d, sr, 0.0)
            hi = hi + jnp.where(jrow >= d, si, 0.0)
            d *= 2
            i += 1
        cin_r, cin_i = carry_ref[t, :, :half], carry_ref[t, :, half:]
        kr, ki = _cmul(cpow_ref[t, :, :half], cpow_ref[t, :, half:], cin_r, cin_i)
        hr = hr + kr
        hi = hi + ki
        carry_ref[t, :, :half] = hr[n_rows - 1:n_rows, :]
        carry_ref[t, :, half:] = hi[n_rows - 1:n_rows, :]
        pr_r = jnp.where(jrow == 0, cin_r, pltpu.roll(hr, 1, axis=0))
        pr_i = jnp.where(jrow == 0, cin_i, pltpu.roll(hi, 1, axis=0))
        z = _dot(jnp.concatenate([pr_r, pr_i], axis=1), ccat_ref[t])
        for s in range(L):
            zt_ref[t, pl.ds(s, n_rows, stride=L), :] = z[:, s * LANES:(s + 1) * LANES]

    y_state = jnp.concatenate([zt_ref[t] for t in range(SLAB // LANES)], axis=1)
    y = y_lag + y_state + u * dskip_ref[...]
    y = 0.5 * y * (1.0 + jnp.tanh(math.sqrt(2.0 / math.pi) * (y + 0.044715 * (y * y * y))))
    y_ref[0] = y.astype(y_ref.dtype)


def _s5_conv(h, tables, d_skip, *, tm=1024):
    bsz, seq, c = h.shape
    tm = min(tm, seq)
    w_lag, b_cat, c_cat, step_pow, carry_pow = tables
    n_slab = c // SLAB
    tps = SLAB // LANES
    st = b_cat.shape[2]
    return pl.pallas_call(
        _s5_kernel,
        grid=(n_slab, bsz, seq // tm),
        in_specs=[
            pl.BlockSpec((1, tm, SLAB), lambda s, b, i: (b, i, s)),
            pl.BlockSpec((1,) + w_lag.shape[1:], lambda s, b, i: (s, 0, 0)),
            pl.BlockSpec((tps,) + b_cat.shape[1:], lambda s, b, i: (s, 0, 0)),
            pl.BlockSpec((tps,) + c_cat.shape[1:], lambda s, b, i: (s, 0, 0)),
            pl.BlockSpec((tps,) + step_pow.shape[1:], lambda s, b, i: (s, 0, 0)),
            pl.BlockSpec((tps,) + carry_pow.shape[1:], lambda s, b, i: (s, 0, 0)),
            pl.BlockSpec((1, SLAB), lambda s, b, i: (0, s)),
        ],
        out_specs=pl.BlockSpec((1, tm, SLAB), lambda s, b, i: (b, i, s)),
        out_shape=jax.ShapeDtypeStruct((bsz, seq, c), BF16),
        scratch_shapes=[pltpu.VMEM((tps, 1, st), F32), pltpu.VMEM((tps, tm, LANES), F32),
                        pltpu.VMEM((tps, tm, LANES), F32)],
        compiler_params=_cparams(("parallel", "parallel", "arbitrary")),
        name="s5_conv",
    )(h, w_lag, b_cat, c_cat, step_pow, carry_pow, d_skip.reshape(1, c))


def _glu_ln_kernel(x_ref, y_ref, w_ref, g_ref, b_ref, o_ref):
    x = x_ref[...]
    c = x.shape[1]
    z = jnp.dot(y_ref[...], w_ref[...], preferred_element_type=F32)
    mix = z[:, :c] * jax.nn.sigmoid(z[:, c:])
    o_ref[...] = _layer_norm(DN_ALPHA * x + mix, g_ref[...], b_ref[...])


def _glu_ln(x2, y2, w, g, b, *, tm=512):
    n, c = x2.shape
    tm = min(tm, n)
    tile = pl.BlockSpec((tm, c), lambda i: (i, 0))
    const = lambda i: (0, 0)
    return pl.pallas_call(
        _glu_ln_kernel,
        grid=(n // tm,),
        in_specs=[tile, tile, pl.BlockSpec(w.shape, const),
                  pl.BlockSpec((1, c), const), pl.BlockSpec((1, c), const)],
        out_specs=tile,
        out_shape=jax.ShapeDtypeStruct((n, c), F32),
        compiler_params=_cparams(("parallel",)),
        name="glu_ln",
    )(x2, y2, w.astype(BF16), g.reshape(1, c), b.reshape(1, c))


def _rwkv_layer(h, ln_g, ln_b, mu, w0, w1, w2, a0, a1, a2, g1, g2, k_k, k_a, r_k, wr, wk, wv, wo,
                lnx_g, lnx_b):
    bsz, seq, c = h.shape
    zeros = jnp.zeros((6, c), F32)
    vecs = jnp.concatenate([mu, w0[None], a0[None], k_k[None], k_a[None], zeros], axis=0)
    scan_vecs = jnp.concatenate([r_k.reshape(1, c), lnx_g[None], lnx_b[None], zeros[:5]], axis=0)
    idx = jnp.arange(SLAB) // HEAD
    jmat = (idx[:, None] == idx[None, :]).astype(BF16)
    r, k, v, lw, an, bn, gate = _rw_proj(h, vecs, wr, wk, wv, w1, w2, a1, a2, g1, g2, jmat)
    y = _rw_scan(r, k, v, lw, an, bn, scan_vecs, jmat)
    n = bsz * seq
    out = _gate_proj_ln(h.reshape(n, c), y.reshape(n, c), gate.reshape(n, c), wo, ln_g, ln_b)
    return out.reshape(bsz, seq, c)


def _s5_layer(h, ln_g, ln_b, a_re, a_im, log_dt, b_re, b_im, c_re, c_im, d_skip, w_glu, *, tm=1024):
    bsz, seq, c = h.shape
    tm = min(tm, seq)
    tables = _s5_tables(a_re, a_im, log_dt, b_re, b_im, c_re, c_im, tm // S5_CHUNK)
    y = _s5_conv(h, tables, d_skip, tm=tm)
    n = bsz * seq
    out = _glu_ln(h.reshape(n, c), y.reshape(n, c), w_glu, ln_g, ln_b)
    return out.reshape(bsz, seq, c)


def _mlp_layer(h, w1, w2, g, b):
    bsz, seq, c = h.shape
    return _mlp_ln(h.reshape(bsz * seq, c), w1, w2, g, b).reshape(bsz, seq, c)


def kernel(x, ln_g, ln_b, rw_mu, rw_w0, rw_w1, rw_w2, rw_a0, rw_a1, rw_a2, rw_g1, rw_g2, rw_k_k, rw_k_a, rw_r_k, rw_wr, rw_wk, rw_wv, rw_wo, rw_lnx_g, rw_lnx_b, s5_a_re, s5_a_im, s5_log_dt, s5_b_re, s5_b_im, s5_c_re, s5_c_im, s5_d, s5_w_glu, mlp_w1, mlp_w2):
    h = x
    h = _rwkv_layer(h, ln_g[0], ln_b[0], rw_mu[0], rw_w0[0], rw_w1[0], rw_w2[0], rw_a0[0], rw_a1[0],
                    rw_a2[0], rw_g1[0], rw_g2[0], rw_k_k[0], rw_k_a[0], rw_r_k[0], rw_wr[0], rw_wk[0],
                    rw_wv[0], rw_wo[0], rw_lnx_g[0], rw_lnx_b[0])
    h = _mlp_layer(h, mlp_w1[0], mlp_w2[0], ln_g[1], ln_b[1])
    h = _s5_layer(h, ln_g[2], ln_b[2], s5_a_re[0], s5_a_im[0], s5_log_dt[0], s5_b_re[0], s5_b_im[0],
                  s5_c_re[0], s5_c_im[0], s5_d[0], s5_w_glu[0])
    h = _mlp_layer(h, mlp_w1[1], mlp_w2[1], ln_g[3], ln_b[3])
    return h
```

```python
import functools
import math

import jax
import jax.numpy as jnp
from jax import lax
from jax.experimental import pallas as pl
from jax.experimental.pallas import tpu as pltpu

F32 = jnp.float32
BF16 = jnp.bfloat16

DEPTH = 2
DN_ALPHA = (2.0 * DEPTH) ** 0.25
LN_EPS = 1e-5
GN_EPS = 64e-5

LANES = 128
HEAD = 64
PAIR = 2 * HEAD
SLAB = 256
RW_CHUNK = 64
S5_CHUNK = 8
VMEM_LIMIT = 56 * 1024 * 1024


def _cparams(sem):
    return pltpu.CompilerParams(dimension_semantics=sem, vmem_limit_bytes=VMEM_LIMIT)


def _layer_norm(z, g, b):
    mu = jnp.mean(z, axis=-1, keepdims=True)
    d = z - mu
    var = jnp.mean(d * d, axis=-1, keepdims=True)
    return d * lax.rsqrt(var + LN_EPS) * g + b


def _dot(a, b):
    return jnp.dot(a.astype(BF16), b.astype(BF16), preferred_element_type=F32)


def _dot_t(a, b):
    return lax.dot_general(a.astype(BF16), b.astype(BF16), (((1,), (1,)), ((), ())),
                           preferred_element_type=F32)


def _tdot(a, b):
    return lax.dot_general(a.astype(BF16), b.astype(BF16), (((0,), (0,)), ((), ())),
                           preferred_element_type=F32)


def _mlp_kernel(x_ref, w1_ref, w2_ref, g_ref, b_ref, o_ref, *, ff_chunk):
    x = x_ref[...]
    xb = x.astype(BF16)
    d_ff = w1_ref.shape[1]
    acc = jnp.zeros(x.shape, F32)
    for c in range(d_ff // ff_chunk):
        h = jnp.dot(xb, w1_ref[:, c * ff_chunk:(c + 1) * ff_chunk], preferred_element_type=F32)
        h = jnp.square(jnp.maximum(h, 0.0))
        acc = acc + jnp.dot(h.astype(BF16), w2_ref[c * ff_chunk:(c + 1) * ff_chunk, :],
                            preferred_element_type=F32)
    o_ref[...] = _layer_norm(DN_ALPHA * x + acc, g_ref[...], b_ref[...])


def _mlp_ln(x2, w1, w2, g, b, *, tm=512):
    n, c = x2.shape
    d_ff = w1.shape[1]
    tm = min(tm, n)
    const = lambda i: (0, 0)
    return pl.pallas_call(
        functools.partial(_mlp_kernel, ff_chunk=min(1024, d_ff)),
        grid=(n // tm,),
        in_specs=[
            pl.BlockSpec((tm, c), lambda i: (i, 0)),
            pl.BlockSpec((c, d_ff), const, pipeline_mode=pl.Buffered(1)),
            pl.BlockSpec((d_ff, c), const, pipeline_mode=pl.Buffered(1)),
            pl.BlockSpec((1, c), const),
            pl.BlockSpec((1, c), const),
        ],
        out_specs=pl.BlockSpec((tm, c), lambda i: (i, 0)),
        out_shape=jax.ShapeDtypeStruct((n, c), F32),
        compiler_params=_cparams(("parallel",)),
        name="mlp_ln",
    )(x2, w1.astype(BF16), w2.astype(BF16), g.reshape(1, c), b.reshape(1, c))


_V_MU, _V_W0, _V_A0, _V_KK, _V_KA = 0, 6, 7, 8, 9


def _head_sum(x, j_ref):
    parts = [_dot(x[:, s:s + SLAB], j_ref[...]) for s in range(0, x.shape[1], SLAB)]
    return jnp.concatenate(parts, axis=1)


def _rw_proj_kernel(x_ref, xp_ref, vec_ref, wr_ref, wk_ref, wv_ref, w1_ref, w2_ref, a1_ref, a2_ref,
                    g1_ref, g2_ref, j_ref, r_ref, k_ref, v_ref, lw_ref, an_ref, bn_ref, g_ref):
    x = x_ref[0]
    tm = x.shape[0]
    row = lax.broadcasted_iota(jnp.int32, (tm, 1), 0)
    prev_last = jnp.where(pl.program_id(1) == 0, 0.0, xp_ref[0][7:8, :])
    xprev = jnp.where(row == 0, prev_last, pltpu.roll(x, 1, axis=0))
    xx = xprev - x

    def mix(i):
        return x + xx * vec_ref[_V_MU + i:_V_MU + i + 1, :]

    r = _dot(mix(0), wr_ref[...])
    w_pre = vec_ref[_V_W0:_V_W0 + 1, :] + _dot(jnp.tanh(_dot(mix(1), w1_ref[...])), w2_ref[...])
    k = _dot(mix(2), wk_ref[...])
    v = _dot(mix(3), wv_ref[...])
    a = jax.nn.sigmoid(vec_ref[_V_A0:_V_A0 + 1, :] + _dot(_dot(mix(4), a1_ref[...]), a2_ref[...]))
    g = _dot(jax.nn.sigmoid(_dot(mix(5), g1_ref[...])), g2_ref[...])

    kk = k * vec_ref[_V_KK:_V_KK + 1, :]
    kk = kk / jnp.maximum(jnp.sqrt(_head_sum(kk * kk, j_ref)), 1e-12)
    r_ref[0] = r
    k_ref[0] = k * (1.0 + (a - 1.0) * vec_ref[_V_KA:_V_KA + 1, :])
    v_ref[0] = v
    lw_ref[0] = (-math.exp(-0.5)) * jax.nn.sigmoid(w_pre)
    an_ref[0] = -kk
    bn_ref[0] = kk * a
    g_ref[0] = g


def _rw_proj(x, vecs, wr, wk, wv, w1, w2, a1, a2, g1, g2, jmat, *, tm=256):
    bsz, seq, c = x.shape
    tm = min(tm, seq)
    const = lambda b, i: (0, 0)
    tile = pl.BlockSpec((1, tm, c), lambda b, i: (b, i, 0))
    prev = pl.BlockSpec((1, 8, c), lambda b, i: (b, jnp.maximum(i * (tm // 8) - 1, 0), 0))

    def full(w):
        return pl.BlockSpec(w.shape, const)

    ws = [w.astype(BF16) for w in (wr, wk, wv, w1, w2, a1, a2, g1, g2)]
    out = jax.ShapeDtypeStruct((bsz, seq, c), F32)
    return pl.pallas_call(
        _rw_proj_kernel,
        grid=(bsz, seq // tm),
        in_specs=[tile, prev, full(vecs)] + [full(w) for w in ws] + [full(jmat)],
        out_specs=[tile] * 7,
        out_shape=[out] * 7,
        compiler_params=_cparams(("parallel", "arbitrary")),
        name="rwkv_proj",
    )(x, x, vecs, *ws, jmat)


def _rw_scan_kernel(r_ref, k_ref, v_ref, lw_ref, an_ref, bn_ref, vec_ref, j_ref, y_ref,
                    state_ref, cs_ref, o_ref):
    tm = r_ref.shape[1]
    c = r_ref.shape[2]
    n_pairs = c // PAIR
    L = RW_CHUNK

    @pl.when(pl.program_id(1) == 0)
    def _():
        state_ref[...] = jnp.zeros_like(state_ref)

    row = lax.broadcasted_iota(jnp.int32, (tm, 1), 0) % L
    cs = lw_ref[0]
    d = 1
    while d < L:
        cs = cs + jnp.where(row >= d, pltpu.roll(cs, d, axis=0), 0.0)
        d *= 2
    cs_ref[...] = cs

    ri = lax.broadcasted_iota(jnp.int32, (PAIR, PAIR), 0)
    ci = lax.broadcasted_iota(jnp.int32, (PAIR, PAIR), 1)
    same_head = (ri // L) == (ci // L)
    mask_strict = same_head & ((ci % L) < (ri % L))
    mask_incl = same_head & ((ci % L) <= (ri % L))
    eye = (ri == ci).astype(F32)
    lane = lax.broadcasted_iota(jnp.int32, (L, PAIR), 1)
    head0 = lane < HEAD

    def level_mask(dd):
        return same_head & ((ri // (2 * dd)) == (ci // (2 * dd))) & ((ri % (2 * dd)) >= dd) & ((ci % (2 * dd)) < dd)

    def sel(res):
        return jnp.where(head0, res[:L], res[L:])

    def dup(x):
        return jnp.concatenate([x, x], axis=0)

    def chunk_body(ch, carry):
        r0 = pl.multiple_of(ch * L, L)
        rows = pl.ds(r0, L)
        pairs = range(n_pairs)
        lanes = [slice(p * PAIR, (p + 1) * PAIR) for p in pairs]
        v = [v_ref[0, rows, lanes[p]] for p in pairs]
        a_t, r_t, b_t, k_t, b_d, k_d, e_l = [], [], [], [], [], [], []
        for p in pairs:
            r = r_ref[0, rows, lanes[p]]
            k = k_ref[0, rows, lanes[p]]
            a = an_ref[0, rows, lanes[p]]
            b = bn_ref[0, rows, lanes[p]]
            lw = lw_ref[0, rows, lanes[p]]
            cs_c = cs_ref[rows, lanes[p]]
            cs_last = cs_c[L - 1:L, :]
            a_t.append(a * jnp.exp(cs_c - lw))
            r_t.append(r * jnp.exp(cs_c))
            e_m = jnp.exp(-cs_c)
            b_t.append(b * e_m)
            k_t.append(k * e_m)
            e_d = jnp.exp(cs_last - cs_c)
            b_d.append(b * e_d)
            k_d.append(k * e_d)
            e_l.append(jnp.exp(cs_last))

        sc = []
        for p in pairs:
            lhs = jnp.concatenate([jnp.where(head0, a_t[p], 0.0), jnp.where(head0, 0.0, a_t[p]),
                                   jnp.where(head0, r_t[p], 0.0), jnp.where(head0, 0.0, r_t[p])], axis=0)
            rhs = jnp.concatenate([b_t[p], b_t[p], k_t[p], k_t[p]], axis=0)
            sc.append(_dot_t(lhs, rhs))
        n_ab = [jnp.where(mask_strict, sc[p][:PAIR, :PAIR], 0.0) for p in pairs]
        n_ak = [jnp.where(mask_strict, sc[p][:PAIR, PAIR:], 0.0) for p in pairs]
        n_r = [jnp.concatenate([jnp.where(mask_incl, sc[p][PAIR:, :PAIR], 0.0),
                                jnp.where(mask_incl, sc[p][PAIR:, PAIR:], 0.0)], axis=1) for p in pairs]

        t_inv = [eye + jnp.where(level_mask(1), n_ab[p], 0.0) for p in pairs]
        dd = 2
        while dd < L:
            lm = level_mask(dd)
            nt = [_dot(jnp.where(lm, n_ab[p], 0.0), t_inv[p]) for p in pairs]
            t_inv = [t_inv[p] + _dot(t_inv[p], nt[p]) for p in pairs]
            dd *= 2

        s_prev = [state_ref[p] for p in pairs]
        ars = [_dot_t(jnp.concatenate([a_t[p], r_t[p]], axis=0), s_prev[p]) for p in pairs]
        vv = [dup(v[p]) for p in pairs]
        w = [ars[p][:L] + sel(_dot(n_ak[p], vv[p])) for p in pairs]
        u = [sel(_dot(t_inv[p], dup(w[p]))) for p in pairs]
        for p in pairs:
            o_ref[rows, lanes[p]] = ars[p][L:] + sel(_dot(n_r[p], jnp.concatenate([dup(u[p]), vv[p]], axis=0)))
        for p in pairs:
            upd = _tdot(jnp.concatenate([u[p], v[p]], axis=0), jnp.concatenate([b_d[p], k_d[p]], axis=0))
            state_ref[p] = s_prev[p] * e_l[p] + jnp.where(same_head, upd, 0.0)
        return carry

    lax.fori_loop(0, tm // L, chunk_body, 0)

    o = o_ref[...]
    mean = _head_sum(o, j_ref) * (1.0 / HEAD)
    dlt = o - mean
    var = _head_sum(dlt * dlt, j_ref) * (1.0 / HEAD)
    bonus = _head_sum(r_ref[0] * k_ref[0] * vec_ref[0:1, :], j_ref) * v_ref[0]
    y_ref[0] = dlt * lax.rsqrt(var + GN_EPS) * vec_ref[1:2, :] + vec_ref[2:3, :] + bonus


def _rw_scan(r, k, v, lw, an, bn, vecs, jmat, *, tm=256):
    bsz, seq, c = r.shape
    tm = min(tm, seq)
    tile = pl.BlockSpec((1, tm, c), lambda b, i: (b, i, 0))
    const = lambda b, i: (0, 0)
    return pl.pallas_call(
        _rw_scan_kernel,
        grid=(bsz, seq // tm),
        in_specs=[tile] * 6 + [pl.BlockSpec(vecs.shape, const), pl.BlockSpec(jmat.shape, const)],
        out_specs=tile,
        out_shape=jax.ShapeDtypeStruct((bsz, seq, c), F32),
        scratch_shapes=[pltpu.VMEM((c // PAIR, PAIR, PAIR), F32),
                        pltpu.VMEM((tm, c), F32),
                        pltpu.VMEM((tm, c), F32)],
        compiler_params=_cparams(("parallel", "arbitrary")),
        name="rwkv_scan",
    )(r, k, v, lw, an, bn, vecs, jmat)


def _gate_proj_ln_kernel(x_ref, y_ref, gate_ref, w_ref, g_ref, b_ref, o_ref):
    x = x_ref[...]
    mix = _dot(y_ref[...] * gate_ref[...], w_ref[...])
    o_ref[...] = _layer_norm(DN_ALPHA * x + mix, g_ref[...], b_ref[...])


def _gate_proj_ln(x2, y2, gate2, w, g, b, *, tm=512):
    n, c = x2.shape
    tm = min(tm, n)
    tile = pl.BlockSpec((tm, c), lambda i: (i, 0))
    const = lambda i: (0, 0)
    return pl.pallas_call(
        _gate_proj_ln_kernel,
        grid=(n // tm,),
        in_specs=[tile, tile, tile, pl.BlockSpec(w.shape, const),
                  pl.BlockSpec((1, c), const), pl.BlockSpec((1, c), const)],
        out_specs=tile,
        out_shape=jax.ShapeDtypeStruct((n, c), F32),
        compiler_params=_cparams(("parallel",)),
        name="gate_proj_ln",
    )(x2, y2, gate2, w.astype(BF16), g.reshape(1, c), b.reshape(1, c))


def _s5_tables(a_re, a_im, log_dt, b_re, b_im, c_re, c_im, n_rows):
    g, p = a_re.shape
    s = b_re.shape[-1]
    L = S5_CHUNK
    dt = jnp.exp(log_dt.astype(F32))[:, None]
    lam_re = jnp.minimum(a_re.astype(F32), -1e-4)
    lam_im = a_im.astype(F32)

    def apow(n):
        n = jnp.asarray(n, F32)[:, None, None]
        mag = jnp.exp(n * dt * lam_re)
        return mag * jnp.cos(n * dt * lam_im), mag * jnp.sin(n * dt * lam_im)

    ab_re, ab_im = apow([1])
    ab_re, ab_im = ab_re[0], ab_im[0]
    den = lam_re * lam_re + lam_im * lam_im
    nr, ni = ab_re - 1.0, ab_im
    coef_re = (nr * lam_re + ni * lam_im) / den
    coef_im = (ni * lam_re - nr * lam_im) / den
    bb_re = coef_re[..., None] * b_re - coef_im[..., None] * b_im
    bb_im = coef_re[..., None] * b_im + coef_im[..., None] * b_re
    cr, ci = c_re.astype(F32), c_im.astype(F32)

    pw_re, pw_im = apow(jnp.arange(L))
    cb_re = (jnp.einsum('gcp,tgp,gpd->gtcd', cr, pw_re, bb_re) - jnp.einsum('gcp,tgp,gpd->gtcd', cr, pw_im, bb_im)
             - jnp.einsum('gcp,tgp,gpd->gtcd', ci, pw_re, bb_im) - jnp.einsum('gcp,tgp,gpd->gtcd', ci, pw_im, bb_re))
    gs = SLAB // s
    k5 = cb_re.reshape(g // gs, gs, L, s, s)
    w_lag = jnp.einsum('sgtcd,gh->stgdhc', k5, jnp.eye(gs, dtype=F32)).reshape(g // gs, L * SLAB, SLAB)

    gt = LANES // s
    n_t = g // gt
    eye_t = jnp.eye(gt, dtype=F32)
    rv_re, rv_im = pw_re[::-1], pw_im[::-1]
    in_re = rv_re[..., None] * bb_re[None] - rv_im[..., None] * bb_im[None]
    in_im = rv_re[..., None] * bb_im[None] + rv_im[..., None] * bb_re[None]

    def lay_in(m):
        m5 = m.reshape(L, n_t, gt, p, s)
        return jnp.einsum('stgpc,gh->tsgchp', m5, eye_t).reshape(n_t, L * LANES, gt * p)

    b_cat = jnp.concatenate([lay_in(in_re), lay_in(in_im)], axis=-1)
    p1_re, p1_im = apow(jnp.arange(1, L + 1))
    out_re = cr[None] * p1_re[:, :, None, :] - ci[None] * p1_im[:, :, None, :]
    out_im = cr[None] * p1_im[:, :, None, :] + ci[None] * p1_re[:, :, None, :]

    def lay_out(m):
        m5 = m.reshape(L, n_t, gt, s, p)
        return jnp.einsum('stgcp,gh->tgpshc', m5, eye_t).reshape(n_t, gt * p, L * LANES)

    c_cat = jnp.concatenate([lay_out(out_re), lay_out(-out_im)], axis=1)

    def lay_vec(re, im):
        n = re.shape[0]
        f = lambda m: m.reshape(n, n_t, gt * p).transpose(1, 0, 2)
        return jnp.concatenate([f(re), f(im)], axis=-1)

    n_steps = max(1, int(math.log2(n_rows)))
    step_pow = lay_vec(*apow([L * (2 ** i) for i in range(n_steps)]))
    carry_pow = lay_vec(*apow(L * jnp.arange(1, n_rows + 1)))
    return w_lag.astype(BF16), b_cat.astype(BF16), c_cat.astype(BF16), step_pow, carry_pow


def _cmul(ar, ai, br, bi):
    return ar * br - ai * bi, ar * bi + ai * br


def _s5_kernel(u_ref, wlag_ref, bcat_ref, ccat_ref, spow_ref, cpow_ref, dskip_ref, y_ref,
               carry_ref, ut_ref, zt_ref):
    tm = u_ref.shape[1]
    L = S5_CHUNK
    n_rows = tm // L
    half = bcat_ref.shape[2] // 2

    @pl.when(pl.program_id(2) == 0)
    def _():
        carry_ref[...] = jnp.zeros_like(carry_ref)

    u = u_ref[0]
    row = lax.broadcasted_iota(jnp.int32, (tm, 1), 0) % L
    shifted = [u.astype(BF16)]
    for tau in range(1, L):
        shifted.append(jnp.where(row >= tau, pltpu.roll(u, tau, axis=0), 0.0).astype(BF16))
    y_lag = jnp.dot(jnp.concatenate(shifted, axis=1), wlag_ref[0], preferred_element_type=F32)

    jrow = lax.broadcasted_iota(jnp.int32, (n_rows, 1), 0)
    for t in range(SLAB // LANES):
        ut_ref[t] = u[:, t * LANES:(t + 1) * LANES]
        ucat = jnp.concatenate([ut_ref[t, pl.ds(s, n_rows, stride=L), :] for s in range(L)], axis=1)
        x = _dot(ucat, bcat_ref[t])
        hr, hi = x[:, :half], x[:, half:]
        d, i = 1, 0
        while d < n_rows:
            pr, pi = spow_ref[t, i:i + 1, :half], spow_ref[t, i:i + 1, half:]
            sr, si = _cmul(pr, pi, pltpu.roll(hr, d, axis=0), pltpu.roll(hi, d, axis=0))
            hr = hr + jnp.where(jrow >= d, sr, 0.0)
            hi = hi + jnp.where(jrow >= d, si, 0.0)
            d *= 2
            i += 1
        cin_r, cin_i = carry_ref[t, :, :half], carry_ref[t, :, half:]
        kr, ki = _cmul(cpow_ref[t, :, :half], cpow_ref[t, :, half:], cin_r, cin_i)
        hr = hr + kr
        hi = hi + ki
        carry_ref[t, :, :half] = hr[n_rows - 1:n_rows, :]
        carry_ref[t, :, half:] = hi[n_rows - 1:n_rows, :]
        pr_r = jnp.where(jrow == 0, cin_r, pltpu.roll(hr, 1, axis=0))
        pr_i = jnp.where(jrow == 0, cin_i, pltpu.roll(hi, 1, axis=0))
        z = _dot(jnp.concatenate([pr_r, pr_i], axis=1), ccat_ref[t])
        for s in range(L):
            zt_ref[t, pl.ds(s, n_rows, stride=L), :] = z[:, s * LANES:(s + 1) * LANES]

    y_state = jnp.concatenate([zt_ref[t] for t in range(SLAB // LANES)], axis=1)
    y = y_lag + y_state + u * dskip_ref[...]
    y = 0.5 * y * (1.0 + jnp.tanh(math.sqrt(2.0 / math.pi) * (y + 0.044715 * (y * y * y))))
    y_ref[0] = y.astype(y_ref.dtype)


def _s5_conv(h, tables, d_skip, *, tm=1024):
    bsz, seq, c = h.shape
    tm = min(tm, seq)
    w_lag, b_cat, c_cat, step_pow, carry_pow = tables
    n_slab = c // SLAB
    tps = SLAB // LANES
    st = b_cat.shape[2]
    return pl.pallas_call(
        _s5_kernel,
        grid=(n_slab, bsz, seq // tm),
        in_specs=[
            pl.BlockSpec((1, tm, SLAB), lambda s, b, i: (b, i, s)),
            pl.BlockSpec((1,) + w_lag.shape[1:], lambda s, b, i: (s, 0, 0)),
            pl.BlockSpec((tps,) + b_cat.shape[1:], lambda s, b, i: (s, 0, 0)),
            pl.BlockSpec((tps,) + c_cat.shape[1:], lambda s, b, i: (s, 0, 0)),
            pl.BlockSpec((tps,) + step_pow.shape[1:], lambda s, b, i: (s, 0, 0)),
            pl.BlockSpec((tps,) + carry_pow.shape[1:], lambda s, b, i: (s, 0, 0)),
            pl.BlockSpec((1, SLAB), lambda s, b, i: (0, s)),
        ],
        out_specs=pl.BlockSpec((1, tm, SLAB), lambda s, b, i: (b, i, s)),
        out_shape=jax.ShapeDtypeStruct((bsz, seq, c), BF16),
        scratch_shapes=[pltpu.VMEM((tps, 1, st), F32), pltpu.VMEM((tps, tm, LANES), F32),
                        pltpu.VMEM((tps, tm, LANES), F32)],
        compiler_params=_cparams(("parallel", "parallel", "arbitrary")),
        name="s5_conv",
    )(h, w_lag, b_cat, c_cat, step_pow, carry_pow, d_skip.reshape(1, c))


def _glu_ln_kernel(x_ref, y_ref, w_ref, g_ref, b_ref, o_ref):
    x = x_ref[...]
    c = x.shape[1]
    z = jnp.dot(y_ref[...], w_ref[...], preferred_element_type=F32)
    mix = z[:, :c] * jax.nn.sigmoid(z[:, c:])
    o_ref[...] = _layer_norm(DN_ALPHA * x + mix, g_ref[...], b_ref[...])


def _glu_ln(x2, y2, w, g, b, *, tm=512):
    n, c = x2.shape
    tm = min(tm, n)
    tile = pl.BlockSpec((tm, c), lambda i: (i, 0))
    const = lambda i: (0, 0)
    return pl.pallas_call(
        _glu_ln_kernel,
        grid=(n // tm,),
        in_specs=[tile, tile, pl.BlockSpec(w.shape, const),
                  pl.BlockSpec((1, c), const), pl.BlockSpec((1, c), const)],
        out_specs=tile,
        out_shape=jax.ShapeDtypeStruct((n, c), F32),
        compiler_params=_cparams(("parallel",)),
        name="glu_ln",
    )(x2, y2, w.astype(BF16), g.reshape(1, c), b.reshape(1, c))


def _rwkv_layer(h, ln_g, ln_b, mu, w0, w1, w2, a0, a1, a2, g1, g2, k_k, k_a, r_k, wr, wk, wv, wo,
                lnx_g, lnx_b):
    bsz, seq, c = h.shape
    zeros = jnp.zeros((6, c), F32)
    vecs = jnp.concatenate([mu, w0[None], a0[None], k_k[None], k_a[None], zeros], axis=0)
    scan_vecs = jnp.concatenate([r_k.reshape(1, c), lnx_g[None], lnx_b[None], zeros[:5]], axis=0)
    idx = jnp.arange(SLAB) // HEAD
    jmat = (idx[:, None] == idx[None, :]).astype(BF16)
    r, k, v, lw, an, bn, gate = _rw_proj(h, vecs, wr, wk, wv, w1, w2, a1, a2, g1, g2, jmat)
    y = _rw_scan(r, k, v, lw, an, bn, scan_vecs, jmat)
    n = bsz * seq
    out = _gate_proj_ln(h.reshape(n, c), y.reshape(n, c), gate.reshape(n, c), wo, ln_g, ln_b)
    return out.reshape(bsz, seq, c)


def _s5_layer(h, ln_g, ln_b, a_re, a_im, log_dt, b_re, b_im, c_re, c_im, d_skip, w_glu, *, tm=1024):
    bsz, seq, c = h.shape
    tm = min(tm, seq)
    tables = _s5_tables(a_re, a_im, log_dt, b_re, b_im, c_re, c_im, tm // S5_CHUNK)
    y = _s5_conv(h, tables, d_skip, tm=tm)
    n = bsz * seq
    out = _glu_ln(h.reshape(n, c), y.reshape(n, c), w_glu, ln_g, ln_b)
    return out.reshape(bsz, seq, c)


def _mlp_layer(h, w1, w2, g, b):
    bsz, seq, c = h.shape
    return _mlp_ln(h.reshape(bsz * seq, c), w1, w2, g, b).reshape(bsz, seq, c)


def kernel(x, ln_g, ln_b, rw_mu, rw_w0, rw_w1, rw_w2, rw_a0, rw_a1, rw_a2, rw_g1, rw_g2, rw_k_k, rw_k_a, rw_r_k, rw_wr, rw_wk, rw_wv, rw_wo, rw_lnx_g, rw_lnx_b, s5_a_re, s5_a_im, s5_log_dt, s5_b_re, s5_b_im, s5_c_re, s5_c_im, s5_d, s5_w_glu, mlp_w1, mlp_w2):
    h = x
    h = _rwkv_layer(h, ln_g[0], ln_b[0], rw_mu[0], rw_w0[0], rw_w1[0], rw_w2[0], rw_a0[0], rw_a1[0],
                    rw_a2[0], rw_g1[0], rw_g2[0], rw_k_k[0], rw_k_a[0], rw_r_k[0], rw_wr[0], rw_wk[0],
                    rw_wv[0], rw_wo[0], rw_lnx_g[0], rw_lnx_b[0])
    h = _mlp_layer(h, mlp_w1[0], mlp_w2[0], ln_g[1], ln_b[1])
    h = _s5_layer(h, ln_g[2], ln_b[2], s5_a_re[0], s5_a_im[0], s5_log_dt[0], s5_b_re[0], s5_b_im[0],
                  s5_c_re[0], s5_c_im[0], s5_d[0], s5_w_glu[0])
    h = _mlp_layer(h, mlp_w1[1], mlp_w2[1], ln_g[3], ln_b[3])
    return h
```

```python
import functools
import math

import jax
import jax.numpy as jnp
from jax import lax
from jax.experimental import pallas as pl
from jax.experimental.pallas import tpu as pltpu

F32 = jnp.float32
BF16 = jnp.bfloat16

DEPTH = 2
DN_ALPHA = (2.0 * DEPTH) ** 0.25
LN_EPS = 1e-5
GN_EPS = 64e-5

LANES = 128
HEAD = 64
PAIR = 2 * HEAD
SLAB = 256
RW_CHUNK = 64
S5_CHUNK = 8
VMEM_LIMIT = 56 * 1024 * 1024


def _cparams(sem):
    return pltpu.CompilerParams(dimension_semantics=sem, vmem_limit_bytes=VMEM_LIMIT)


def _layer_norm(z, g, b):
    mu = jnp.mean(z, axis=-1, keepdims=True)
    d = z - mu
    var = jnp.mean(d * d, axis=-1, keepdims=True)
    return d * lax.rsqrt(var + LN_EPS) * g + b


def _dot(a, b):
    return jnp.dot(a.astype(BF16), b.astype(BF16), preferred_element_type=F32)


def _dot_t(a, b):
    return lax.dot_general(a.astype(BF16), b.astype(BF16), (((1,), (1,)), ((), ())),
                           preferred_element_type=F32)


def _tdot(a, b):
    return lax.dot_general(a.astype(BF16), b.astype(BF16), (((0,), (0,)), ((), ())),
                           preferred_element_type=F32)


def _mlp_kernel(x_ref, w1_ref, w2_ref, g_ref, b_ref, o_ref, *, ff_chunk):
    x = x_ref[...]
    xb = x.astype(BF16)
    d_ff = w1_ref.shape[1]
    acc = jnp.zeros(x.shape, F32)
    for c in range(d_ff // ff_chunk):
        h = jnp.dot(xb, w1_ref[:, c * ff_chunk:(c + 1) * ff_chunk], preferred_element_type=F32)
        h = jnp.square(jnp.maximum(h, 0.0))
        acc = acc + jnp.dot(h.astype(BF16), w2_ref[c * ff_chunk:(c + 1) * ff_chunk, :],
                            preferred_element_type=F32)
    o_ref[...] = _layer_norm(DN_ALPHA * x + acc, g_ref[...], b_ref[...])


def _mlp_ln(x2, w1, w2, g, b, *, tm=512):
    n, c = x2.shape
    d_ff = w1.shape[1]
    tm = min(tm, n)
    const = lambda i: (0, 0)
    return pl.pallas_call(
        functools.partial(_mlp_kernel, ff_chunk=min(1024, d_ff)),
        grid=(n // tm,),
        in_specs=[
            pl.BlockSpec((tm, c), lambda i: (i, 0)),
            pl.BlockSpec((c, d_ff), const, pipeline_mode=pl.Buffered(1)),
            pl.BlockSpec((d_ff, c), const, pipeline_mode=pl.Buffered(1)),
            pl.BlockSpec((1, c), const),
            pl.BlockSpec((1, c), const),
        ],
        out_specs=pl.BlockSpec((tm, c), lambda i: (i, 0)),
        out_shape=jax.ShapeDtypeStruct((n, c), F32),
        compiler_params=_cparams(("parallel",)),
        name="mlp_ln",
    )(x2, w1.astype(BF16), w2.astype(BF16), g.reshape(1, c), b.reshape(1, c))


_V_MU, _V_W0, _V_A0, _V_KK, _V_KA = 0, 6, 7, 8, 9


def _head_sum(x, j_ref):
    parts = [_dot(x[:, s:s + SLAB], j_ref[...]) for s in range(0, x.shape[1], SLAB)]
    return jnp.concatenate(parts, axis=1)


def _rw_proj_kernel(x_ref, xp_ref, vec_ref, wr_ref, wk_ref, wv_ref, w1_ref, w2_ref, a1_ref, a2_ref,
                    g1_ref, g2_ref, j_ref, r_ref, k_ref, v_ref, lw_ref, an_ref, bn_ref, g_ref):
    x = x_ref[0]
    tm = x.shape[0]
    row = lax.broadcasted_iota(jnp.int32, (tm, 1), 0)
    prev_last = jnp.where(pl.program_id(1) == 0, 0.0, xp_ref[0][7:8, :])
    xprev = jnp.where(row == 0, prev_last, pltpu.roll(x, 1, axis=0))
    xx = xprev - x

    def mix(i):
        return x + xx * vec_ref[_V_MU + i:_V_MU + i + 1, :]

    r = _dot(mix(0), wr_ref[...])
    w_pre = vec_ref[_V_W0:_V_W0 + 1, :] + _dot(jnp.tanh(_dot(mix(1), w1_ref[...])), w2_ref[...])
    k = _dot(mix(2), wk_ref[...])
    v = _dot(mix(3), wv_ref[...])
    a = jax.nn.sigmoid(vec_ref[_V_A0:_V_A0 + 1, :] + _dot(_dot(mix(4), a1_ref[...]), a2_ref[...]))
    g = _dot(jax.nn.sigmoid(_dot(mix(5), g1_ref[...])), g2_ref[...])

    kk = k * vec_ref[_V_KK:_V_KK + 1, :]
    kk = kk / jnp.maximum(jnp.sqrt(_head_sum(kk * kk, j_ref)), 1e-12)
    r_ref[0] = r.astype(r_ref.dtype)
    k_ref[0] = (k * (1.0 + (a - 1.0) * vec_ref[_V_KA:_V_KA + 1, :])).astype(k_ref.dtype)
    v_ref[0] = v.astype(v_ref.dtype)
    lw_ref[0] = (-math.exp(-0.5)) * jax.nn.sigmoid(w_pre)
    an_ref[0] = (-kk).astype(an_ref.dtype)
    bn_ref[0] = (kk * a).astype(bn_ref.dtype)
    g_ref[0] = g.astype(g_ref.dtype)


def _rw_proj(x, vecs, wr, wk, wv, w1, w2, a1, a2, g1, g2, jmat, *, tm=512):
    bsz, seq, c = x.shape
    tm = min(tm, seq)
    const = lambda b, i: (0, 0)
    tile = pl.BlockSpec((1, tm, c), lambda b, i: (b, i, 0))
    prev = pl.BlockSpec((1, 8, c), lambda b, i: (b, jnp.maximum(i * (tm // 8) - 1, 0), 0))

    def full(w):
        return pl.BlockSpec(w.shape, const)

    ws = [w.astype(BF16) for w in (wr, wk, wv, w1, w2, a1, a2, g1, g2)]
    out = lambda dt: jax.ShapeDtypeStruct((bsz, seq, c), dt)
    return pl.pallas_call(
        _rw_proj_kernel,
        grid=(bsz, seq // tm),
        in_specs=[tile, prev, full(vecs)] + [full(w) for w in ws] + [full(jmat)],
        out_specs=[tile] * 7,
        out_shape=[out(BF16), out(BF16), out(BF16), out(F32), out(BF16), out(BF16), out(BF16)],
        compiler_params=_cparams(("parallel", "arbitrary")),
        name="rwkv_proj",
    )(x, x, vecs, *ws, jmat)


def _rw_scan_kernel(r_ref, k_ref, v_ref, lw_ref, an_ref, bn_ref, vec_ref, j_ref, y_ref,
                    state_ref, cs_ref, o_ref):
    tm = r_ref.shape[1]
    c = r_ref.shape[2]
    n_pairs = c // PAIR
    L = RW_CHUNK

    @pl.when(pl.program_id(1) == 0)
    def _():
        state_ref[...] = jnp.zeros_like(state_ref)

    row = lax.broadcasted_iota(jnp.int32, (tm, 1), 0) % L
    cs = lw_ref[0]
    d = 1
    while d < L:
        cs = cs + jnp.where(row >= d, pltpu.roll(cs, d, axis=0), 0.0)
        d *= 2
    cs_ref[...] = cs

    ri = lax.broadcasted_iota(jnp.int32, (PAIR, PAIR), 0)
    ci = lax.broadcasted_iota(jnp.int32, (PAIR, PAIR), 1)
    same_head = (ri // L) == (ci // L)
    mask_strict = same_head & ((ci % L) < (ri % L))
    mask_incl = same_head & ((ci % L) <= (ri % L))
    eye = (ri == ci).astype(F32)
    lane = lax.broadcasted_iota(jnp.int32, (L, PAIR), 1)
    head0 = lane < HEAD

    def level_mask(dd):
        return same_head & ((ri // (2 * dd)) == (ci // (2 * dd))) & ((ri % (2 * dd)) >= dd) & ((ci % (2 * dd)) < dd)

    def sel(res):
        return jnp.where(head0, res[:L], res[L:])

    def dup(x):
        return jnp.concatenate([x, x], axis=0)

    def chunk_body(ch, carry):
        r0 = pl.multiple_of(ch * L, L)
        rows = pl.ds(r0, L)
        pairs = range(n_pairs)
        lanes = [slice(p * PAIR, (p + 1) * PAIR) for p in pairs]
        v = [v_ref[0, rows, lanes[p]].astype(F32) for p in pairs]
        a_t, r_t, b_t, k_t, b_d, k_d, e_l = [], [], [], [], [], [], []
        for p in pairs:
            r = r_ref[0, rows, lanes[p]].astype(F32)
            k = k_ref[0, rows, lanes[p]].astype(F32)
            a = an_ref[0, rows, lanes[p]].astype(F32)
            b = bn_ref[0, rows, lanes[p]].astype(F32)
            lw = lw_ref[0, rows, lanes[p]]
            cs_c = cs_ref[rows, lanes[p]]
            cs_last = cs_c[L - 1:L, :]
            a_t.append(a * jnp.exp(cs_c - lw))
            r_t.append(r * jnp.exp(cs_c))
            e_m = jnp.exp(-cs_c)
            b_t.append(b * e_m)
            k_t.append(k * e_m)
            e_d = jnp.exp(cs_last - cs_c)
            b_d.append(b * e_d)
            k_d.append(k * e_d)
            e_l.append(jnp.exp(cs_last))

        sc = []
        for p in pairs:
            lhs = jnp.concatenate([jnp.where(head0, a_t[p], 0.0), jnp.where(head0, 0.0, a_t[p]),
                                   jnp.where(head0, r_t[p], 0.0), jnp.where(head0, 0.0, r_t[p])], axis=0)
            rhs = jnp.concatenate([b_t[p], b_t[p], k_t[p], k_t[p]], axis=0)
            sc.append(_dot_t(lhs, rhs))
        n_ab = [jnp.where(mask_strict, sc[p][:PAIR, :PAIR], 0.0) for p in pairs]
        n_ak = [jnp.where(mask_strict, sc[p][:PAIR, PAIR:], 0.0) for p in pairs]
        n_r = [jnp.concatenate([jnp.where(mask_incl, sc[p][PAIR:, :PAIR], 0.0),
                                jnp.where(mask_incl, sc[p][PAIR:, PAIR:], 0.0)], axis=1) for p in pairs]

        t_inv = [eye + jnp.where(level_mask(1), n_ab[p], 0.0) for p in pairs]
        dd = 2
        while dd < L:
            lm = level_mask(dd)
            nt = [_dot(jnp.where(lm, n_ab[p], 0.0), t_inv[p]) for p in pairs]
            t_inv = [t_inv[p] + _dot(t_inv[p], nt[p]) for p in pairs]
            dd *= 2

        s_prev = [state_ref[p] for p in pairs]
        ars = [_dot_t(jnp.concatenate([a_t[p], r_t[p]], axis=0), s_prev[p]) for p in pairs]
        vv = [dup(v[p]) for p in pairs]
        w = [ars[p][:L] + sel(_dot(n_ak[p], vv[p])) for p in pairs]
        u = [sel(_dot(t_inv[p], dup(w[p]))) for p in pairs]
        for p in pairs:
            o_ref[rows, lanes[p]] = ars[p][L:] + sel(_dot(n_r[p], jnp.concatenate([dup(u[p]), vv[p]], axis=0)))
        for p in pairs:
            upd = _tdot(jnp.concatenate([u[p], v[p]], axis=0), jnp.concatenate([b_d[p], k_d[p]], axis=0))
            state_ref[p] = s_prev[p] * e_l[p] + jnp.where(same_head, upd, 0.0)
        return carry

    lax.fori_loop(0, tm // L, chunk_body, 0)

    o = o_ref[...]
    mean = _head_sum(o, j_ref) * (1.0 / HEAD)
    dlt = o - mean
    var = _head_sum(dlt * dlt, j_ref) * (1.0 / HEAD)
    bonus = _head_sum(r_ref[0].astype(F32) * k_ref[0].astype(F32) * vec_ref[0:1, :], j_ref) * v_ref[0].astype(F32)
    y_ref[0] = dlt * lax.rsqrt(var + GN_EPS) * vec_ref[1:2, :] + vec_ref[2:3, :] + bonus


def _rw_scan(r, k, v, lw, an, bn, vecs, jmat, *, tm=256):
    bsz, seq, c = r.shape
    tm = min(tm, seq)
    tile = pl.BlockSpec((1, tm, c), lambda b, i: (b, i, 0))
    const = lambda b, i: (0, 0)
    return pl.pallas_call(
        _rw_scan_kernel,
        grid=(bsz, seq // tm),
        in_specs=[tile] * 6 + [pl.BlockSpec(vecs.shape, const), pl.BlockSpec(jmat.shape, const)],
        out_specs=tile,
        out_shape=jax.ShapeDtypeStruct((bsz, seq, c), F32),
        scratch_shapes=[pltpu.VMEM((c // PAIR, PAIR, PAIR), F32),
                        pltpu.VMEM((tm, c), F32),
                        pltpu.VMEM((tm, c), F32)],
        compiler_params=_cparams(("parallel", "arbitrary")),
        name="rwkv_scan",
    )(r, k, v, lw, an, bn, vecs, jmat)


def _gate_proj_ln_kernel(x_ref, y_ref, gate_ref, w_ref, g_ref, b_ref, o_ref):
    x = x_ref[...]
    mix = _dot(y_ref[...] * gate_ref[...].astype(F32), w_ref[...])
    o_ref[...] = _layer_norm(DN_ALPHA * x + mix, g_ref[...], b_ref[...])


def _gate_proj_ln(x2, y2, gate2, w, g, b, *, tm=512):
    n, c = x2.shape
    tm = min(tm, n)
    tile = pl.BlockSpec((tm, c), lambda i: (i, 0))
    const = lambda i: (0, 0)
    return pl.pallas_call(
        _gate_proj_ln_kernel,
        grid=(n // tm,),
        in_specs=[tile, tile, tile, pl.BlockSpec(w.shape, const),
                  pl.BlockSpec((1, c), const), pl.BlockSpec((1, c), const)],
        out_specs=tile,
        out_shape=jax.ShapeDtypeStruct((n, c), F32),
        compiler_params=_cparams(("parallel",)),
        name="gate_proj_ln",
    )(x2, y2, gate2, w.astype(BF16), g.reshape(1, c), b.reshape(1, c))


def _s5_tables(a_re, a_im, log_dt, b_re, b_im, c_re, c_im, n_rows):
    g, p = a_re.shape
    s = b_re.shape[-1]
    L = S5_CHUNK
    dt = jnp.exp(log_dt.astype(F32))[:, None]
    lam_re = jnp.minimum(a_re.astype(F32), -1e-4)
    lam_im = a_im.astype(F32)

    def apow(n):
        n = jnp.asarray(n, F32)[:, None, None]
        mag = jnp.exp(n * dt * lam_re)
        return mag * jnp.cos(n * dt * lam_im), mag * jnp.sin(n * dt * lam_im)

    ab_re, ab_im = apow([1])
    ab_re, ab_im = ab_re[0], ab_im[0]
    den = lam_re * lam_re + lam_im * lam_im
    nr, ni = ab_re - 1.0, ab_im
    coef_re = (nr * lam_re + ni * lam_im) / den
    coef_im = (ni * lam_re - nr * lam_im) / den
    bb_re = coef_re[..., None] * b_re - coef_im[..., None] * b_im
    bb_im = coef_re[..., None] * b_im + coef_im[..., None] * b_re
    cr, ci = c_re.astype(F32), c_im.astype(F32)

    pw_re, pw_im = apow(jnp.arange(L))
    cb_re = (jnp.einsum('gcp,tgp,gpd->gtcd', cr, pw_re, bb_re) - jnp.einsum('gcp,tgp,gpd->gtcd', cr, pw_im, bb_im)
             - jnp.einsum('gcp,tgp,gpd->gtcd', ci, pw_re, bb_im) - jnp.einsum('gcp,tgp,gpd->gtcd', ci, pw_im, bb_re))
    gs = SLAB // s
    k5 = cb_re.reshape(g // gs, gs, L, s, s).transpose(0, 2, 1, 4, 3)
    eye_s = jnp.eye(gs, dtype=F32)[None, None, :, None, :, None]
    w_lag = (k5[:, :, :, :, None, :] * eye_s).astype(BF16).reshape(g // gs, L * SLAB, SLAB)

    gt = LANES // s
    n_t = g // gt
    rv_re, rv_im = pw_re[::-1], pw_im[::-1]
    in_re = rv_re[..., None] * bb_re[None] - rv_im[..., None] * bb_im[None]
    in_im = rv_re[..., None] * bb_im[None] + rv_im[..., None] * bb_re[None]
    m_in = jnp.stack([in_re, in_im], axis=0).reshape(2, L, n_t, gt, p, s).transpose(2, 1, 3, 5, 0, 4)
    eye_in = jnp.eye(gt, dtype=F32)[None, None, :, None, None, :, None]
    b_cat = (m_in[:, :, :, :, :, None, :] * eye_in).astype(BF16).reshape(n_t, L * LANES, 2 * gt * p)
    p1_re, p1_im = apow(jnp.arange(1, L + 1))
    out_re = cr[None] * p1_re[:, :, None, :] - ci[None] * p1_im[:, :, None, :]
    out_im = cr[None] * p1_im[:, :, None, :] + ci[None] * p1_re[:, :, None, :]
    m_out = jnp.stack([out_re, -out_im], axis=0).reshape(2, L, n_t, gt, s, p).transpose(2, 0, 3, 5, 1, 4)
    eye_out = jnp.eye(gt, dtype=F32)[None, None, :, None, None, :, None]
    c_cat = (m_out[:, :, :, :, :, None, :] * eye_out).astype(BF16).reshape(n_t, 2 * gt * p, L * LANES)

    def lay_vec(re, im):
        n = re.shape[0]
        f = lambda m: m.reshape(n, n_t, gt * p).transpose(1, 0, 2)
        return jnp.concatenate([f(re), f(im)], axis=-1)

    n_steps = max(1, int(math.log2(n_rows)))
    step_pow = lay_vec(*apow([L * (2 ** i) for i in range(n_steps)]))
    carry_pow = lay_vec(*apow(L * jnp.arange(1, n_rows + 1)))
    return w_lag, b_cat, c_cat, step_pow, carry_pow


def _cmul(ar, ai, br, bi):
    return ar * br - ai * bi, ar * bi + ai * br


def _s5_kernel(u_ref, wlag_ref, bcat_ref, ccat_ref, spow_ref, cpow_ref, dskip_ref, y_ref,
               carry_ref, ut_ref, zt_ref):
    tm = u_ref.shape[1]
    L = S5_CHUNK
    n_rows = tm // L
    half = bcat_ref.shape[2] // 2

    @pl.when(pl.program_id(2) == 0)
    def _():
        carry_ref[...] = jnp.zeros_like(carry_ref)

    u = u_ref[0]
    row = lax.broadcasted_iota(jnp.int32, (tm, 1), 0) % L
    shifted = [u.astype(BF16)]
    for tau in range(1, L):
        shifted.append(jnp.where(row >= tau, pltpu.roll(u, tau, axis=0), 0.0).astype(BF16))
    y_lag = jnp.dot(jnp.concatenate(shifted, axis=1), wlag_ref[0], preferred_element_type=F32)

    jrow = lax.broadcasted_iota(jnp.int32, (n_rows, 1), 0)
    for t in range(SLAB // LANES):
        ut_ref[t] = u[:, t * LANES:(t + 1) * LANES]
        ucat = jnp.concatenate([ut_ref[t, pl.ds(s, n_rows, stride=L), :] for s in range(L)], axis=1)
        x = _dot(ucat, bcat_ref[t])
        hr, hi = x[:, :half], x[:, half:]
        d, i = 1, 0
        while d < n_rows:
            pr, pi = spow_ref[t, i:i + 1, :half], spow_ref[t, i:i + 1, half:]
            sr, si = _cmul(pr, pi, pltpu.roll(hr, d, axis=0), pltpu.roll(hi, d, axis=0))
            hr = hr + jnp.where(jrow >= d, sr, 0.0)
            hi = hi + jnp.where(jrow >= d, si, 0.0)
            d *= 2
            i += 1
        cin_r, cin_i = carry_ref[t, :, :half], carry_ref[t, :, half:]
        kr, ki = _cmul(cpow_ref[t, :, :half], cpow_ref[t, :, half:], cin_r, cin_i)
        hr = hr + kr
        hi = hi + ki
        carry_ref[t, :, :half] = hr[n_rows - 1:n_rows, :]
        carry_ref[t, :, half:] = hi[n_rows - 1:n_rows, :]
        pr_r = jnp.where(jrow == 0, cin_r, pltpu.roll(hr, 1, axis=0))
        pr_i = jnp.where(jrow == 0, cin_i, pltpu.roll(hi, 1, axis=0))
        z = _dot(jnp.concatenate([pr_r, pr_i], axis=1), ccat_ref[t])
        for s in range(L):
            zt_ref[t, pl.ds(s, n_rows, stride=L), :] = z[:, s * LANES:(s + 1) * LANES]

    y_state = jnp.concatenate([zt_ref[t] for t in range(SLAB // LANES)], axis=1)
    y = y_lag + y_state + u * dskip_ref[...]
    y = 0.5 * y * (1.0 + jnp.tanh(math.sqrt(2.0 / math.pi) * (y + 0.044715 * (y * y * y))))
    y_ref[0] = y.astype(y_ref.dtype)


def _s5_conv(h, tables, d_skip, *, tm=1024):
    bsz, seq, c = h.shape
    tm = min(tm, seq)
    w_lag, b_cat, c_cat, step_pow, carry_pow = tables
    n_slab = c // SLAB
    tps = SLAB // LANES
    st = b_cat.shape[2]
    return pl.pallas_call(
        _s5_kernel,
        grid=(n_slab, bsz, seq // tm),
        in_specs=[
            pl.BlockSpec((1, tm, SLAB), lambda s, b, i: (b, i, s)),
            pl.BlockSpec((1,) + w_lag.shape[1:], lambda s, b, i: (s, 0, 0)),
            pl.BlockSpec((tps,) + b_cat.shape[1:], lambda s, b, i: (s, 0, 0)),
            pl.BlockSpec((tps,) + c_cat.shape[1:], lambda s, b, i: (s, 0, 0)),
            pl.BlockSpec((tps,) + step_pow.shape[1:], lambda s, b, i: (s, 0, 0)),
            pl.BlockSpec((tps,) + carry_pow.shape[1:], lambda s, b, i: (s, 0, 0)),
            pl.BlockSpec((1, SLAB), lambda s, b, i: (0, s)),
        ],
        out_specs=pl.BlockSpec((1, tm, SLAB), lambda s, b, i: (b, i, s)),
        out_shape=jax.ShapeDtypeStruct((bsz, seq, c), BF16),
        scratch_shapes=[pltpu.VMEM((tps, 1, st), F32), pltpu.VMEM((tps, tm, LANES), F32),
                        pltpu.VMEM((tps, tm, LANES), F32)],
        compiler_params=_cparams(("parallel", "parallel", "arbitrary")),
        name="s5_conv",
    )(h, w_lag, b_cat, c_cat, step_pow, carry_pow, d_skip.reshape(1, c))


def _glu_ln_kernel(x_ref, y_ref, w_ref, g_ref, b_ref, o_ref):
    x = x_ref[...]
    c = x.shape[1]
    z = jnp.dot(y_ref[...], w_ref[...], preferred_element_type=F32)
    mix = z[:, :c] * jax.nn.sigmoid(z[:, c:])
    o_ref[...] = _layer_norm(DN_ALPHA * x + mix, g_ref[...], b_ref[...])


def _glu_ln(x2, y2, w, g, b, *, tm=512):
    n, c = x2.shape
    tm = min(tm, n)
    tile = pl.BlockSpec((tm, c), lambda i: (i, 0))
    const = lambda i: (0, 0)
    return pl.pallas_call(
        _glu_ln_kernel,
        grid=(n // tm,),
        in_specs=[tile, tile, pl.BlockSpec(w.shape, const),
                  pl.BlockSpec((1, c), const), pl.BlockSpec((1, c), const)],
        out_specs=tile,
        out_shape=jax.ShapeDtypeStruct((n, c), F32),
        compiler_params=_cparams(("parallel",)),
        name="glu_ln",
    )(x2, y2, w.astype(BF16), g.reshape(1, c), b.reshape(1, c))


def _rwkv_layer(h, ln_g, ln_b, mu, w0, w1, w2, a0, a1, a2, g1, g2, k_k, k_a, r_k, wr, wk, wv, wo,
                lnx_g, lnx_b):
    bsz, seq, c = h.shape
    zeros = jnp.zeros((6, c), F32)
    vecs = jnp.concatenate([mu, w0[None], a0[None], k_k[None], k_a[None], zeros], axis=0)
    scan_vecs = jnp.concatenate([r_k.reshape(1, c), lnx_g[None], lnx_b[None], zeros[:5]], axis=0)
    idx = jnp.arange(SLAB) // HEAD
    jmat = (idx[:, None] == idx[None, :]).astype(BF16)
    r, k, v, lw, an, bn, gate = _rw_proj(h, vecs, wr, wk, wv, w1, w2, a1, a2, g1, g2, jmat)
    y = _rw_scan(r, k, v, lw, an, bn, scan_vecs, jmat)
    n = bsz * seq
    out = _gate_proj_ln(h.reshape(n, c), y.reshape(n, c), gate.reshape(n, c), wo, ln_g, ln_b)
    return out.reshape(bsz, seq, c)


def _s5_layer(h, ln_g, ln_b, a_re, a_im, log_dt, b_re, b_im, c_re, c_im, d_skip, w_glu, *, tm=1024):
    bsz, seq, c = h.shape
    tm = min(tm, seq)
    tables = _s5_tables(a_re, a_im, log_dt, b_re, b_im, c_re, c_im, tm // S5_CHUNK)
    y = _s5_conv(h, tables, d_skip, tm=tm)
    n = bsz * seq
    out = _glu_ln(h.reshape(n, c), y.reshape(n, c), w_glu, ln_g, ln_b)
    return out.reshape(bsz, seq, c)


def _mlp_layer(h, w1, w2, g, b):
    bsz, seq, c = h.shape
    return _mlp_ln(h.reshape(bsz * seq, c), w1, w2, g, b).reshape(bsz, seq, c)


def kernel(x, ln_g, ln_b, rw_mu, rw_w0, rw_w1, rw_w2, rw_a0, rw_a1, rw_a2, rw_g1, rw_g2, rw_k_k, rw_k_a, rw_r_k, rw_wr, rw_wk, rw_wv, rw_wo, rw_lnx_g, rw_lnx_b, s5_a_re, s5_a_im, s5_log_dt, s5_b_re, s5_b_im, s5_c_re, s5_c_im, s5_d, s5_w_glu, mlp_w1, mlp_w2):
    h = x
    h = _rwkv_layer(h, ln_g[0], ln_b[0], rw_mu[0], rw_w0[0], rw_w1[0], rw_w2[0], rw_a0[0], rw_a1[0],
                    rw_a2[0], rw_g1[0], rw_g2[0], rw_k_k[0], rw_k_a[0], rw_r_k[0], rw_wr[0], rw_wk[0],
                    rw_wv[0], rw_wo[0], rw_lnx_g[0], rw_lnx_b[0])
    h = _mlp_layer(h, mlp_w1[0], mlp_w2[0], ln_g[1], ln_b[1])
    h = _s5_layer(h, ln_g[2], ln_b[2], s5_a_re[0], s5_a_im[0], s5_log_dt[0], s5_b_re[0], s5_b_im[0],
                  s5_c_re[0], s5_c_im[0], s5_d[0], s5_w_glu[0])
    h = _mlp_layer(h, mlp_w1[1], mlp_w2[1], ln_g[3], ln_b[3])
    return h
```

```python
import functools
import math

import jax
import jax.numpy as jnp
from jax import lax
from jax.experimental import pallas as pl
from jax.experimental.pallas import tpu as pltpu

F32 = jnp.float32
BF16 = jnp.bfloat16

DEPTH = 2
DN_ALPHA = (2.0 * DEPTH) ** 0.25
LN_EPS = 1e-5
GN_EPS = 64e-5

LANES = 128
SUBLANES = 8
_SUBLANE_STEPS = (1, 2, 4)
HEAD = 64
PAIR = 2 * HEAD
SLAB = 256
RW_CHUNK = 64
S5_CHUNK = 8
VMEM_LIMIT = 56 * 1024 * 1024


def _cparams(sem):
    return pltpu.CompilerParams(dimension_semantics=sem, vmem_limit_bytes=VMEM_LIMIT)


def _layer_norm(z, g, b):
    mu = jnp.mean(z, axis=-1, keepdims=True)
    d = z - mu
    var = jnp.mean(d * d, axis=-1, keepdims=True)
    return d * lax.rsqrt(var + LN_EPS) * g + b


def _dot(a, b):
    return jnp.dot(a.astype(BF16), b.astype(BF16), preferred_element_type=F32)


def _dot_t(a, b):
    return lax.dot_general(a.astype(BF16), b.astype(BF16), (((1,), (1,)), ((), ())),
                           preferred_element_type=F32)


def _tdot(a, b):
    return lax.dot_general(a.astype(BF16), b.astype(BF16), (((0,), (0,)), ((), ())),
                           preferred_element_type=F32)


def _mlp_kernel(x_ref, w1_ref, w2_ref, g_ref, b_ref, o_ref, *, ff_chunk):
    x = x_ref[...]
    xb = x.astype(BF16)
    d_ff = w1_ref.shape[1]
    acc = jnp.zeros(x.shape, F32)
    for c in range(d_ff // ff_chunk):
        h = jnp.dot(xb, w1_ref[:, c * ff_chunk:(c + 1) * ff_chunk], preferred_element_type=F32)
        h = jnp.square(jnp.maximum(h, 0.0))
        acc = acc + jnp.dot(h.astype(BF16), w2_ref[c * ff_chunk:(c + 1) * ff_chunk, :],
                            preferred_element_type=F32)
    o_ref[...] = _layer_norm(DN_ALPHA * x + acc, g_ref[...], b_ref[...])


def _mlp_ln(x2, w1, w2, g, b, *, tm=512):
    n, c = x2.shape
    d_ff = w1.shape[1]
    tm = min(tm, n)
    const = lambda i: (0, 0)
    return pl.pallas_call(
        functools.partial(_mlp_kernel, ff_chunk=min(1024, d_ff)),
        grid=(n // tm,),
        in_specs=[
            pl.BlockSpec((tm, c), lambda i: (i, 0)),
            pl.BlockSpec((c, d_ff), const, pipeline_mode=pl.Buffered(1)),
            pl.BlockSpec((d_ff, c), const, pipeline_mode=pl.Buffered(1)),
            pl.BlockSpec((1, c), const),
            pl.BlockSpec((1, c), const),
        ],
        out_specs=pl.BlockSpec((tm, c), lambda i: (i, 0)),
        out_shape=jax.ShapeDtypeStruct((n, c), F32),
        compiler_params=_cparams(("parallel",)),
        name="mlp_ln",
    )(x2, w1.astype(BF16), w2.astype(BF16), g.reshape(1, c), b.reshape(1, c))


_V_MU, _V_W0, _V_A0, _V_KK, _V_KA = 0, 6, 7, 8, 9


def _head_sum(x, j_ref):
    parts = [_dot(x[:, s:s + SLAB], j_ref[...]) for s in range(0, x.shape[1], SLAB)]
    return jnp.concatenate(parts, axis=1)


def _rw_proj_kernel(x_ref, xp_ref, vec_ref, wr_ref, wk_ref, wv_ref, w1_ref, w2_ref, a1_ref, a2_ref,
                    g1_ref, g2_ref, j_ref, r_ref, k_ref, v_ref, lw_ref, an_ref, bn_ref, g_ref):
    x = x_ref[0]
    tm = x.shape[0]
    row = lax.broadcasted_iota(jnp.int32, (tm, 1), 0)
    prev_last = jnp.where(pl.program_id(1) == 0, 0.0, xp_ref[0][7:8, :])
    xprev = jnp.where(row == 0, prev_last, pltpu.roll(x, 1, axis=0))
    xx = xprev - x

    def mix(i):
        return x + xx * vec_ref[_V_MU + i:_V_MU + i + 1, :]

    r = _dot(mix(0), wr_ref[...])
    w_pre = vec_ref[_V_W0:_V_W0 + 1, :] + _dot(jnp.tanh(_dot(mix(1), w1_ref[...])), w2_ref[...])
    k = _dot(mix(2), wk_ref[...])
    v = _dot(mix(3), wv_ref[...])
    a = jax.nn.sigmoid(vec_ref[_V_A0:_V_A0 + 1, :] + _dot(_dot(mix(4), a1_ref[...]), a2_ref[...]))
    g = _dot(jax.nn.sigmoid(_dot(mix(5), g1_ref[...])), g2_ref[...])

    kk = k * vec_ref[_V_KK:_V_KK + 1, :]
    kk = kk / jnp.maximum(jnp.sqrt(_head_sum(kk * kk, j_ref)), 1e-12)
    r_ref[0] = r.astype(r_ref.dtype)
    k_ref[0] = (k * (1.0 + (a - 1.0) * vec_ref[_V_KA:_V_KA + 1, :])).astype(k_ref.dtype)
    v_ref[0] = v.astype(v_ref.dtype)
    lw_ref[0] = (-math.exp(-0.5)) * jax.nn.sigmoid(w_pre)
    an_ref[0] = (-kk).astype(an_ref.dtype)
    bn_ref[0] = (kk * a).astype(bn_ref.dtype)
    g_ref[0] = g.astype(g_ref.dtype)


def _rw_proj(x, vecs, wr, wk, wv, w1, w2, a1, a2, g1, g2, jmat, *, tm=512):
    bsz, seq, c = x.shape
    tm = min(tm, seq)
    const = lambda b, i: (0, 0)
    tile = pl.BlockSpec((1, tm, c), lambda b, i: (b, i, 0))
    prev = pl.BlockSpec((1, 8, c), lambda b, i: (b, jnp.maximum(i * (tm // 8) - 1, 0), 0))

    def full(w):
        return pl.BlockSpec(w.shape, const)

    ws = [w.astype(BF16) for w in (wr, wk, wv, w1, w2, a1, a2, g1, g2)]
    out = lambda dt: jax.ShapeDtypeStruct((bsz, seq, c), dt)
    return pl.pallas_call(
        _rw_proj_kernel,
        grid=(bsz, seq // tm),
        in_specs=[tile, prev, full(vecs)] + [full(w) for w in ws] + [full(jmat)],
        out_specs=[tile] * 7,
        out_shape=[out(BF16), out(BF16), out(BF16), out(F32), out(BF16), out(BF16), out(BF16)],
        compiler_params=_cparams(("parallel", "arbitrary")),
        name="rwkv_proj",
    )(x, x, vecs, *ws, jmat)


def _rw_scan_kernel(r_ref, k_ref, v_ref, lw_ref, an_ref, bn_ref, vec_ref, j_ref, y_ref,
                    state_ref, cs_ref, o_ref):
    nb, tm, c = r_ref.shape
    n_pairs = c // PAIR
    L = RW_CHUNK

    @pl.when(pl.program_id(1) == 0)
    def _():
        state_ref[...] = jnp.zeros_like(state_ref)

    row = lax.broadcasted_iota(jnp.int32, (nb * tm, 1), 0) % L
    cs = lw_ref[...].reshape(nb * tm, c)
    d = 1
    while d < L:
        cs = cs + jnp.where(row >= d, pltpu.roll(cs, d, axis=0), 0.0)
        d *= 2
    cs_ref[...] = cs.reshape(nb, tm, c)

    ri = lax.broadcasted_iota(jnp.int32, (PAIR, PAIR), 0)
    ci = lax.broadcasted_iota(jnp.int32, (PAIR, PAIR), 1)
    same_head = (ri // L) == (ci // L)
    mask_strict = same_head & ((ci % L) < (ri % L))
    mask_incl = same_head & ((ci % L) <= (ri % L))
    eye = (ri == ci).astype(F32)
    lane = lax.broadcasted_iota(jnp.int32, (L, PAIR), 1)
    head0 = lane < HEAD

    def level_mask(dd):
        return same_head & ((ri // (2 * dd)) == (ci // (2 * dd))) & ((ri % (2 * dd)) >= dd) & ((ci % (2 * dd)) < dd)

    def sel(res):
        return jnp.where(head0, res[:L], res[L:])

    def dup(x):
        return jnp.concatenate([x, x], axis=0)

    def chunk_body(ch, carry):
        r0 = pl.multiple_of(ch * L, L)
        rows = pl.ds(r0, L)
        pairs = range(nb * n_pairs)
        seq = [p // n_pairs for p in pairs]
        lanes = [slice((p % n_pairs) * PAIR, (p % n_pairs + 1) * PAIR) for p in pairs]
        v = [v_ref[seq[p], rows, lanes[p]].astype(F32) for p in pairs]
        a_t, r_t, b_t, k_t, b_d, k_d, e_l = [], [], [], [], [], [], []
        for p in pairs:
            r = r_ref[seq[p], rows, lanes[p]].astype(F32)
            k = k_ref[seq[p], rows, lanes[p]].astype(F32)
            a = an_ref[seq[p], rows, lanes[p]].astype(F32)
            b = bn_ref[seq[p], rows, lanes[p]].astype(F32)
            lw = lw_ref[seq[p], rows, lanes[p]]
            cs_c = cs_ref[seq[p], rows, lanes[p]]
            cs_last = cs_c[L - 1:L, :]
            a_t.append(a * jnp.exp(cs_c - lw))
            r_t.append(r * jnp.exp(cs_c))
            e_m = jnp.exp(-cs_c)
            b_t.append(b * e_m)
            k_t.append(k * e_m)
            e_d = jnp.exp(cs_last - cs_c)
            b_d.append(b * e_d)
            k_d.append(k * e_d)
            e_l.append(jnp.exp(cs_last))

        sc = []
        for p in pairs:
            lhs = jnp.concatenate([jnp.where(head0, a_t[p], 0.0), jnp.where(head0, 0.0, a_t[p]),
                                   jnp.where(head0, r_t[p], 0.0), jnp.where(head0, 0.0, r_t[p])], axis=0)
            rhs = jnp.concatenate([b_t[p], b_t[p], k_t[p], k_t[p]], axis=0)
            sc.append(_dot_t(lhs, rhs))
        n_ab = [jnp.where(mask_strict, sc[p][:PAIR, :PAIR], 0.0) for p in pairs]
        n_ak = [jnp.where(mask_strict, sc[p][:PAIR, PAIR:], 0.0) for p in pairs]
        n_r = [jnp.concatenate([jnp.where(mask_incl, sc[p][PAIR:, :PAIR], 0.0),
                                jnp.where(mask_incl, sc[p][PAIR:, PAIR:], 0.0)], axis=1) for p in pairs]

        t_inv = [eye + jnp.where(level_mask(1), n_ab[p], 0.0) for p in pairs]
        dd = 2
        while dd < L:
            lm = level_mask(dd)
            nt = [_dot(jnp.where(lm, n_ab[p], 0.0), t_inv[p]) for p in pairs]
            t_inv = [t_inv[p] + _dot(t_inv[p], nt[p]) for p in pairs]
            dd *= 2

        s_prev = [state_ref[p] for p in pairs]
        ars = [_dot_t(jnp.concatenate([a_t[p], r_t[p]], axis=0), s_prev[p]) for p in pairs]
        vv = [dup(v[p]) for p in pairs]
        w = [ars[p][:L] + sel(_dot(n_ak[p], vv[p])) for p in pairs]
        u = [sel(_dot(t_inv[p], dup(w[p]))) for p in pairs]
        for p in pairs:
            o_ref[seq[p], rows, lanes[p]] = ars[p][L:] + sel(_dot(n_r[p], jnp.concatenate([dup(u[p]), vv[p]], axis=0)))
        for p in pairs:
            upd = _tdot(jnp.concatenate([u[p], v[p]], axis=0), jnp.concatenate([b_d[p], k_d[p]], axis=0))
            state_ref[p] = s_prev[p] * e_l[p] + jnp.where(same_head, upd, 0.0)
        return carry

    lax.fori_loop(0, tm // L, chunk_body, 0)

    def flat(ref):
        return ref[...].reshape(nb * tm, c).astype(F32)

    o = flat(o_ref)
    mean = _head_sum(o, j_ref) * (1.0 / HEAD)
    dlt = o - mean
    var = _head_sum(dlt * dlt, j_ref) * (1.0 / HEAD)
    bonus = _head_sum(flat(r_ref) * flat(k_ref) * vec_ref[0:1, :], j_ref) * flat(v_ref)
    y = dlt * lax.rsqrt(var + GN_EPS) * vec_ref[1:2, :] + vec_ref[2:3, :] + bonus
    y_ref[...] = y.reshape(nb, tm, c)


def _rw_scan(r, k, v, lw, an, bn, vecs, jmat, *, tm=128, nb=4):
    bsz, seq, c = r.shape
    tm = min(tm, seq)
    nb = math.gcd(nb, bsz)
    tile = pl.BlockSpec((nb, tm, c), lambda b, i: (b, i, 0))
    const = lambda b, i: (0, 0)
    return pl.pallas_call(
        _rw_scan_kernel,
        grid=(bsz // nb, seq // tm),
        in_specs=[tile] * 6 + [pl.BlockSpec(vecs.shape, const), pl.BlockSpec(jmat.shape, const)],
        out_specs=tile,
        out_shape=jax.ShapeDtypeStruct((bsz, seq, c), F32),
        scratch_shapes=[pltpu.VMEM((nb * (c // PAIR), PAIR, PAIR), F32),
                        pltpu.VMEM((nb, tm, c), F32),
                        pltpu.VMEM((nb, tm, c), F32)],
        compiler_params=_cparams(("parallel", "arbitrary")),
        name="rwkv_scan",
    )(r, k, v, lw, an, bn, vecs, jmat)


def _gate_proj_ln_kernel(x_ref, y_ref, gate_ref, w_ref, g_ref, b_ref, o_ref):
    x = x_ref[...]
    mix = _dot(y_ref[...] * gate_ref[...].astype(F32), w_ref[...])
    o_ref[...] = _layer_norm(DN_ALPHA * x + mix, g_ref[...], b_ref[...])


def _gate_proj_ln(x2, y2, gate2, w, g, b, *, tm=512):
    n, c = x2.shape
    tm = min(tm, n)
    tile = pl.BlockSpec((tm, c), lambda i: (i, 0))
    const = lambda i: (0, 0)
    return pl.pallas_call(
        _gate_proj_ln_kernel,
        grid=(n // tm,),
        in_specs=[tile, tile, tile, pl.BlockSpec(w.shape, const),
                  pl.BlockSpec((1, c), const), pl.BlockSpec((1, c), const)],
        out_specs=tile,
        out_shape=jax.ShapeDtypeStruct((n, c), F32),
        compiler_params=_cparams(("parallel",)),
        name="gate_proj_ln",
    )(x2, y2, gate2, w.astype(BF16), g.reshape(1, c), b.reshape(1, c))


def _s5_tables(a_re, a_im, log_dt, b_re, b_im, c_re, c_im):
    g, p = a_re.shape
    s = b_re.shape[-1]
    L = S5_CHUNK
    dt = jnp.exp(log_dt.astype(F32))[:, None]
    lam_re = jnp.minimum(a_re.astype(F32), -1e-4)
    lam_im = a_im.astype(F32)

    def apow(n):
        n = jnp.asarray(n, F32)[:, None, None]
        mag = jnp.exp(n * dt * lam_re)
        return mag * jnp.cos(n * dt * lam_im), mag * jnp.sin(n * dt * lam_im)

    ab_re, ab_im = apow([1])
    ab_re, ab_im = ab_re[0], ab_im[0]
    den = lam_re * lam_re + lam_im * lam_im
    nr, ni = ab_re - 1.0, ab_im
    coef_re = (nr * lam_re + ni * lam_im) / den
    coef_im = (ni * lam_re - nr * lam_im) / den
    bb_re = coef_re[..., None] * b_re - coef_im[..., None] * b_im
    bb_im = coef_re[..., None] * b_im + coef_im[..., None] * b_re
    cr, ci = c_re.astype(F32), c_im.astype(F32)

    pw_re, pw_im = apow(jnp.arange(L))
    cb_re = (jnp.einsum('gcp,tgp,gpd->gtcd', cr, pw_re, bb_re) - jnp.einsum('gcp,tgp,gpd->gtcd', cr, pw_im, bb_im)
             - jnp.einsum('gcp,tgp,gpd->gtcd', ci, pw_re, bb_im) - jnp.einsum('gcp,tgp,gpd->gtcd', ci, pw_im, bb_re))
    def group_mask(n_groups, n_cols, col_group_size):
        col_group = (jnp.arange(n_cols) // col_group_size) % n_groups
        return (jnp.arange(n_groups)[:, None] == col_group[None, :]).astype(F32)[None, None, :, None, :]

    gs = SLAB // s
    n_slab = g // gs
    m_lag = cb_re.reshape(n_slab, gs, L, s, s).transpose(0, 2, 4, 1, 3).reshape(n_slab, L, s, SLAB)
    w_lag = (m_lag[:, :, None, :, :] * group_mask(gs, SLAB, s)).astype(BF16).reshape(n_slab, L * SLAB, SLAB)

    gt = LANES // s
    n_t = g // gt
    st = 2 * gt * p
    rv_re, rv_im = pw_re[::-1], pw_im[::-1]
    in_re = rv_re[..., None] * bb_re[None] - rv_im[..., None] * bb_im[None]
    in_im = rv_re[..., None] * bb_im[None] + rv_im[..., None] * bb_re[None]
    m_in = jnp.stack([in_re, in_im], axis=0).reshape(2, L, n_t, gt, p, s)
    m_in = m_in.transpose(2, 1, 5, 0, 3, 4).reshape(n_t, L, s, st)
    b_cat = (m_in[:, :, None, :, :] * group_mask(gt, st, p)).astype(BF16).reshape(n_t, L * LANES, st)
    p1_re, p1_im = apow(jnp.arange(1, L + 1))
    out_re = cr[None] * p1_re[:, :, None, :] - ci[None] * p1_im[:, :, None, :]
    out_im = cr[None] * p1_im[:, :, None, :] + ci[None] * p1_re[:, :, None, :]
    m_out = jnp.stack([out_re, -out_im], axis=0).reshape(2, L, n_t, gt, s, p)
    m_out = m_out.transpose(2, 0, 5, 1, 3, 4).reshape(n_t, 2, p, L * LANES)
    c_cat = (m_out[:, :, None, :, :] * group_mask(gt, L * LANES, s)).astype(BF16).reshape(n_t, st, L * LANES)

    def lay_vec(re, im):
        n = re.shape[0]
        f = lambda m: m.reshape(n, n_t, gt * p).transpose(1, 0, 2)
        return jnp.concatenate([f(re), f(im)], axis=-1)

    step_pow = lay_vec(*apow([L * d for d in _SUBLANE_STEPS]))
    carry_pow = lay_vec(*apow(L * jnp.arange(1, SUBLANES + 1)))
    return w_lag, b_cat, c_cat, step_pow, carry_pow


def _cmul(ar, ai, br, bi):
    return ar * br - ai * bi, ar * bi + ai * br


def _s5_kernel(u_ref, wlag_ref, bcat_ref, ccat_ref, spow_ref, cpow_ref, dskip_ref, y_ref,
               carry_ref, ut_ref, zt_ref):
    tm = u_ref.shape[1]
    L = S5_CHUNK
    assert L == SUBLANES
    n_rows = tm // L
    half = bcat_ref.shape[2] // 2

    @pl.when(pl.program_id(2) == 0)
    def _():
        carry_ref[...] = jnp.zeros_like(carry_ref)

    u = u_ref[0]
    sub3 = lax.broadcasted_iota(jnp.int32, (1, SUBLANES, 1), 1)
    u3 = u.reshape(tm // SUBLANES, SUBLANES, SLAB)
    shifted = [u.astype(BF16)]
    for tau in range(1, L):
        sh = jnp.where(sub3 >= tau, pltpu.roll(u3, tau, axis=1), 0.0)
        shifted.append(sh.reshape(tm, SLAB).astype(BF16))
    y_lag = jnp.dot(jnp.concatenate(shifted, axis=1), wlag_ref[0], preferred_element_type=F32)

    n_blk = n_rows // SUBLANES
    sub2 = lax.broadcasted_iota(jnp.int32, (SUBLANES, 1), 0)
    for t in range(SLAB // LANES):
        ut_ref[t] = u[:, t * LANES:(t + 1) * LANES]
        ucat = jnp.concatenate([ut_ref[t, pl.ds(s, n_rows, stride=L), :] for s in range(L)], axis=1)
        x = _dot(ucat, bcat_ref[t])
        hr = x[:, :half].reshape(n_blk, SUBLANES, half)
        hi = x[:, half:].reshape(n_blk, SUBLANES, half)
        for i, d in enumerate(_SUBLANE_STEPS):
            pr, pi = spow_ref[t, i:i + 1, :half], spow_ref[t, i:i + 1, half:]
            sr, si = _cmul(pr, pi, pltpu.roll(hr, d, axis=1), pltpu.roll(hi, d, axis=1))
            hr = hr + jnp.where(sub3 >= d, sr, 0.0)
            hi = hi + jnp.where(sub3 >= d, si, 0.0)
        cr, ci = carry_ref[t, :, :half], carry_ref[t, :, half:]
        cp_r, cp_i = cpow_ref[t, :, :half], cpow_ref[t, :, half:]
        prev_r, prev_i = [], []
        for kb in range(n_blk):
            kr, ki = _cmul(cp_r, cp_i, cr, ci)
            br, bi = hr[kb] + kr, hi[kb] + ki
            prev_r.append(jnp.where(sub2 == 0, cr, pltpu.roll(br, 1, axis=0)))
            prev_i.append(jnp.where(sub2 == 0, ci, pltpu.roll(bi, 1, axis=0)))
            cr, ci = br[SUBLANES - 1:SUBLANES, :], bi[SUBLANES - 1:SUBLANES, :]
        carry_ref[t, :, :half] = cr
        carry_ref[t, :, half:] = ci
        prev = jnp.concatenate([jnp.concatenate(prev_r, axis=0), jnp.concatenate(prev_i, axis=0)], axis=1)
        z = _dot(prev, ccat_ref[t])
        for s in range(L):
            zt_ref[t, pl.ds(s, n_rows, stride=L), :] = z[:, s * LANES:(s + 1) * LANES]

    y_state = jnp.concatenate([zt_ref[t] for t in range(SLAB // LANES)], axis=1)
    y = y_lag + y_state + u * dskip_ref[...]
    y = 0.5 * y * (1.0 + jnp.tanh(math.sqrt(2.0 / math.pi) * (y + 0.044715 * (y * y * y))))
    y_ref[0] = y.astype(y_ref.dtype)


def _s5_conv(h, tables, d_skip, *, tm=1024):
    bsz, seq, c = h.shape
    tm = min(tm, seq)
    w_lag, b_cat, c_cat, step_pow, carry_pow = tables
    n_slab = c // SLAB
    tps = SLAB // LANES
    st = b_cat.shape[2]
    return pl.pallas_call(
        _s5_kernel,
        grid=(n_slab, bsz, seq // tm),
        in_specs=[
            pl.BlockSpec((1, tm, SLAB), lambda s, b, i: (b, i, s)),
            pl.BlockSpec((1,) + w_lag.shape[1:], lambda s, b, i: (s, 0, 0)),
            pl.BlockSpec((tps,) + b_cat.shape[1:], lambda s, b, i: (s, 0, 0)),
            pl.BlockSpec((tps,) + c_cat.shape[1:], lambda s, b, i: (s, 0, 0)),
            pl.BlockSpec((tps,) + step_pow.shape[1:], lambda s, b, i: (s, 0, 0)),
            pl.BlockSpec((tps,) + carry_pow.shape[1:], lambda s, b, i: (s, 0, 0)),
            pl.BlockSpec((1, SLAB), lambda s, b, i: (0, s)),
        ],
        out_specs=pl.BlockSpec((1, tm, SLAB), lambda s, b, i: (b, i, s)),
        out_shape=jax.ShapeDtypeStruct((bsz, seq, c), BF16),
        scratch_shapes=[pltpu.VMEM((tps, 1, st), F32), pltpu.VMEM((tps, tm, LANES), F32),
                        pltpu.VMEM((tps, tm, LANES), F32)],
        compiler_params=_cparams(("parallel", "parallel", "arbitrary")),
        name="s5_conv",
    )(h, w_lag, b_cat, c_cat, step_pow, carry_pow, d_skip.reshape(1, c))


def _glu_ln_kernel(x_ref, y_ref, w_ref, g_ref, b_ref, o_ref):
    x = x_ref[...]
    c = x.shape[1]
    z = jnp.dot(y_ref[...], w_ref[...], preferred_element_type=F32)
    mix = z[:, :c] * jax.nn.sigmoid(z[:, c:])
    o_ref[...] = _layer_norm(DN_ALPHA * x + mix, g_ref[...], b_ref[...])


def _glu_ln(x2, y2, w, g, b, *, tm=512):
    n, c = x2.shape
    tm = min(tm, n)
    tile = pl.BlockSpec((tm, c), lambda i: (i, 0))
    const = lambda i: (0, 0)
    return pl.pallas_call(
        _glu_ln_kernel,
        grid=(n // tm,),
        in_specs=[tile, tile, pl.BlockSpec(w.shape, const),
                  pl.BlockSpec((1, c), const), pl.BlockSpec((1, c), const)],
        out_specs=tile,
        out_shape=jax.ShapeDtypeStruct((n, c), F32),
        compiler_params=_cparams(("parallel",)),
        name="glu_ln",
    )(x2, y2, w.astype(BF16), g.reshape(1, c), b.reshape(1, c))


def _rwkv_layer(h, ln_g, ln_b, mu, w0, w1, w2, a0, a1, a2, g1, g2, k_k, k_a, r_k, wr, wk, wv, wo,
                lnx_g, lnx_b):
    bsz, seq, c = h.shape
    zeros = jnp.zeros((6, c), F32)
    vecs = jnp.concatenate([mu, w0[None], a0[None], k_k[None], k_a[None], zeros], axis=0)
    scan_vecs = jnp.concatenate([r_k.reshape(1, c), lnx_g[None], lnx_b[None], zeros[:5]], axis=0)
    idx = jnp.arange(SLAB) // HEAD
    jmat = (idx[:, None] == idx[None, :]).astype(BF16)
    r, k, v, lw, an, bn, gate = _rw_proj(h, vecs, wr, wk, wv, w1, w2, a1, a2, g1, g2, jmat)
    y = _rw_scan(r, k, v, lw, an, bn, scan_vecs, jmat)
    n = bsz * seq
    out = _gate_proj_ln(h.reshape(n, c), y.reshape(n, c), gate.reshape(n, c), wo, ln_g, ln_b)
    return out.reshape(bsz, seq, c)


def _s5_layer(h, ln_g, ln_b, a_re, a_im, log_dt, b_re, b_im, c_re, c_im, d_skip, w_glu, *, tm=1024):
    bsz, seq, c = h.shape
    tm = min(tm, seq)
    tables = _s5_tables(a_re, a_im, log_dt, b_re, b_im, c_re, c_im)
    y = _s5_conv(h, tables, d_skip, tm=tm)
    n = bsz * seq
    out = _glu_ln(h.reshape(n, c), y.reshape(n, c), w_glu, ln_g, ln_b)
    return out.reshape(bsz, seq, c)


def _mlp_layer(h, w1, w2, g, b):
    bsz, seq, c = h.shape
    return _mlp_ln(h.reshape(bsz * seq, c), w1, w2, g, b).reshape(bsz, seq, c)


def kernel(x, ln_g, ln_b, rw_mu, rw_w0, rw_w1, rw_w2, rw_a0, rw_a1, rw_a2, rw_g1, rw_g2, rw_k_k, rw_k_a, rw_r_k, rw_wr, rw_wk, rw_wv, rw_wo, rw_lnx_g, rw_lnx_b, s5_a_re, s5_a_im, s5_log_dt, s5_b_re, s5_b_im, s5_c_re, s5_c_im, s5_d, s5_w_glu, mlp_w1, mlp_w2):
    h = x
    h = _rwkv_layer(h, ln_g[0], ln_b[0], rw_mu[0], rw_w0[0], rw_w1[0], rw_w2[0], rw_a0[0], rw_a1[0],
                    rw_a2[0], rw_g1[0], rw_g2[0], rw_k_k[0], rw_k_a[0], rw_r_k[0], rw_wr[0], rw_wk[0],
                    rw_wv[0], rw_wo[0], rw_lnx_g[0], rw_lnx_b[0])
    h = _mlp_layer(h, mlp_w1[0], mlp_w2[0], ln_g[1], ln_b[1])
    h = _s5_layer(h, ln_g[2], ln_b[2], s5_a_re[0], s5_a_im[0], s5_log_dt[0], s5_b_re[0], s5_b_im[0],
                  s5_c_re[0], s5_c_im[0], s5_d[0], s5_w_glu[0])
    h = _mlp_layer(h, mlp_w1[1], mlp_w2[1], ln_g[3], ln_b[3])
    return h
```

```python
import functools
import math

import jax
import jax.numpy as jnp
from jax import lax
from jax.experimental import pallas as pl
from jax.experimental.pallas import tpu as pltpu

F32 = jnp.float32
BF16 = jnp.bfloat16

DEPTH = 2
DN_ALPHA = (2.0 * DEPTH) ** 0.25
LN_EPS = 1e-5
GN_EPS = 64e-5

LANES = 128
SUBLANES = 8
_SUBLANE_STEPS = (1, 2, 4)
HEAD = 64
PAIR = 2 * HEAD
SLAB = 256
RW_CHUNK = 64
S5_CHUNK = 8
VMEM_LIMIT = 56 * 1024 * 1024


def _cparams(sem):
    return pltpu.CompilerParams(dimension_semantics=sem, vmem_limit_bytes=VMEM_LIMIT)


def _layer_norm(z, g, b):
    mu = jnp.mean(z, axis=-1, keepdims=True)
    d = z - mu
    var = jnp.mean(d * d, axis=-1, keepdims=True)
    return d * lax.rsqrt(var + LN_EPS) * g + b


def _dot(a, b):
    return jnp.dot(a.astype(BF16), b.astype(BF16), preferred_element_type=F32)


def _dot_t(a, b):
    return lax.dot_general(a.astype(BF16), b.astype(BF16), (((1,), (1,)), ((), ())),
                           preferred_element_type=F32)


def _tdot(a, b):
    return lax.dot_general(a.astype(BF16), b.astype(BF16), (((0,), (0,)), ((), ())),
                           preferred_element_type=F32)


MLP_FF_CHUNK = 1024


def _mlp_ln_value(x, w1_ref, w2_ref, g_ref, b_ref):
    xb = x.astype(BF16)
    d_ff = w1_ref.shape[1]
    chunk = min(MLP_FF_CHUNK, d_ff)
    acc = jnp.zeros(x.shape, F32)
    for c in range(d_ff // chunk):
        h = jnp.dot(xb, w1_ref[:, c * chunk:(c + 1) * chunk], preferred_element_type=F32)
        h = jnp.square(jnp.maximum(h, 0.0))
        acc = acc + jnp.dot(h.astype(BF16), w2_ref[c * chunk:(c + 1) * chunk, :], preferred_element_type=F32)
    return _layer_norm(DN_ALPHA * x + acc, g_ref[...], b_ref[...])


def _gate_mlp_kernel(x_ref, y_ref, gate_ref, wo_ref, g0_ref, b0_ref, w1_ref, w2_ref, g1_ref, b1_ref, o_ref):
    x = x_ref[...]
    mix = _dot(y_ref[...].astype(F32) * gate_ref[...].astype(F32), wo_ref[...])
    h = _layer_norm(DN_ALPHA * x + mix, g0_ref[...], b0_ref[...])
    o_ref[...] = _mlp_ln_value(h, w1_ref, w2_ref, g1_ref, b1_ref)


def _glu_mlp_kernel(x_ref, y_ref, wglu_ref, g0_ref, b0_ref, w1_ref, w2_ref, g1_ref, b1_ref, o_ref):
    x = x_ref[...]
    c = x.shape[1]
    z = jnp.dot(y_ref[...], wglu_ref[...], preferred_element_type=F32)
    mix = z[:, :c] * jax.nn.sigmoid(z[:, c:])
    h = _layer_norm(DN_ALPHA * x + mix, g0_ref[...], b0_ref[...])
    o_ref[...] = _mlp_ln_value(h, w1_ref, w2_ref, g1_ref, b1_ref)


def _mixer_out_mlp(body, name, row_inputs, w_mix, ln0, w1, w2, ln1, *, tm=512):
    n, c = row_inputs[0].shape
    tm = min(tm, n)
    tile = pl.BlockSpec((tm, c), lambda i: (i, 0))
    const = lambda i: (0, 0)

    def resident(w):
        return pl.BlockSpec(w.shape, const, pipeline_mode=pl.Buffered(1))

    vec = pl.BlockSpec((1, c), const)
    ws = [w.astype(BF16) for w in (w_mix, w1, w2)]
    return pl.pallas_call(
        body,
        grid=(n // tm,),
        in_specs=[tile] * len(row_inputs) + [resident(ws[0]), vec, vec, resident(ws[1]), resident(ws[2]), vec, vec],
        out_specs=tile,
        out_shape=jax.ShapeDtypeStruct((n, c), F32),
        compiler_params=_cparams(("parallel",)),
        name=name,
    )(*row_inputs, ws[0], ln0[0].reshape(1, c), ln0[1].reshape(1, c), ws[1], ws[2],
      ln1[0].reshape(1, c), ln1[1].reshape(1, c))


_V_MU, _V_W0, _V_A0, _V_KK, _V_KA = 0, 6, 7, 8, 9


def _head_sum(x, j_ref):
    parts = [_dot(x[:, s:s + SLAB], j_ref[...]) for s in range(0, x.shape[1], SLAB)]
    return jnp.concatenate(parts, axis=1)


def _rw_proj_kernel(x_ref, xp_ref, vec_ref, wr_ref, wk_ref, wv_ref, w1_ref, w2_ref, a1_ref, a2_ref,
                    g1_ref, g2_ref, j_ref, r_ref, k_ref, v_ref, lw_ref, an_ref, bn_ref, g_ref):
    x = x_ref[0]
    tm = x.shape[0]
    row = lax.broadcasted_iota(jnp.int32, (tm, 1), 0)
    prev_last = jnp.where(pl.program_id(1) == 0, 0.0, xp_ref[0][7:8, :])
    xprev = jnp.where(row == 0, prev_last, pltpu.roll(x, 1, axis=0))
    xx = xprev - x

    def mix(i):
        return x + xx * vec_ref[_V_MU + i:_V_MU + i + 1, :]

    r = _dot(mix(0), wr_ref[...])
    w_pre = vec_ref[_V_W0:_V_W0 + 1, :] + _dot(jnp.tanh(_dot(mix(1), w1_ref[...])), w2_ref[...])
    k = _dot(mix(2), wk_ref[...])
    v = _dot(mix(3), wv_ref[...])
    a = jax.nn.sigmoid(vec_ref[_V_A0:_V_A0 + 1, :] + _dot(_dot(mix(4), a1_ref[...]), a2_ref[...]))
    g = _dot(jax.nn.sigmoid(_dot(mix(5), g1_ref[...])), g2_ref[...])

    kk = k * vec_ref[_V_KK:_V_KK + 1, :]
    kk = kk / jnp.maximum(jnp.sqrt(_head_sum(kk * kk, j_ref)), 1e-12)
    r_ref[0] = r.astype(r_ref.dtype)
    k_ref[0] = (k * (1.0 + (a - 1.0) * vec_ref[_V_KA:_V_KA + 1, :])).astype(k_ref.dtype)
    v_ref[0] = v.astype(v_ref.dtype)
    lw_ref[0] = (-math.exp(-0.5)) * jax.nn.sigmoid(w_pre)
    an_ref[0] = (-kk).astype(an_ref.dtype)
    bn_ref[0] = (kk * a).astype(bn_ref.dtype)
    g_ref[0] = g.astype(g_ref.dtype)


def _rw_proj(x, vecs, wr, wk, wv, w1, w2, a1, a2, g1, g2, jmat, *, tm=512):
    bsz, seq, c = x.shape
    tm = min(tm, seq)
    const = lambda b, i: (0, 0)
    tile = pl.BlockSpec((1, tm, c), lambda b, i: (b, i, 0))
    prev = pl.BlockSpec((1, 8, c), lambda b, i: (b, jnp.maximum(i * (tm // 8) - 1, 0), 0))

    def full(w):
        return pl.BlockSpec(w.shape, const)

    ws = [w.astype(BF16) for w in (wr, wk, wv, w1, w2, a1, a2, g1, g2)]
    out = lambda dt: jax.ShapeDtypeStruct((bsz, seq, c), dt)
    return pl.pallas_call(
        _rw_proj_kernel,
        grid=(bsz, seq // tm),
        in_specs=[tile, prev, full(vecs)] + [full(w) for w in ws] + [full(jmat)],
        out_specs=[tile] * 7,
        out_shape=[out(BF16), out(BF16), out(BF16), out(F32), out(BF16), out(BF16), out(BF16)],
        compiler_params=_cparams(("parallel", "arbitrary")),
        name="rwkv_proj",
    )(x, x, vecs, *ws, jmat)


def _rw_scan_kernel(r_ref, k_ref, v_ref, lw_ref, an_ref, bn_ref, vec_ref, y_ref, state_ref):
    nb, tm, c = r_ref.shape
    n_pairs = c // PAIR
    L = RW_CHUNK

    @pl.when(pl.program_id(1) == 0)
    def _():
        state_ref[...] = jnp.zeros_like(state_ref)

    sub3 = lax.broadcasted_iota(jnp.int32, (1, SUBLANES, 1), 1)

    def chunk_cumsum(x):
        x3 = x.reshape(L // SUBLANES, SUBLANES, x.shape[1])
        for d in _SUBLANE_STEPS:
            x3 = x3 + jnp.where(sub3 >= d, pltpu.roll(x3, d, axis=1), 0.0)
        blocks, run = [], None
        for kb in range(L // SUBLANES):
            blk = x3[kb] if run is None else x3[kb] + run
            blocks.append(blk)
            run = blk[SUBLANES - 1:SUBLANES, :]
        return jnp.concatenate(blocks, axis=0)

    ri = lax.broadcasted_iota(jnp.int32, (PAIR, PAIR), 0)
    ci = lax.broadcasted_iota(jnp.int32, (PAIR, PAIR), 1)
    same_head = (ri // L) == (ci // L)
    mask_strict = same_head & ((ci % L) < (ri % L))
    mask_incl = same_head & ((ci % L) <= (ri % L))
    eye = (ri == ci).astype(F32)
    lane = lax.broadcasted_iota(jnp.int32, (L, PAIR), 1)
    head0 = lane < HEAD
    ri_h = lax.broadcasted_iota(jnp.int32, (PAIR // 2, PAIR), 0)
    ci_h = lax.broadcasted_iota(jnp.int32, (PAIR // 2, PAIR), 1)

    def level_mask(dd):
        return same_head & ((ri // (2 * dd)) == (ci // (2 * dd))) & ((ri % (2 * dd)) >= dd) & ((ci % (2 * dd)) < dd)

    def sel(res):
        return jnp.where(head0, res[:L], res[L:])

    def dup(x):
        return jnp.concatenate([x, x], axis=0)

    def head_sum(x):
        s0 = jnp.sum(jnp.where(head0, x, 0.0), axis=1, keepdims=True)
        s1 = jnp.sum(jnp.where(head0, 0.0, x), axis=1, keepdims=True)
        return jnp.where(head0, s0, s1)

    def take_rows(x, dd, half):
        return jnp.concatenate([x[(2 * b + half) * dd:(2 * b + half + 1) * dd]
                                for b in range(PAIR // (2 * dd))], axis=0)

    def merge_rows(first, second, dd):
        parts = []
        for b in range(PAIR // (2 * dd)):
            parts += [first[b * dd:(b + 1) * dd], second[b * dd:(b + 1) * dd]]
        return jnp.concatenate(parts, axis=0)

    n_chunks = tm // L
    per_chunk = nb * n_pairs
    probs = range(n_chunks * per_chunk)
    rows = [slice((p // per_chunk) * L, (p // per_chunk + 1) * L) for p in probs]
    seq = [(p % per_chunk) // n_pairs for p in probs]
    lanes = [slice((p % n_pairs) * PAIR, (p % n_pairs + 1) * PAIR) for p in probs]

    def tile(ref, p):
        return ref[seq[p], rows[p], lanes[p]]

    rhs, ar, bk, vv, e_l = [], [], [], [], []
    for p in probs:
        r = tile(r_ref, p).astype(F32)
        k = tile(k_ref, p).astype(F32)
        a = tile(an_ref, p).astype(F32)
        b = tile(bn_ref, p).astype(F32)
        lw = tile(lw_ref, p)
        cs_c = chunk_cumsum(lw)
        cs_last = cs_c[L - 1:L, :]
        a_t = a * jnp.exp(cs_c - lw)
        r_t = r * jnp.exp(cs_c)
        e_m = jnp.exp(-cs_c)
        b_t = b * e_m
        k_t = k * e_m
        e_d = jnp.exp(cs_last - cs_c)
        rhs.append(jnp.concatenate([jnp.where(head0, b_t, 0.0), jnp.where(head0, 0.0, b_t),
                                    jnp.where(head0, k_t, 0.0), jnp.where(head0, 0.0, k_t)], axis=0).astype(BF16))
        ar.append(jnp.concatenate([a_t, r_t], axis=0).astype(BF16))
        bk.append(jnp.concatenate([b * e_d, k * e_d], axis=0).astype(BF16))
        vv.append(dup(tile(v_ref, p)))
        e_l.append(jnp.exp(cs_last))

    sc = [_dot_t(ar[p], rhs[p]) for p in probs]
    n_ab = [jnp.where(mask_strict, dup(sc[p][:L, :PAIR]), 0.0) for p in probs]
    n_ak = [jnp.where(mask_strict, dup(sc[p][:L, PAIR:]), 0.0).astype(BF16) for p in probs]
    n_r = [jnp.concatenate([jnp.where(mask_incl, dup(sc[p][L:, :PAIR]), 0.0),
                            jnp.where(mask_incl, dup(sc[p][L:, PAIR:]), 0.0)], axis=1).astype(BF16) for p in probs]
    w0 = [sel(_dot(n_ak[p], vv[p])) for p in probs]

    t_inv = [eye + jnp.where(level_mask(1), n_ab[p], 0.0) for p in probs]
    n_ab_b = [n_ab[p].astype(BF16) for p in probs]
    dd = 2
    while dd < SUBLANES:
        lm = level_mask(dd)
        nt = [jnp.where(lm, _dot(n_ab_b[p], t_inv[p]), 0.0) for p in probs]
        t_inv = [t_inv[p] + _dot(t_inv[p], nt[p]) for p in probs]
        dd *= 2
    zero_half = jnp.zeros((PAIR // 2, PAIR), F32)
    while dd < L:
        lm_hi = ((ri_h // dd) == (ci_h // (2 * dd))) & ((ci_h % (2 * dd)) < dd)
        nt = [merge_rows(zero_half, jnp.where(lm_hi, _dot(take_rows(n_ab[p], dd, 1), t_inv[p]), 0.0), dd)
              for p in probs]
        t_hi = [take_rows(t_inv[p], dd, 1) for p in probs]
        t_inv = [merge_rows(take_rows(t_inv[p], dd, 0), t_hi[p] + _dot(t_hi[p], nt[p]), dd) for p in probs]
        dd *= 2
    t_inv = [t_inv[p].astype(BF16) for p in probs]

    for ch in range(n_chunks):
        cur = range(ch * per_chunk, (ch + 1) * per_chunk)
        slot = {p: p - ch * per_chunk for p in cur}
        s_prev = {p: state_ref[slot[p]] for p in cur}
        ars = {p: _dot_t(ar[p], s_prev[p]) for p in cur}
        u = {p: sel(_dot(t_inv[p], dup(ars[p][:L] + w0[p]))) for p in cur}
        o = {p: ars[p][L:] + sel(_dot(n_r[p], jnp.concatenate([dup(u[p]).astype(BF16), vv[p]], axis=0))) for p in cur}
        for p in cur:
            upd = _tdot(jnp.concatenate([u[p].astype(BF16), vv[p][:L]], axis=0), bk[p])
            state_ref[slot[p]] = s_prev[p] * e_l[p] + jnp.where(same_head, upd, 0.0)
        for p in cur:
            dlt = o[p] - head_sum(o[p]) * (1.0 / HEAD)
            var = head_sum(dlt * dlt) * (1.0 / HEAD)
            rk = tile(r_ref, p).astype(F32) * tile(k_ref, p).astype(F32)
            bonus = head_sum(rk * vec_ref[0:1, lanes[p]]) * tile(v_ref, p).astype(F32)
            y = dlt * lax.rsqrt(var + GN_EPS) * vec_ref[1:2, lanes[p]] + vec_ref[2:3, lanes[p]] + bonus
            y_ref[seq[p], rows[p], lanes[p]] = y.astype(y_ref.dtype)


def _rw_scan(r, k, v, lw, an, bn, vecs, *, tm=128, nb=4):
    bsz, seq, c = r.shape
    tm = min(tm, seq)
    nb = math.gcd(nb, bsz)
    tile = pl.BlockSpec((nb, tm, c), lambda b, i: (b, i, 0))
    const = lambda b, i: (0, 0)
    return pl.pallas_call(
        _rw_scan_kernel,
        grid=(bsz // nb, seq // tm),
        in_specs=[tile] * 6 + [pl.BlockSpec(vecs.shape, const)],
        out_specs=tile,
        out_shape=jax.ShapeDtypeStruct((bsz, seq, c), BF16),
        scratch_shapes=[pltpu.VMEM((nb * (c // PAIR), PAIR, PAIR), F32)],
        compiler_params=_cparams(("parallel", "arbitrary")),
        name="rwkv_scan",
    )(r, k, v, lw, an, bn, vecs)


def _s5_tables(a_re, a_im, log_dt, b_re, b_im, c_re, c_im):
    g, p = a_re.shape
    s = b_re.shape[-1]
    L = S5_CHUNK
    dt = jnp.exp(log_dt.astype(F32))[:, None]
    lam_re = jnp.minimum(a_re.astype(F32), -1e-4)
    lam_im = a_im.astype(F32)

    def apow(n):
        n = jnp.asarray(n, F32)[:, None, None]
        mag = jnp.exp(n * dt * lam_re)
        return mag * jnp.cos(n * dt * lam_im), mag * jnp.sin(n * dt * lam_im)

    ab_re, ab_im = apow([1])
    ab_re, ab_im = ab_re[0], ab_im[0]
    den = lam_re * lam_re + lam_im * lam_im
    nr, ni = ab_re - 1.0, ab_im
    coef_re = (nr * lam_re + ni * lam_im) / den
    coef_im = (ni * lam_re - nr * lam_im) / den
    bb_re = coef_re[..., None] * b_re - coef_im[..., None] * b_im
    bb_im = coef_re[..., None] * b_im + coef_im[..., None] * b_re
    cr, ci = c_re.astype(F32), c_im.astype(F32)

    pw_re, pw_im = apow(jnp.arange(L))
    cb_re = (jnp.einsum('gcp,tgp,gpd->gtcd', cr, pw_re, bb_re) - jnp.einsum('gcp,tgp,gpd->gtcd', cr, pw_im, bb_im)
             - jnp.einsum('gcp,tgp,gpd->gtcd', ci, pw_re, bb_im) - jnp.einsum('gcp,tgp,gpd->gtcd', ci, pw_im, bb_re))
    def group_mask(n_groups, n_cols, col_group_size):
        col_group = (jnp.arange(n_cols) // col_group_size) % n_groups
        return (jnp.arange(n_groups)[:, None] == col_group[None, :]).astype(F32)[None, None, :, None, :]

    gs = SLAB // s
    n_slab = g // gs
    m_lag = cb_re.reshape(n_slab, gs, L, s, s).transpose(0, 2, 4, 1, 3).reshape(n_slab, L, s, SLAB)
    w_lag = (m_lag[:, :, None, :, :] * group_mask(gs, SLAB, s)).astype(BF16).reshape(n_slab, L * SLAB, SLAB)

    gt = LANES // s
    n_t = g // gt
    st = 2 * gt * p
    rv_re, rv_im = pw_re[::-1], pw_im[::-1]
    in_re = rv_re[..., None] * bb_re[None] - rv_im[..., None] * bb_im[None]
    in_im = rv_re[..., None] * bb_im[None] + rv_im[..., None] * bb_re[None]
    m_in = jnp.stack([in_re, in_im], axis=0).reshape(2, L, n_t, gt, p, s)
    m_in = m_in.transpose(2, 1, 5, 0, 3, 4).reshape(n_t, L, s, st)
    b_cat = (m_in[:, :, None, :, :] * group_mask(gt, st, p)).astype(BF16).reshape(n_t, L * LANES, st)
    p1_re, p1_im = apow(jnp.arange(1, L + 1))
    out_re = cr[None] * p1_re[:, :, None, :] - ci[None] * p1_im[:, :, None, :]
    out_im = cr[None] * p1_im[:, :, None, :] + ci[None] * p1_re[:, :, None, :]
    m_out = jnp.stack([out_re, -out_im], axis=0).reshape(2, L, n_t, gt, s, p)
    m_out = m_out.transpose(2, 0, 5, 1, 3, 4).reshape(n_t, 2, p, L * LANES)
    c_cat = (m_out[:, :, None, :, :] * group_mask(gt, L * LANES, s)).astype(BF16).reshape(n_t, st, L * LANES)

    def lay_vec(re, im):
        n = re.shape[0]
        f = lambda m: m.reshape(n, n_t, gt * p).transpose(1, 0, 2)
        return jnp.concatenate([f(re), f(im)], axis=-1)

    step_pow = lay_vec(*apow([L * d for d in _SUBLANE_STEPS]))
    carry_pow = lay_vec(*apow(L * jnp.arange(1, SUBLANES + 1)))
    return w_lag, b_cat, c_cat, step_pow, carry_pow


def _cmul(ar, ai, br, bi):
    return ar * br - ai * bi, ar * bi + ai * br


def _s5_kernel(u_ref, wlag_ref, bcat_ref, ccat_ref, spow_ref, cpow_ref, dskip_ref, y_ref,
               carry_ref, ut_ref, zt_ref):
    tm = u_ref.shape[1]
    L = S5_CHUNK
    assert L == SUBLANES
    n_rows = tm // L
    half = bcat_ref.shape[2] // 2

    @pl.when(pl.program_id(2) == 0)
    def _():
        carry_ref[...] = jnp.zeros_like(carry_ref)

    u = u_ref[0]
    sub3 = lax.broadcasted_iota(jnp.int32, (1, SUBLANES, 1), 1)
    u3 = u.reshape(tm // SUBLANES, SUBLANES, SLAB)
    shifted = [u.astype(BF16)]
    for tau in range(1, L):
        sh = jnp.where(sub3 >= tau, pltpu.roll(u3, tau, axis=1), 0.0)
        shifted.append(sh.reshape(tm, SLAB).astype(BF16))
    y_lag = jnp.dot(jnp.concatenate(shifted, axis=1), wlag_ref[0], preferred_element_type=F32)

    n_blk = n_rows // SUBLANES
    sub2 = lax.broadcasted_iota(jnp.int32, (SUBLANES, 1), 0)
    for t in range(SLAB // LANES):
        ut_ref[t] = u[:, t * LANES:(t + 1) * LANES]
        ucat = jnp.concatenate([ut_ref[t, pl.ds(s, n_rows, stride=L), :] for s in range(L)], axis=1)
        x = _dot(ucat, bcat_ref[t])
        hr = x[:, :half].reshape(n_blk, SUBLANES, half)
        hi = x[:, half:].reshape(n_blk, SUBLANES, half)
        for i, d in enumerate(_SUBLANE_STEPS):
            pr, pi = spow_ref[t, i:i + 1, :half], spow_ref[t, i:i + 1, half:]
            sr, si = _cmul(pr, pi, pltpu.roll(hr, d, axis=1), pltpu.roll(hi, d, axis=1))
            hr = hr + jnp.where(sub3 >= d, sr, 0.0)
            hi = hi + jnp.where(sub3 >= d, si, 0.0)
        cr, ci = carry_ref[t, :, :half], carry_ref[t, :, half:]
        cp_r, cp_i = cpow_ref[t, :, :half], cpow_ref[t, :, half:]
        prev_r, prev_i = [], []
        for kb in range(n_blk):
            kr, ki = _cmul(cp_r, cp_i, cr, ci)
            br, bi = hr[kb] + kr, hi[kb] + ki
            prev_r.append(jnp.where(sub2 == 0, cr, pltpu.roll(br, 1, axis=0)))
            prev_i.append(jnp.where(sub2 == 0, ci, pltpu.roll(bi, 1, axis=0)))
            cr, ci = br[SUBLANES - 1:SUBLANES, :], bi[SUBLANES - 1:SUBLANES, :]
        carry_ref[t, :, :half] = cr
        carry_ref[t, :, half:] = ci
        prev = jnp.concatenate([jnp.concatenate(prev_r, axis=0), jnp.concatenate(prev_i, axis=0)], axis=1)
        z = _dot(prev, ccat_ref[t])
        for s in range(L):
            zt_ref[t, pl.ds(s, n_rows, stride=L), :] = z[:, s * LANES:(s + 1) * LANES]

    y_state = jnp.concatenate([zt_ref[t] for t in range(SLAB // LANES)], axis=1)
    y = y_lag + y_state + u * dskip_ref[...]
    y = 0.5 * y * (1.0 + jnp.tanh(math.sqrt(2.0 / math.pi) * (y + 0.044715 * (y * y * y))))
    y_ref[0] = y.astype(y_ref.dtype)


def _s5_conv(h, tables, d_skip, *, tm=1024):
    bsz, seq, c = h.shape
    tm = min(tm, seq)
    w_lag, b_cat, c_cat, step_pow, carry_pow = tables
    n_slab = c // SLAB
    tps = SLAB // LANES
    st = b_cat.shape[2]
    return pl.pallas_call(
        _s5_kernel,
        grid=(n_slab, bsz, seq // tm),
        in_specs=[
            pl.BlockSpec((1, tm, SLAB), lambda s, b, i: (b, i, s)),
            pl.BlockSpec((1,) + w_lag.shape[1:], lambda s, b, i: (s, 0, 0)),
            pl.BlockSpec((tps,) + b_cat.shape[1:], lambda s, b, i: (s, 0, 0)),
            pl.BlockSpec((tps,) + c_cat.shape[1:], lambda s, b, i: (s, 0, 0)),
            pl.BlockSpec((tps,) + step_pow.shape[1:], lambda s, b, i: (s, 0, 0)),
            pl.BlockSpec((tps,) + carry_pow.shape[1:], lambda s, b, i: (s, 0, 0)),
            pl.BlockSpec((1, SLAB), lambda s, b, i: (0, s)),
        ],
        out_specs=pl.BlockSpec((1, tm, SLAB), lambda s, b, i: (b, i, s)),
        out_shape=jax.ShapeDtypeStruct((bsz, seq, c), BF16),
        scratch_shapes=[pltpu.VMEM((tps, 1, st), F32), pltpu.VMEM((tps, tm, LANES), F32),
                        pltpu.VMEM((tps, tm, LANES), F32)],
        compiler_params=_cparams(("parallel", "parallel", "arbitrary")),
        name="s5_conv",
    )(h, w_lag, b_cat, c_cat, step_pow, carry_pow, d_skip.reshape(1, c))


def _rwkv_block(h, ln0, ln1, mu, w0, w1, w2, a0, a1, a2, g1, g2, k_k, k_a, r_k, wr, wk, wv, wo,
                lnx_g, lnx_b, mlp_w1, mlp_w2):
    bsz, seq, c = h.shape
    zeros = jnp.zeros((6, c), F32)
    vecs = jnp.concatenate([mu, w0[None], a0[None], k_k[None], k_a[None], zeros], axis=0)
    scan_vecs = jnp.concatenate([r_k.reshape(1, c), lnx_g[None], lnx_b[None], zeros[:5]], axis=0)
    idx = jnp.arange(SLAB) // HEAD
    jmat = (idx[:, None] == idx[None, :]).astype(BF16)
    r, k, v, lw, an, bn, gate = _rw_proj(h, vecs, wr, wk, wv, w1, w2, a1, a2, g1, g2, jmat)
    y = _rw_scan(r, k, v, lw, an, bn, scan_vecs)
    n = bsz * seq
    out = _mixer_out_mlp(_gate_mlp_kernel, "rwkv_out_mlp", [h.reshape(n, c), y.reshape(n, c), gate.reshape(n, c)],
                         wo, ln0, mlp_w1, mlp_w2, ln1)
    return out.reshape(bsz, seq, c)


def _s5_block(h, ln0, ln1, a_re, a_im, log_dt, b_re, b_im, c_re, c_im, d_skip, w_glu, mlp_w1, mlp_w2):
    bsz, seq, c = h.shape
    tables = _s5_tables(a_re, a_im, log_dt, b_re, b_im, c_re, c_im)
    y = _s5_conv(h, tables, d_skip)
    n = bsz * seq
    out = _mixer_out_mlp(_glu_mlp_kernel, "s5_out_mlp", [h.reshape(n, c), y.reshape(n, c)],
                         w_glu, ln0, mlp_w1, mlp_w2, ln1)
    return out.reshape(bsz, seq, c)


def kernel(x, ln_g, ln_b, rw_mu, rw_w0, rw_w1, rw_w2, rw_a0, rw_a1, rw_a2, rw_g1, rw_g2, rw_k_k, rw_k_a, rw_r_k, rw_wr, rw_wk, rw_wv, rw_wo, rw_lnx_g, rw_lnx_b, s5_a_re, s5_a_im, s5_log_dt, s5_b_re, s5_b_im, s5_c_re, s5_c_im, s5_d, s5_w_glu, mlp_w1, mlp_w2):
    h = _rwkv_block(x, (ln_g[0], ln_b[0]), (ln_g[1], ln_b[1]), rw_mu[0], rw_w0[0], rw_w1[0], rw_w2[0],
                    rw_a0[0], rw_a1[0], rw_a2[0], rw_g1[0], rw_g2[0], rw_k_k[0], rw_k_a[0], rw_r_k[0],
                    rw_wr[0], rw_wk[0], rw_wv[0], rw_wo[0], rw_lnx_g[0], rw_lnx_b[0], mlp_w1[0], mlp_w2[0])
    h = _s5_block(h, (ln_g[2], ln_b[2]), (ln_g[3], ln_b[3]), s5_a_re[0], s5_a_im[0], s5_log_dt[0],
                  s5_b_re[0], s5_b_im[0], s5_c_re[0], s5_c_im[0], s5_d[0], s5_w_glu[0], mlp_w1[1], mlp_w2[1])
    return h
```

```python
import functools
import math

import jax
import jax.numpy as jnp
from jax import lax
from jax.experimental import pallas as pl
from jax.experimental.pallas import tpu as pltpu

F32 = jnp.float32
BF16 = jnp.bfloat16

DEPTH = 2
DN_ALPHA = (2.0 * DEPTH) ** 0.25
LN_EPS = 1e-5
GN_EPS = 64e-5

LANES = 128
SUBLANES = 8
_SUBLANE_STEPS = (1, 2, 4)
HEAD = 64
PAIR = 2 * HEAD
SLAB = 256
RW_CHUNK = 64
S5_CHUNK = 8
VMEM_LIMIT = 56 * 1024 * 1024


def _cparams(sem):
    return pltpu.CompilerParams(dimension_semantics=sem, vmem_limit_bytes=VMEM_LIMIT)


def _layer_norm(z, g, b):
    mu = jnp.mean(z, axis=-1, keepdims=True)
    d = z - mu
    var = jnp.mean(d * d, axis=-1, keepdims=True)
    return d * lax.rsqrt(var + LN_EPS) * g + b


def _dot(a, b):
    return jnp.dot(a.astype(BF16), b.astype(BF16), preferred_element_type=F32)


def _dot_t(a, b):
    return lax.dot_general(a.astype(BF16), b.astype(BF16), (((1,), (1,)), ((), ())),
                           preferred_element_type=F32)


def _tdot(a, b):
    return lax.dot_general(a.astype(BF16), b.astype(BF16), (((0,), (0,)), ((), ())),
                           preferred_element_type=F32)


MLP_FF_CHUNK = 1024


def _mlp_ln_value(x, w1_ref, w2_ref, g_ref, b_ref):
    xb = x.astype(BF16)
    d_ff = w1_ref.shape[1]
    chunk = min(MLP_FF_CHUNK, d_ff)
    acc = jnp.zeros(x.shape, F32)
    for c in range(d_ff // chunk):
        h = jnp.dot(xb, w1_ref[:, c * chunk:(c + 1) * chunk], preferred_element_type=F32)
        h = jnp.square(jnp.maximum(h, 0.0))
        acc = acc + jnp.dot(h.astype(BF16), w2_ref[c * chunk:(c + 1) * chunk, :], preferred_element_type=F32)
    return _layer_norm(DN_ALPHA * x + acc, g_ref[...], b_ref[...])


def _gate_mlp_kernel(x_ref, y_ref, gate_ref, wo_ref, g0_ref, b0_ref, w1_ref, w2_ref, g1_ref, b1_ref, o_ref):
    x = x_ref[...]
    mix = _dot(y_ref[...].astype(F32) * gate_ref[...].astype(F32), wo_ref[...])
    h = _layer_norm(DN_ALPHA * x + mix, g0_ref[...], b0_ref[...])
    o_ref[...] = _mlp_ln_value(h, w1_ref, w2_ref, g1_ref, b1_ref)


def _glu_mlp_kernel(x_ref, y_ref, wglu_ref, g0_ref, b0_ref, w1_ref, w2_ref, g1_ref, b1_ref, o_ref):
    x = x_ref[...]
    c = x.shape[1]
    y = y_ref[...].astype(F32)
    y = 0.5 * y * (1.0 + jnp.tanh(math.sqrt(2.0 / math.pi) * (y + 0.044715 * (y * y * y))))
    z = _dot(y, wglu_ref[...])
    mix = z[:, :c] * jax.nn.sigmoid(z[:, c:])
    h = _layer_norm(DN_ALPHA * x + mix, g0_ref[...], b0_ref[...])
    o_ref[...] = _mlp_ln_value(h, w1_ref, w2_ref, g1_ref, b1_ref)


def _mixer_out_mlp(body, name, row_inputs, w_mix, ln0, mlp_w1, mlp_w2, layer, ln1, *, tm=512):
    n, c = row_inputs[0].shape
    tm = min(tm, n)
    tile = pl.BlockSpec((tm, c), lambda i: (i, 0))
    const = lambda i: (0, 0)
    vec = pl.BlockSpec((1, c), const)
    w_mix = w_mix.astype(BF16)

    def stacked(w):
        return pl.BlockSpec((None,) + w.shape[1:], lambda i: (layer, 0, 0), pipeline_mode=pl.Buffered(1))

    return pl.pallas_call(
        body,
        grid=(n // tm,),
        in_specs=[tile] * len(row_inputs) + [pl.BlockSpec(w_mix.shape, const, pipeline_mode=pl.Buffered(1)),
                                             vec, vec, stacked(mlp_w1), stacked(mlp_w2), vec, vec],
        out_specs=tile,
        out_shape=jax.ShapeDtypeStruct((n, c), F32),
        compiler_params=_cparams(("parallel",)),
        name=name,
    )(*row_inputs, w_mix, ln0[0].reshape(1, c), ln0[1].reshape(1, c), mlp_w1, mlp_w2,
      ln1[0].reshape(1, c), ln1[1].reshape(1, c))


_V_MU, _V_W0, _V_A0, _V_KK, _V_KA = 0, 6, 7, 8, 9


def _head_sum(x, j_ref):
    parts = [_dot(x[:, s:s + SLAB], j_ref[...]) for s in range(0, x.shape[1], SLAB)]
    return jnp.concatenate(parts, axis=1)


def _rw_proj_kernel(x_ref, xp_ref, vec_ref, wr_ref, wk_ref, wv_ref, w1_ref, w2_ref, a1_ref, a2_ref,
                    g1_ref, g2_ref, j_ref, r_ref, k_ref, v_ref, lw_ref, an_ref, bn_ref, g_ref):
    x = x_ref[0]
    tm = x.shape[0]
    row = lax.broadcasted_iota(jnp.int32, (tm, 1), 0)
    prev_last = jnp.where(pl.program_id(1) == 0, 0.0, xp_ref[0][7:8, :])
    xprev = jnp.where(row == 0, prev_last, pltpu.roll(x, 1, axis=0))
    xx = xprev - x

    def mix(i):
        return x + xx * vec_ref[_V_MU + i:_V_MU + i + 1, :]

    r = _dot(mix(0), wr_ref[...])
    w_pre = vec_ref[_V_W0:_V_W0 + 1, :] + _dot(jnp.tanh(_dot(mix(1), w1_ref[...])), w2_ref[...])
    k = _dot(mix(2), wk_ref[...])
    v = _dot(mix(3), wv_ref[...])
    a = jax.nn.sigmoid(vec_ref[_V_A0:_V_A0 + 1, :] + _dot(_dot(mix(4), a1_ref[...]), a2_ref[...]))
    g = _dot(jax.nn.sigmoid(_dot(mix(5), g1_ref[...])), g2_ref[...])

    kk = k * vec_ref[_V_KK:_V_KK + 1, :]
    kk = kk / jnp.maximum(jnp.sqrt(_head_sum(kk * kk, j_ref)), 1e-12)
    r_ref[0] = r.astype(r_ref.dtype)
    k_ref[0] = (k * (1.0 + (a - 1.0) * vec_ref[_V_KA:_V_KA + 1, :])).astype(k_ref.dtype)
    v_ref[0] = v.astype(v_ref.dtype)
    lw_ref[0] = (-math.exp(-0.5)) * jax.nn.sigmoid(w_pre)
    an_ref[0] = (-kk).astype(an_ref.dtype)
    bn_ref[0] = (kk * a).astype(bn_ref.dtype)
    g_ref[0] = g.astype(g_ref.dtype)


def _rw_proj(x, vecs, wr, wk, wv, w1, w2, a1, a2, g1, g2, jmat, *, tm=512):
    bsz, seq, c = x.shape
    tm = min(tm, seq)
    const = lambda b, i: (0, 0)
    tile = pl.BlockSpec((1, tm, c), lambda b, i: (b, i, 0))
    prev = pl.BlockSpec((1, 8, c), lambda b, i: (b, jnp.maximum(i * (tm // 8) - 1, 0), 0))

    def full(w):
        return pl.BlockSpec(w.shape, const)

    ws = [w.astype(BF16) for w in (wr, wk, wv, w1, w2, a1, a2, g1, g2)]
    out = lambda dt: jax.ShapeDtypeStruct((bsz, seq, c), dt)
    return pl.pallas_call(
        _rw_proj_kernel,
        grid=(bsz, seq // tm),
        in_specs=[tile, prev, full(vecs)] + [full(w) for w in ws] + [full(jmat)],
        out_specs=[tile] * 7,
        out_shape=[out(BF16), out(BF16), out(BF16), out(F32), out(BF16), out(BF16), out(BF16)],
        compiler_params=_cparams(("parallel", "arbitrary")),
        name="rwkv_proj",
    )(x, x, vecs, *ws, jmat)


def _rw_scan_kernel(r_ref, k_ref, v_ref, lw_ref, an_ref, bn_ref, vec_ref, y_ref, state_ref):
    nb, tm, c = r_ref.shape
    n_pairs = c // PAIR
    L = RW_CHUNK

    @pl.when(pl.program_id(1) == 0)
    def _():
        state_ref[...] = jnp.zeros_like(state_ref)

    sub3 = lax.broadcasted_iota(jnp.int32, (1, SUBLANES, 1), 1)

    def chunk_cumsum(x):
        x3 = x.reshape(L // SUBLANES, SUBLANES, x.shape[1])
        for d in _SUBLANE_STEPS:
            x3 = x3 + jnp.where(sub3 >= d, pltpu.roll(x3, d, axis=1), 0.0)
        blocks, run = [], None
        for kb in range(L // SUBLANES):
            blk = x3[kb] if run is None else x3[kb] + run
            blocks.append(blk)
            run = blk[SUBLANES - 1:SUBLANES, :]
        return jnp.concatenate(blocks, axis=0)

    ri = lax.broadcasted_iota(jnp.int32, (PAIR, PAIR), 0)
    ci = lax.broadcasted_iota(jnp.int32, (PAIR, PAIR), 1)
    same_head = (ri // L) == (ci // L)
    mask_strict = same_head & ((ci % L) < (ri % L))
    mask_incl = same_head & ((ci % L) <= (ri % L))
    eye = (ri == ci).astype(F32)
    lane = lax.broadcasted_iota(jnp.int32, (L, PAIR), 1)
    head0 = lane < HEAD
    ri_h = lax.broadcasted_iota(jnp.int32, (PAIR // 2, PAIR), 0)
    ci_h = lax.broadcasted_iota(jnp.int32, (PAIR // 2, PAIR), 1)

    def level_mask(dd):
        return same_head & ((ri // (2 * dd)) == (ci // (2 * dd))) & ((ri % (2 * dd)) >= dd) & ((ci % (2 * dd)) < dd)

    def sel(res):
        return jnp.where(head0, res[:L], res[L:])

    def dup(x):
        return jnp.concatenate([x, x], axis=0)

    def head_sum(x):
        s0 = jnp.sum(jnp.where(head0, x, 0.0), axis=1, keepdims=True)
        s1 = jnp.sum(jnp.where(head0, 0.0, x), axis=1, keepdims=True)
        return jnp.where(head0, s0, s1)

    def take_rows(x, dd, half):
        return jnp.concatenate([x[(2 * b + half) * dd:(2 * b + half + 1) * dd]
                                for b in range(PAIR // (2 * dd))], axis=0)

    def merge_rows(first, second, dd):
        parts = []
        for b in range(PAIR // (2 * dd)):
            parts += [first[b * dd:(b + 1) * dd], second[b * dd:(b + 1) * dd]]
        return jnp.concatenate(parts, axis=0)

    n_chunks = tm // L
    per_chunk = nb * n_pairs
    probs = range(n_chunks * per_chunk)
    rows = [slice((p // per_chunk) * L, (p // per_chunk + 1) * L) for p in probs]
    seq = [(p % per_chunk) // n_pairs for p in probs]
    lanes = [slice((p % n_pairs) * PAIR, (p % n_pairs + 1) * PAIR) for p in probs]

    def tile(ref, p):
        return ref[seq[p], rows[p], lanes[p]]

    rhs, ar, bk, vv, e_l = [], [], [], [], []
    for p in probs:
        r = tile(r_ref, p).astype(F32)
        k = tile(k_ref, p).astype(F32)
        a = tile(an_ref, p).astype(F32)
        b = tile(bn_ref, p).astype(F32)
        lw = tile(lw_ref, p)
        cs_c = chunk_cumsum(lw)
        cs_last = cs_c[L - 1:L, :]
        a_t = a * jnp.exp(cs_c - lw)
        r_t = r * jnp.exp(cs_c)
        e_m = jnp.exp(-cs_c)
        b_t = b * e_m
        k_t = k * e_m
        e_d = jnp.exp(cs_last - cs_c)
        rhs.append(jnp.concatenate([jnp.where(head0, b_t, 0.0), jnp.where(head0, 0.0, b_t),
                                    jnp.where(head0, k_t, 0.0), jnp.where(head0, 0.0, k_t)], axis=0).astype(BF16))
        ar.append(jnp.concatenate([a_t, r_t], axis=0).astype(BF16))
        bk.append(jnp.concatenate([b * e_d, k * e_d], axis=0).astype(BF16))
        vv.append(dup(tile(v_ref, p)))
        e_l.append(jnp.exp(cs_last))

    sc = [_dot_t(ar[p], rhs[p]) for p in probs]
    n_ab = [jnp.where(mask_strict, dup(sc[p][:L, :PAIR]), 0.0) for p in probs]
    n_ak = [jnp.where(mask_strict, dup(sc[p][:L, PAIR:]), 0.0).astype(BF16) for p in probs]
    n_r = [jnp.concatenate([jnp.where(mask_incl, dup(sc[p][L:, :PAIR]), 0.0),
                            jnp.where(mask_incl, dup(sc[p][L:, PAIR:]), 0.0)], axis=1).astype(BF16) for p in probs]
    w0 = [sel(_dot(n_ak[p], vv[p])) for p in probs]

    t_inv = [jnp.where(level_mask(1), n_ab[p], eye) for p in probs]
    n_ab_b = [n_ab[p].astype(BF16) for p in probs]
    dd = 2
    while dd < SUBLANES:
        lm = level_mask(dd)
        nt = [jnp.where(lm, _dot(n_ab_b[p], t_inv[p]), 0.0) for p in probs]
        t_inv = [t_inv[p] + _dot(t_inv[p], nt[p]) for p in probs]
        dd *= 2
    zero_half = jnp.zeros((PAIR // 2, PAIR), F32)
    while dd < L:
        lm_hi = ((ri_h // dd) == (ci_h // (2 * dd))) & ((ci_h % (2 * dd)) < dd)
        nt = [merge_rows(zero_half, jnp.where(lm_hi, _dot(take_rows(n_ab[p], dd, 1), t_inv[p]), 0.0), dd)
              for p in probs]
        t_hi = [take_rows(t_inv[p], dd, 1) for p in probs]
        t_inv = [merge_rows(take_rows(t_inv[p], dd, 0), t_hi[p] + _dot(t_hi[p], nt[p]), dd) for p in probs]
        dd *= 2
    t_inv = [t_inv[p].astype(BF16) for p in probs]

    for ch in range(n_chunks):
        cur = range(ch * per_chunk, (ch + 1) * per_chunk)
        slot = {p: p - ch * per_chunk for p in cur}
        s_prev = {p: state_ref[slot[p]] for p in cur}
        ars = {p: _dot_t(ar[p], s_prev[p]) for p in cur}
        u = {p: sel(_dot(t_inv[p], dup(ars[p][:L] + w0[p]))) for p in cur}
        o = {p: ars[p][L:] + sel(_dot(n_r[p], jnp.concatenate([dup(u[p]).astype(BF16), vv[p]], axis=0))) for p in cur}
        for p in cur:
            upd = _tdot(jnp.concatenate([u[p].astype(BF16), vv[p][:L]], axis=0), bk[p])
            state_ref[slot[p]] = s_prev[p] * e_l[p] + jnp.where(same_head, upd, 0.0)
        for p in cur:
            dlt = o[p] - head_sum(o[p]) * (1.0 / HEAD)
            var = head_sum(dlt * dlt) * (1.0 / HEAD)
            rk = tile(r_ref, p).astype(F32) * tile(k_ref, p).astype(F32)
            bonus = head_sum(rk * vec_ref[0:1, lanes[p]]) * tile(v_ref, p).astype(F32)
            y = dlt * lax.rsqrt(var + GN_EPS) * vec_ref[1:2, lanes[p]] + vec_ref[2:3, lanes[p]] + bonus
            y_ref[seq[p], rows[p], lanes[p]] = y.astype(y_ref.dtype)


def _rw_scan(r, k, v, lw, an, bn, vecs, *, tm=128, nb=4):
    bsz, seq, c = r.shape
    tm = min(tm, seq)
    nb = math.gcd(nb, bsz)
    tile = pl.BlockSpec((nb, tm, c), lambda b, i: (b, i, 0))
    const = lambda b, i: (0, 0)
    return pl.pallas_call(
        _rw_scan_kernel,
        grid=(bsz // nb, seq // tm),
        in_specs=[tile] * 6 + [pl.BlockSpec(vecs.shape, const)],
        out_specs=tile,
        out_shape=jax.ShapeDtypeStruct((bsz, seq, c), BF16),
        scratch_shapes=[pltpu.VMEM((nb * (c // PAIR), PAIR, PAIR), F32)],
        compiler_params=_cparams(("parallel", "arbitrary")),
        name="rwkv_scan",
    )(r, k, v, lw, an, bn, vecs)


def _s5_tables(a_re, a_im, log_dt, b_re, b_im, c_re, c_im):
    g, p = a_re.shape
    s = b_re.shape[-1]
    L = S5_CHUNK
    dt = jnp.exp(log_dt.astype(F32))[:, None]
    lam_re = jnp.minimum(a_re.astype(F32), -1e-4)
    lam_im = a_im.astype(F32)

    def apow(n):
        n = jnp.asarray(n, F32)[:, None, None]
        mag = jnp.exp(n * dt * lam_re)
        return mag * jnp.cos(n * dt * lam_im), mag * jnp.sin(n * dt * lam_im)

    ab_re, ab_im = apow([1])
    ab_re, ab_im = ab_re[0], ab_im[0]
    den = lam_re * lam_re + lam_im * lam_im
    nr, ni = ab_re - 1.0, ab_im
    coef_re = (nr * lam_re + ni * lam_im) / den
    coef_im = (ni * lam_re - nr * lam_im) / den
    bb_re = coef_re[..., None] * b_re - coef_im[..., None] * b_im
    bb_im = coef_re[..., None] * b_im + coef_im[..., None] * b_re
    cr, ci = c_re.astype(F32), c_im.astype(F32)

    pw_re, pw_im = apow(jnp.arange(L))
    cb_re = (jnp.einsum('gcp,tgp,gpd->gtcd', cr, pw_re, bb_re) - jnp.einsum('gcp,tgp,gpd->gtcd', cr, pw_im, bb_im)
             - jnp.einsum('gcp,tgp,gpd->gtcd', ci, pw_re, bb_im) - jnp.einsum('gcp,tgp,gpd->gtcd', ci, pw_im, bb_re))

    def group_mask(n_groups, n_cols, col_group_size):
        col_group = (jnp.arange(n_cols) // col_group_size) % n_groups
        return (jnp.arange(n_groups)[:, None] == col_group[None, :]).astype(F32)[None, None, :, None, :]

    gs = SLAB // s
    n_slab = g // gs
    m_lag = cb_re.reshape(n_slab, gs, L, s, s).transpose(0, 2, 4, 1, 3).reshape(n_slab, L, s, SLAB)
    w_lag = (m_lag[:, :, None, :, :] * group_mask(gs, SLAB, s)).astype(BF16).reshape(n_slab, L * SLAB, SLAB)

    gt = LANES // s
    n_t = g // gt
    st = 2 * gt * p
    rv_re, rv_im = pw_re[::-1], pw_im[::-1]
    in_re = rv_re[..., None] * bb_re[None] - rv_im[..., None] * bb_im[None]
    in_im = rv_re[..., None] * bb_im[None] + rv_im[..., None] * bb_re[None]
    m_in = jnp.stack([in_re, in_im], axis=0).reshape(2, L, n_t, gt, p, s)
    m_in = m_in.transpose(2, 1, 5, 0, 3, 4).reshape(n_t, L, s, st)
    b_cat = (m_in[:, :, None, :, :] * group_mask(gt, st, p)).astype(BF16).reshape(n_t, L * LANES, st)
    p1_re, p1_im = apow(jnp.arange(1, L + 1))
    out_re = cr[None] * p1_re[:, :, None, :] - ci[None] * p1_im[:, :, None, :]
    out_im = cr[None] * p1_im[:, :, None, :] + ci[None] * p1_re[:, :, None, :]
    m_out = jnp.stack([out_re, -out_im], axis=0).reshape(2, L, n_t, gt, s, p)
    m_out = m_out.transpose(2, 0, 5, 1, 3, 4).reshape(n_t, 2, p, L * LANES)
    c_cat = (m_out[:, :, None, :, :] * group_mask(gt, L * LANES, s)).astype(BF16).reshape(n_t, st, L * LANES)

    def lay_vec(re, im):
        n = re.shape[0]
        f = lambda m: m.reshape(n, n_t, gt * p).transpose(1, 0, 2)
        return jnp.concatenate([f(re), f(im)], axis=-1)

    step_pow = lay_vec(*apow([L * d for d in _SUBLANE_STEPS]))
    carry_pow = lay_vec(*apow(L * jnp.arange(1, SUBLANES + 1)))
    return w_lag, b_cat, c_cat, step_pow, carry_pow


def _cmul(ar, ai, br, bi):
    return ar * br - ai * bi, ar * bi + ai * br


def _s5_kernel(u_ref, wlag_ref, bcat_ref, ccat_ref, spow_ref, cpow_ref, dskip_ref, y_ref,
               carry_ref, ut_ref, zt_ref):
    tm = u_ref.shape[1]
    L = S5_CHUNK
    assert L == SUBLANES
    n_rows = tm // L
    half = bcat_ref.shape[2] // 2

    @pl.when(pl.program_id(2) == 0)
    def _():
        carry_ref[...] = jnp.zeros_like(carry_ref)

    u = u_ref[0]
    sub3 = lax.broadcasted_iota(jnp.int32, (1, SUBLANES, 1), 1)
    sub2 = lax.broadcasted_iota(jnp.int32, (SUBLANES, 1), 0)
    n_blk = n_rows // SUBLANES
    tiles = range(SLAB // LANES)

    x = []
    for t in tiles:
        ut_ref[t] = u[:, t * LANES:(t + 1) * LANES]
        ucat = jnp.concatenate([ut_ref[t, pl.ds(s, n_rows, stride=L), :] for s in range(L)], axis=1)
        x.append(_dot(ucat, bcat_ref[t]))

    u3 = u.reshape(tm // SUBLANES, SUBLANES, SLAB)
    shifted = [u.astype(BF16)]
    for tau in range(1, L):
        sh = jnp.where(sub3 >= tau, pltpu.roll(u3, tau, axis=1), 0.0)
        shifted.append(sh.reshape(tm, SLAB).astype(BF16))
    y_lag = jnp.dot(jnp.concatenate(shifted, axis=1), wlag_ref[0], preferred_element_type=F32)

    for t in tiles:
        hr = x[t][:, :half].reshape(n_blk, SUBLANES, half)
        hi = x[t][:, half:].reshape(n_blk, SUBLANES, half)
        for i, d in enumerate(_SUBLANE_STEPS):
            pr, pi = spow_ref[t, i:i + 1, :half], spow_ref[t, i:i + 1, half:]
            sr, si = _cmul(pr, pi, pltpu.roll(hr, d, axis=1), pltpu.roll(hi, d, axis=1))
            hr = hr + jnp.where(sub3 >= d, sr, 0.0)
            hi = hi + jnp.where(sub3 >= d, si, 0.0)
        cr, ci = carry_ref[t, :, :half], carry_ref[t, :, half:]
        cp_r, cp_i = cpow_ref[t, :, :half], cpow_ref[t, :, half:]
        prev_r, prev_i = [], []
        for kb in range(n_blk):
            kr, ki = _cmul(cp_r, cp_i, cr, ci)
            br, bi = hr[kb] + kr, hi[kb] + ki
            prev_r.append(jnp.where(sub2 == 0, cr, pltpu.roll(br, 1, axis=0)))
            prev_i.append(jnp.where(sub2 == 0, ci, pltpu.roll(bi, 1, axis=0)))
            cr, ci = br[SUBLANES - 1:SUBLANES, :], bi[SUBLANES - 1:SUBLANES, :]
        carry_ref[t, :, :half] = cr
        carry_ref[t, :, half:] = ci
        prev = jnp.concatenate([jnp.concatenate(prev_r, axis=0), jnp.concatenate(prev_i, axis=0)], axis=1)
        z = _dot(prev, ccat_ref[t])
        for s in range(L):
            zt_ref[t, pl.ds(s, n_rows, stride=L), :] = z[:, s * LANES:(s + 1) * LANES]

    y_state = jnp.concatenate([zt_ref[t] for t in tiles], axis=1)
    y_ref[0] = (y_lag + y_state + u * dskip_ref[...]).astype(y_ref.dtype)


def _s5_conv(h, tables, d_skip, *, tm=1024):
    bsz, seq, c = h.shape
    tm = min(tm, seq)
    w_lag, b_cat, c_cat, step_pow, carry_pow = tables
    n_slab = c // SLAB
    tps = SLAB // LANES
    st = b_cat.shape[2]
    return pl.pallas_call(
        _s5_kernel,
        grid=(n_slab, bsz, seq // tm),
        in_specs=[
            pl.BlockSpec((1, tm, SLAB), lambda s, b, i: (b, i, s)),
            pl.BlockSpec((1,) + w_lag.shape[1:], lambda s, b, i: (s, 0, 0)),
            pl.BlockSpec((tps,) + b_cat.shape[1:], lambda s, b, i: (s, 0, 0)),
            pl.BlockSpec((tps,) + c_cat.shape[1:], lambda s, b, i: (s, 0, 0)),
            pl.BlockSpec((tps,) + step_pow.shape[1:], lambda s, b, i: (s, 0, 0)),
            pl.BlockSpec((tps,) + carry_pow.shape[1:], lambda s, b, i: (s, 0, 0)),
            pl.BlockSpec((1, SLAB), lambda s, b, i: (0, s)),
        ],
        out_specs=pl.BlockSpec((1, tm, SLAB), lambda s, b, i: (b, i, s)),
        out_shape=jax.ShapeDtypeStruct((bsz, seq, c), BF16),
        scratch_shapes=[pltpu.VMEM((tps, 1, st), F32), pltpu.VMEM((tps, tm, LANES), F32),
                        pltpu.VMEM((tps, tm, LANES), F32)],
        compiler_params=_cparams(("parallel", "parallel", "arbitrary")),
        name="s5_conv",
    )(h, w_lag, b_cat, c_cat, step_pow, carry_pow, d_skip.reshape(1, c))


def _rwkv_block(h, ln0, ln1, mu, w0, w1, w2, a0, a1, a2, g1, g2, k_k, k_a, r_k, wr, wk, wv, wo,
                lnx_g, lnx_b, mlp_w1, mlp_w2, layer):
    bsz, seq, c = h.shape
    zeros = jnp.zeros((6, c), F32)
    vecs = jnp.concatenate([mu, w0[None], a0[None], k_k[None], k_a[None], zeros], axis=0)
    scan_vecs = jnp.concatenate([r_k.reshape(1, c), lnx_g[None], lnx_b[None], zeros[:5]], axis=0)
    idx = jnp.arange(SLAB) // HEAD
    jmat = (idx[:, None] == idx[None, :]).astype(BF16)
    r, k, v, lw, an, bn, gate = _rw_proj(h, vecs, wr, wk, wv, w1, w2, a1, a2, g1, g2, jmat)
    y = _rw_scan(r, k, v, lw, an, bn, scan_vecs)
    n = bsz * seq
    out = _mixer_out_mlp(_gate_mlp_kernel, "rwkv_out_mlp", [h.reshape(n, c), y.reshape(n, c), gate.reshape(n, c)],
                         wo, ln0, mlp_w1, mlp_w2, layer, ln1)
    return out.reshape(bsz, seq, c)


def _s5_block(h, ln0, ln1, a_re, a_im, log_dt, b_re, b_im, c_re, c_im, d_skip, w_glu, mlp_w1, mlp_w2, layer):
    bsz, seq, c = h.shape
    tables = _s5_tables(a_re, a_im, log_dt, b_re, b_im, c_re, c_im)
    y = _s5_conv(h, tables, d_skip)
    n = bsz * seq
    out = _mixer_out_mlp(_glu_mlp_kernel, "s5_out_mlp", [h.reshape(n, c), y.reshape(n, c)],
                         w_glu, ln0, mlp_w1, mlp_w2, layer, ln1)
    return out.reshape(bsz, seq, c)


def kernel(x, ln_g, ln_b, rw_mu, rw_w0, rw_w1, rw_w2, rw_a0, rw_a1, rw_a2, rw_g1, rw_g2, rw_k_k, rw_k_a, rw_r_k, rw_wr, rw_wk, rw_wv, rw_wo, rw_lnx_g, rw_lnx_b, s5_a_re, s5_a_im, s5_log_dt, s5_b_re, s5_b_im, s5_c_re, s5_c_im, s5_d, s5_w_glu, mlp_w1, mlp_w2):
    w1b, w2b = mlp_w1.astype(BF16), mlp_w2.astype(BF16)
    h = _rwkv_block(x, (ln_g[0], ln_b[0]), (ln_g[1], ln_b[1]), rw_mu[0], rw_w0[0], rw_w1[0], rw_w2[0],
                    rw_a0[0], rw_a1[0], rw_a2[0], rw_g1[0], rw_g2[0], rw_k_k[0], rw_k_a[0], rw_r_k[0],
                    rw_wr[0], rw_wk[0], rw_wv[0], rw_wo[0], rw_lnx_g[0], rw_lnx_b[0], w1b, w2b, 0)
    h = _s5_block(h, (ln_g[2], ln_b[2]), (ln_g[3], ln_b[3]), s5_a_re[0], s5_a_im[0], s5_log_dt[0],
                  s5_b_re[0], s5_b_im[0], s5_c_re[0], s5_c_im[0], s5_d[0], s5_w_glu[0], w1b, w2b, 1)
    return h
```

```python
import functools
import math

import jax
import jax.numpy as jnp
from jax import lax
from jax.experimental import pallas as pl
from jax.experimental.pallas import tpu as pltpu

F32 = jnp.float32
BF16 = jnp.bfloat16

DEPTH = 2
DN_ALPHA = (2.0 * DEPTH) ** 0.25
LN_EPS = 1e-5
GN_EPS = 64e-5

LANES = 128
SUBLANES = 8
_SUBLANE_STEPS = (1, 2, 4)
HEAD = 64
PAIR = 2 * HEAD
SLAB = 256
RW_CHUNK = 64
S5_CHUNK = 8
VMEM_LIMIT = 56 * 1024 * 1024


def _cparams(sem):
    return pltpu.CompilerParams(dimension_semantics=sem, vmem_limit_bytes=VMEM_LIMIT)


def _layer_norm(z, g, b):
    mu = jnp.mean(z, axis=-1, keepdims=True)
    d = z - mu
    var = jnp.mean(d * d, axis=-1, keepdims=True)
    return d * lax.rsqrt(var + LN_EPS) * g + b


def _dot(a, b):
    return jnp.dot(a.astype(BF16), b.astype(BF16), preferred_element_type=F32)


def _dot_t(a, b):
    return lax.dot_general(a.astype(BF16), b.astype(BF16), (((1,), (1,)), ((), ())),
                           preferred_element_type=F32)


def _tdot(a, b):
    return lax.dot_general(a.astype(BF16), b.astype(BF16), (((0,), (0,)), ((), ())),
                           preferred_element_type=F32)


MLP_FF_CHUNK = 1024


ROW_SPLIT = 2


def _mixer_out_mlp_body(mix_fn, x_ref, g0_ref, b0_ref, w1_ref, w2_ref, g1_ref, b1_ref, o_ref):
    tm = x_ref.shape[0]
    d_ff = w1_ref.shape[1]
    chunk = min(MLP_FF_CHUNK, d_ff)
    groups = [slice(i * tm // ROW_SPLIT, (i + 1) * tm // ROW_SPLIT) for i in range(ROW_SPLIT)]
    mix = [mix_fn(rows) for rows in groups]
    h = [_layer_norm(DN_ALPHA * x_ref[rows, :] + m, g0_ref[...], b0_ref[...]) for rows, m in zip(groups, mix)]
    hb = [v.astype(BF16) for v in h]
    acc = [jnp.zeros(v.shape, F32) for v in h]
    for c in range(d_ff // chunk):
        up = [jnp.dot(v, w1_ref[:, c * chunk:(c + 1) * chunk], preferred_element_type=F32) for v in hb]
        act = [jnp.square(jnp.maximum(u, 0.0)).astype(BF16) for u in up]
        acc = [s + jnp.dot(a, w2_ref[c * chunk:(c + 1) * chunk, :], preferred_element_type=F32)
               for s, a in zip(acc, act)]
    for rows, v, s in zip(groups, h, acc):
        o_ref[rows, :] = _layer_norm(DN_ALPHA * v + s, g1_ref[...], b1_ref[...])


def _gate_mlp_kernel(x_ref, y_ref, gate_ref, wo_ref, g0_ref, b0_ref, w1_ref, w2_ref, g1_ref, b1_ref, o_ref):
    def mix(rows):
        return _dot(y_ref[rows, :].astype(F32) * gate_ref[rows, :].astype(F32), wo_ref[...])

    _mixer_out_mlp_body(mix, x_ref, g0_ref, b0_ref, w1_ref, w2_ref, g1_ref, b1_ref, o_ref)


def _glu_mlp_kernel(x_ref, y_ref, wglu_ref, g0_ref, b0_ref, w1_ref, w2_ref, g1_ref, b1_ref, o_ref):
    c = x_ref.shape[1]

    def mix(rows):
        y = y_ref[rows, :].astype(F32)
        y = 0.5 * y * (1.0 + jnp.tanh(math.sqrt(2.0 / math.pi) * (y + 0.044715 * (y * y * y))))
        z = _dot(y, wglu_ref[...])
        return z[:, :c] * jax.nn.sigmoid(z[:, c:])

    _mixer_out_mlp_body(mix, x_ref, g0_ref, b0_ref, w1_ref, w2_ref, g1_ref, b1_ref, o_ref)


def _mixer_out_mlp(body, name, row_inputs, w_mix, ln0, mlp_w1, mlp_w2, layer, ln1, *, tm=512):
    n, c = row_inputs[0].shape
    tm = min(tm, n)
    tile = pl.BlockSpec((tm, c), lambda i: (i, 0))
    const = lambda i: (0, 0)
    vec = pl.BlockSpec((1, c), const)
    w_mix = w_mix.astype(BF16)

    def stacked(w):
        return pl.BlockSpec((None,) + w.shape[1:], lambda i: (layer, 0, 0), pipeline_mode=pl.Buffered(1))

    return pl.pallas_call(
        body,
        grid=(n // tm,),
        in_specs=[tile] * len(row_inputs) + [pl.BlockSpec(w_mix.shape, const, pipeline_mode=pl.Buffered(1)),
                                             vec, vec, stacked(mlp_w1), stacked(mlp_w2), vec, vec],
        out_specs=tile,
        out_shape=jax.ShapeDtypeStruct((n, c), F32),
        compiler_params=_cparams(("parallel",)),
        name=name,
    )(*row_inputs, w_mix, ln0[0].reshape(1, c), ln0[1].reshape(1, c), mlp_w1, mlp_w2,
      ln1[0].reshape(1, c), ln1[1].reshape(1, c))


_V_MU, _V_W0, _V_A0, _V_KK, _V_KA = 0, 6, 7, 8, 9


def _head_sum(x, j_ref):
    parts = [_dot(x[:, s:s + SLAB], j_ref[...]) for s in range(0, x.shape[1], SLAB)]
    return jnp.concatenate(parts, axis=1)


def _rw_proj_kernel(x_ref, xp_ref, vec_ref, wr_ref, wk_ref, wv_ref, w1_ref, w2_ref, a1_ref, a2_ref,
                    g1_ref, g2_ref, j_ref, r_ref, k_ref, v_ref, lw_ref, an_ref, bn_ref, g_ref):
    x = x_ref[0]
    tm = x.shape[0]
    row = lax.broadcasted_iota(jnp.int32, (tm, 1), 0)
    prev_last = jnp.where(pl.program_id(1) == 0, 0.0, xp_ref[0][7:8, :])
    xprev = jnp.where(row == 0, prev_last, pltpu.roll(x, 1, axis=0))
    xx = xprev - x

    def mix(i):
        return x + xx * vec_ref[_V_MU + i:_V_MU + i + 1, :]

    r = _dot(mix(0), wr_ref[...])
    w_pre = vec_ref[_V_W0:_V_W0 + 1, :] + _dot(jnp.tanh(_dot(mix(1), w1_ref[...])), w2_ref[...])
    k = _dot(mix(2), wk_ref[...])
    v = _dot(mix(3), wv_ref[...])
    a = jax.nn.sigmoid(vec_ref[_V_A0:_V_A0 + 1, :] + _dot(_dot(mix(4), a1_ref[...]), a2_ref[...]))
    g = _dot(jax.nn.sigmoid(_dot(mix(5), g1_ref[...])), g2_ref[...])

    kk = k * vec_ref[_V_KK:_V_KK + 1, :]
    kk = kk / jnp.maximum(jnp.sqrt(_head_sum(kk * kk, j_ref)), 1e-12)
    r_ref[0] = r.astype(r_ref.dtype)
    k_ref[0] = (k * (1.0 + (a - 1.0) * vec_ref[_V_KA:_V_KA + 1, :])).astype(k_ref.dtype)
    v_ref[0] = v.astype(v_ref.dtype)
    lw_ref[0] = (-math.exp(-0.5)) * jax.nn.sigmoid(w_pre)
    an_ref[0] = (-kk).astype(an_ref.dtype)
    bn_ref[0] = (kk * a).astype(bn_ref.dtype)
    g_ref[0] = g.astype(g_ref.dtype)


def _rw_proj(x, vecs, wr, wk, wv, w1, w2, a1, a2, g1, g2, jmat, *, tm=512):
    bsz, seq, c = x.shape
    tm = min(tm, seq)
    const = lambda b, i: (0, 0)
    tile = pl.BlockSpec((1, tm, c), lambda b, i: (b, i, 0))
    prev = pl.BlockSpec((1, 8, c), lambda b, i: (b, jnp.maximum(i * (tm // 8) - 1, 0), 0))

    def full(w):
        return pl.BlockSpec(w.shape, const)

    ws = [w.astype(BF16) for w in (wr, wk, wv, w1, w2, a1, a2, g1, g2)]
    out = lambda dt: jax.ShapeDtypeStruct((bsz, seq, c), dt)
    return pl.pallas_call(
        _rw_proj_kernel,
        grid=(bsz, seq // tm),
        in_specs=[tile, prev, full(vecs)] + [full(w) for w in ws] + [full(jmat)],
        out_specs=[tile] * 7,
        out_shape=[out(BF16), out(BF16), out(BF16), out(F32), out(BF16), out(BF16), out(BF16)],
        compiler_params=_cparams(("parallel", "arbitrary")),
        name="rwkv_proj",
    )(x, x, vecs, *ws, jmat)


def _rw_scan_kernel(r_ref, k_ref, v_ref, lw_ref, an_ref, bn_ref, vec_ref, y_ref, state_ref):
    nb, tm, c = r_ref.shape
    n_pairs = c // PAIR
    L = RW_CHUNK

    @pl.when(pl.program_id(1) == 0)
    def _():
        state_ref[...] = jnp.zeros_like(state_ref)

    sub3 = lax.broadcasted_iota(jnp.int32, (1, SUBLANES, 1), 1)

    def chunk_cumsum(x):
        x3 = x.reshape(L // SUBLANES, SUBLANES, x.shape[1])
        for d in _SUBLANE_STEPS:
            x3 = x3 + jnp.where(sub3 >= d, pltpu.roll(x3, d, axis=1), 0.0)
        blocks, run = [], None
        for kb in range(L // SUBLANES):
            blk = x3[kb] if run is None else x3[kb] + run
            blocks.append(blk)
            run = blk[SUBLANES - 1:SUBLANES, :]
        return jnp.concatenate(blocks, axis=0)

    ri = lax.broadcasted_iota(jnp.int32, (PAIR, PAIR), 0)
    ci = lax.broadcasted_iota(jnp.int32, (PAIR, PAIR), 1)
    same_head = (ri // L) == (ci // L)
    mask_strict = same_head & ((ci % L) < (ri % L))
    mask_incl = same_head & ((ci % L) <= (ri % L))
    eye = (ri == ci).astype(F32)
    lane = lax.broadcasted_iota(jnp.int32, (L, PAIR), 1)
    head0 = lane < HEAD
    ri_h = lax.broadcasted_iota(jnp.int32, (PAIR // 2, PAIR), 0)
    ci_h = lax.broadcasted_iota(jnp.int32, (PAIR // 2, PAIR), 1)

    def level_mask(dd):
        return same_head & ((ri // (2 * dd)) == (ci // (2 * dd))) & ((ri % (2 * dd)) >= dd) & ((ci % (2 * dd)) < dd)

    def sel(res):
        return jnp.where(head0, res[:L], res[L:])

    def dup(x):
        return jnp.concatenate([x, x], axis=0)

    def head_sum(x):
        s0 = jnp.sum(jnp.where(head0, x, 0.0), axis=1, keepdims=True)
        s1 = jnp.sum(jnp.where(head0, 0.0, x), axis=1, keepdims=True)
        return jnp.where(head0, s0, s1)

    def take_rows(x, dd, half):
        return jnp.concatenate([x[(2 * b + half) * dd:(2 * b + half + 1) * dd]
                                for b in range(PAIR // (2 * dd))], axis=0)

    def merge_rows(first, second, dd):
        parts = []
        for b in range(PAIR // (2 * dd)):
            parts += [first[b * dd:(b + 1) * dd], second[b * dd:(b + 1) * dd]]
        return jnp.concatenate(parts, axis=0)

    n_chunks = tm // L
    per_chunk = nb * n_pairs
    probs = range(n_chunks * per_chunk)
    rows = [slice((p // per_chunk) * L, (p // per_chunk + 1) * L) for p in probs]
    seq = [(p % per_chunk) // n_pairs for p in probs]
    lanes = [slice((p % n_pairs) * PAIR, (p % n_pairs + 1) * PAIR) for p in probs]

    def tile(ref, p):
        return ref[seq[p], rows[p], lanes[p]]

    rhs, ar, bk, vv, e_l = [], [], [], [], []
    for p in probs:
        r = tile(r_ref, p).astype(F32)
        k = tile(k_ref, p).astype(F32)
        a = tile(an_ref, p).astype(F32)
        b = tile(bn_ref, p).astype(F32)
        lw = tile(lw_ref, p)
        cs_c = chunk_cumsum(lw)
        cs_last = cs_c[L - 1:L, :]
        a_t = a * jnp.exp(cs_c - lw)
        r_t = r * jnp.exp(cs_c)
        e_m = jnp.exp(-cs_c)
        b_t = b * e_m
        k_t = k * e_m
        e_d = jnp.exp(cs_last - cs_c)
        rhs.append(jnp.concatenate([jnp.where(head0, b_t, 0.0), jnp.where(head0, 0.0, b_t),
                                    jnp.where(head0, k_t, 0.0), jnp.where(head0, 0.0, k_t)], axis=0).astype(BF16))
        ar.append(jnp.concatenate([a_t, r_t], axis=0).astype(BF16))
        bk.append(jnp.concatenate([b * e_d, k * e_d], axis=0).astype(BF16))
        vv.append(dup(tile(v_ref, p)))
        e_l.append(jnp.exp(cs_last))

    sc = [_dot_t(ar[p], rhs[p]) for p in probs]
    n_ab = [jnp.where(mask_strict, dup(sc[p][:L, :PAIR]), 0.0) for p in probs]
    n_ak = [jnp.where(mask_strict, dup(sc[p][:L, PAIR:]), 0.0).astype(BF16) for p in probs]
    n_r = [jnp.concatenate([jnp.where(mask_incl, dup(sc[p][L:, :PAIR]), 0.0),
                            jnp.where(mask_incl, dup(sc[p][L:, PAIR:]), 0.0)], axis=1).astype(BF16) for p in probs]
    w0 = [sel(_dot(n_ak[p], vv[p])) for p in probs]

    t_inv = [jnp.where(level_mask(1), n_ab[p], eye) for p in probs]
    n_ab_b = [n_ab[p].astype(BF16) for p in probs]
    dd = 2
    while dd < SUBLANES:
        lm = level_mask(dd)
        nt = [jnp.where(lm, _dot(n_ab_b[p], t_inv[p]), 0.0) for p in probs]
        t_inv = [t_inv[p] + _dot(t_inv[p], nt[p]) for p in probs]
        dd *= 2
    zero_half = jnp.zeros((PAIR // 2, PAIR), F32)
    while dd < L:
        lm_hi = ((ri_h // dd) == (ci_h // (2 * dd))) & ((ci_h % (2 * dd)) < dd)
        nt = [merge_rows(zero_half, jnp.where(lm_hi, _dot(take_rows(n_ab[p], dd, 1), t_inv[p]), 0.0), dd)
              for p in probs]
        t_hi = [take_rows(t_inv[p], dd, 1) for p in probs]
        t_inv = [merge_rows(take_rows(t_inv[p], dd, 0), t_hi[p] + _dot(t_hi[p], nt[p]), dd) for p in probs]
        dd *= 2
    t_inv = [t_inv[p].astype(BF16) for p in probs]

    for ch in range(n_chunks):
        cur = range(ch * per_chunk, (ch + 1) * per_chunk)
        slot = {p: p - ch * per_chunk for p in cur}
        s_prev = {p: state_ref[slot[p]] for p in cur}
        ars = {p: _dot_t(ar[p], s_prev[p]) for p in cur}
        u = {p: sel(_dot(t_inv[p], dup(ars[p][:L] + w0[p]))) for p in cur}
        o = {p: ars[p][L:] + sel(_dot(n_r[p], jnp.concatenate([dup(u[p]).astype(BF16), vv[p]], axis=0))) for p in cur}
        for p in cur:
            upd = _tdot(jnp.concatenate([u[p].astype(BF16), vv[p][:L]], axis=0), bk[p])
            state_ref[slot[p]] = s_prev[p] * e_l[p] + jnp.where(same_head, upd, 0.0)
        for p in cur:
            dlt = o[p] - head_sum(o[p]) * (1.0 / HEAD)
            var = head_sum(dlt * dlt) * (1.0 / HEAD)
            rk = tile(r_ref, p).astype(F32) * tile(k_ref, p).astype(F32)
            bonus = head_sum(rk * vec_ref[0:1, lanes[p]]) * tile(v_ref, p).astype(F32)
            y = dlt * lax.rsqrt(var + GN_EPS) * vec_ref[1:2, lanes[p]] + vec_ref[2:3, lanes[p]] + bonus
            y_ref[seq[p], rows[p], lanes[p]] = y.astype(y_ref.dtype)


def _rw_scan(r, k, v, lw, an, bn, vecs, *, tm=128, nb=4):
    bsz, seq, c = r.shape
    tm = min(tm, seq)
    nb = math.gcd(nb, bsz)
    tile = pl.BlockSpec((nb, tm, c), lambda b, i: (b, i, 0))
    const = lambda b, i: (0, 0)
    return pl.pallas_call(
        _rw_scan_kernel,
        grid=(bsz // nb, seq // tm),
        in_specs=[tile] * 6 + [pl.BlockSpec(vecs.shape, const)],
        out_specs=tile,
        out_shape=jax.ShapeDtypeStruct((bsz, seq, c), BF16),
        scratch_shapes=[pltpu.VMEM((nb * (c // PAIR), PAIR, PAIR), F32)],
        compiler_params=_cparams(("parallel", "arbitrary")),
        name="rwkv_scan",
    )(r, k, v, lw, an, bn, vecs)


def _s5_tables(a_re, a_im, log_dt, b_re, b_im, c_re, c_im):
    g, p = a_re.shape
    s = b_re.shape[-1]
    L = S5_CHUNK
    dt = jnp.exp(log_dt.astype(F32))[:, None]
    lam_re = jnp.minimum(a_re.astype(F32), -1e-4)
    lam_im = a_im.astype(F32)

    def apow(n):
        n = jnp.asarray(n, F32)[:, None, None]
        mag = jnp.exp(n * dt * lam_re)
        return mag * jnp.cos(n * dt * lam_im), mag * jnp.sin(n * dt * lam_im)

    ab_re, ab_im = apow([1])
    ab_re, ab_im = ab_re[0], ab_im[0]
    den = lam_re * lam_re + lam_im * lam_im
    nr, ni = ab_re - 1.0, ab_im
    coef_re = (nr * lam_re + ni * lam_im) / den
    coef_im = (ni * lam_re - nr * lam_im) / den
    bb_re = coef_re[..., None] * b_re - coef_im[..., None] * b_im
    bb_im = coef_re[..., None] * b_im + coef_im[..., None] * b_re
    cr, ci = c_re.astype(F32), c_im.astype(F32)

    pw_re, pw_im = apow(jnp.arange(L))
    cb_re = (jnp.einsum('gcp,tgp,gpd->gtcd', cr, pw_re, bb_re) - jnp.einsum('gcp,tgp,gpd->gtcd', cr, pw_im, bb_im)
             - jnp.einsum('gcp,tgp,gpd->gtcd', ci, pw_re, bb_im) - jnp.einsum('gcp,tgp,gpd->gtcd', ci, pw_im, bb_re))

    def group_mask(n_groups, n_cols, col_group_size):
        col_group = (jnp.arange(n_cols) // col_group_size) % n_groups
        return (jnp.arange(n_groups)[:, None] == col_group[None, :]).astype(F32)[None, None, :, None, :]

    gs = SLAB // s
    n_slab = g // gs
    m_lag = cb_re.reshape(n_slab, gs, L, s, s).transpose(0, 2, 4, 1, 3).reshape(n_slab, L, s, SLAB)
    w_lag = (m_lag[:, :, None, :, :] * group_mask(gs, SLAB, s)).astype(BF16).reshape(n_slab, L * SLAB, SLAB)

    gt = LANES // s
    n_t = g // gt
    st = 2 * gt * p
    rv_re, rv_im = pw_re[::-1], pw_im[::-1]
    in_re = rv_re[..., None] * bb_re[None] - rv_im[..., None] * bb_im[None]
    in_im = rv_re[..., None] * bb_im[None] + rv_im[..., None] * bb_re[None]
    m_in = jnp.stack([in_re, in_im], axis=0).reshape(2, L, n_t, gt, p, s)
    m_in = m_in.transpose(2, 1, 5, 0, 3, 4).reshape(n_t, L, s, st)
    b_cat = (m_in[:, :, None, :, :] * group_mask(gt, st, p)).astype(BF16).reshape(n_t, L * LANES, st)
    p1_re, p1_im = apow(jnp.arange(1, L + 1))
    out_re = cr[None] * p1_re[:, :, None, :] - ci[None] * p1_im[:, :, None, :]
    out_im = cr[None] * p1_im[:, :, None, :] + ci[None] * p1_re[:, :, None, :]
    m_out = jnp.stack([out_re, -out_im], axis=0).reshape(2, L, n_t, gt, s, p)
    m_out = m_out.transpose(2, 0, 5, 1, 3, 4).reshape(n_t, 2, p, L * LANES)
    c_cat = (m_out[:, :, None, :, :] * group_mask(gt, L * LANES, s)).astype(BF16).reshape(n_t, st, L * LANES)

    def lay_vec(re, im):
        n = re.shape[0]
        f = lambda m: m.reshape(n, n_t, gt * p).transpose(1, 0, 2)
        return jnp.concatenate([f(re), f(im)], axis=-1)

    step_pow = lay_vec(*apow([L * d for d in _SUBLANE_STEPS]))
    carry_pow = lay_vec(*apow(L * jnp.arange(1, SUBLANES + 1)))
    return w_lag, b_cat, c_cat, step_pow, carry_pow


def _cmul(ar, ai, br, bi):
    return ar * br - ai * bi, ar * bi + ai * br


def _s5_kernel(u_ref, wlag_ref, bcat_ref, ccat_ref, spow_ref, cpow_ref, dskip_ref, y_ref,
               carry_ref, ut_ref, zt_ref):
    tm = u_ref.shape[1]
    L = S5_CHUNK
    assert L == SUBLANES
    n_rows = tm // L
    half = bcat_ref.shape[2] // 2

    @pl.when(pl.program_id(2) == 0)
    def _():
        carry_ref[...] = jnp.zeros_like(carry_ref)

    u = u_ref[0]
    sub3 = lax.broadcasted_iota(jnp.int32, (1, SUBLANES, 1), 1)
    sub2 = lax.broadcasted_iota(jnp.int32, (SUBLANES, 1), 0)
    n_blk = n_rows // SUBLANES
    tiles = range(SLAB // LANES)

    x = []
    for t in tiles:
        ut_ref[t] = u[:, t * LANES:(t + 1) * LANES]
        ucat = jnp.concatenate([ut_ref[t, pl.ds(s, n_rows, stride=L), :] for s in range(L)], axis=1)
        x.append(_dot(ucat, bcat_ref[t]))

    u3 = u.reshape(tm // SUBLANES, SUBLANES, SLAB)
    shifted = [u.astype(BF16)]
    for tau in range(1, L):
        sh = jnp.where(sub3 >= tau, pltpu.roll(u3, tau, axis=1), 0.0)
        shifted.append(sh.reshape(tm, SLAB).astype(BF16))
    y_lag = jnp.dot(jnp.concatenate(shifted, axis=1), wlag_ref[0], preferred_element_type=F32)

    for t in tiles:
        hr = x[t][:, :half].reshape(n_blk, SUBLANES, half)
        hi = x[t][:, half:].reshape(n_blk, SUBLANES, half)
        for i, d in enumerate(_SUBLANE_STEPS):
            pr, pi = spow_ref[t, i:i + 1, :half], spow_ref[t, i:i + 1, half:]
            sr, si = _cmul(pr, pi, pltpu.roll(hr, d, axis=1), pltpu.roll(hi, d, axis=1))
            hr = hr + jnp.where(sub3 >= d, sr, 0.0)
            hi = hi + jnp.where(sub3 >= d, si, 0.0)
        cr, ci = carry_ref[t, :, :half], carry_ref[t, :, half:]
        cp_r, cp_i = cpow_ref[t, :, :half], cpow_ref[t, :, half:]
        prev_r, prev_i = [], []
        for kb in range(n_blk):
            kr, ki = _cmul(cp_r, cp_i, cr, ci)
            br, bi = hr[kb] + kr, hi[kb] + ki
            prev_r.append(jnp.where(sub2 == 0, cr, pltpu.roll(br, 1, axis=0)))
            prev_i.append(jnp.where(sub2 == 0, ci, pltpu.roll(bi, 1, axis=0)))
            cr, ci = br[SUBLANES - 1:SUBLANES, :], bi[SUBLANES - 1:SUBLANES, :]
        carry_ref[t, :, :half] = cr
        carry_ref[t, :, half:] = ci
        prev = jnp.concatenate([jnp.concatenate(prev_r, axis=0), jnp.concatenate(prev_i, axis=0)], axis=1)
        z = _dot(prev, ccat_ref[t])
        for s in range(L):
            zt_ref[t, pl.ds(s, n_rows, stride=L), :] = z[:, s * LANES:(s + 1) * LANES]

    y_state = jnp.concatenate([zt_ref[t] for t in tiles], axis=1)
    y_ref[0] = (y_lag + y_state + u * dskip_ref[...]).astype(y_ref.dtype)


def _s5_conv(h, tables, d_skip, *, tm=1024):
    bsz, seq, c = h.shape
    tm = min(tm, seq)
    w_lag, b_cat, c_cat, step_pow, carry_pow = tables
    n_slab = c // SLAB
    tps = SLAB // LANES
    st = b_cat.shape[2]
    return pl.pallas_call(
        _s5_kernel,
        grid=(n_slab, bsz, seq // tm),
        in_specs=[
            pl.BlockSpec((1, tm, SLAB), lambda s, b, i: (b, i, s)),
            pl.BlockSpec((1,) + w_lag.shape[1:], lambda s, b, i: (s, 0, 0)),
            pl.BlockSpec((tps,) + b_cat.shape[1:], lambda s, b, i: (s, 0, 0)),
            pl.BlockSpec((tps,) + c_cat.shape[1:], lambda s, b, i: (s, 0, 0)),
            pl.BlockSpec((tps,) + step_pow.shape[1:], lambda s, b, i: (s, 0, 0)),
            pl.BlockSpec((tps,) + carry_pow.shape[1:], lambda s, b, i: (s, 0, 0)),
            pl.BlockSpec((1, SLAB), lambda s, b, i: (0, s)),
        ],
        out_specs=pl.BlockSpec((1, tm, SLAB), lambda s, b, i: (b, i, s)),
        out_shape=jax.ShapeDtypeStruct((bsz, seq, c), BF16),
        scratch_shapes=[pltpu.VMEM((tps, 1, st), F32), pltpu.VMEM((tps, tm, LANES), F32),
                        pltpu.VMEM((tps, tm, LANES), F32)],
        compiler_params=_cparams(("parallel", "parallel", "arbitrary")),
        name="s5_conv",
    )(h, w_lag, b_cat, c_cat, step_pow, carry_pow, d_skip.reshape(1, c))


def _rwkv_block(h, ln0, ln1, mu, w0, w1, w2, a0, a1, a2, g1, g2, k_k, k_a, r_k, wr, wk, wv, wo,
                lnx_g, lnx_b, mlp_w1, mlp_w2, layer):
    bsz, seq, c = h.shape
    zeros = jnp.zeros((6, c), F32)
    vecs = jnp.concatenate([mu, w0[None], a0[None], k_k[None], k_a[None], zeros], axis=0)
    scan_vecs = jnp.concatenate([r_k.reshape(1, c), lnx_g[None], lnx_b[None], zeros[:5]], axis=0)
    idx = jnp.arange(SLAB) // HEAD
    jmat = (idx[:, None] == idx[None, :]).astype(BF16)
    r, k, v, lw, an, bn, gate = _rw_proj(h, vecs, wr, wk, wv, w1, w2, a1, a2, g1, g2, jmat)
    y = _rw_scan(r, k, v, lw, an, bn, scan_vecs)
    n = bsz * seq
    out = _mixer_out_mlp(_gate_mlp_kernel, "rwkv_out_mlp", [h.reshape(n, c), y.reshape(n, c), gate.reshape(n, c)],
                         wo, ln0, mlp_w1, mlp_w2, layer, ln1)
    return out.reshape(bsz, seq, c)


def _s5_block(h, ln0, ln1, a_re, a_im, log_dt, b_re, b_im, c_re, c_im, d_skip, w_glu, mlp_w1, mlp_w2, layer):
    bsz, seq, c = h.shape
    tables = _s5_tables(a_re, a_im, log_dt, b_re, b_im, c_re, c_im)
    y = _s5_conv(h, tables, d_skip)
    n = bsz * seq
    out = _mixer_out_mlp(_glu_mlp_kernel, "s5_out_mlp", [h.reshape(n, c), y.reshape(n, c)],
                         w_glu, ln0, mlp_w1, mlp_w2, layer, ln1)
    return out.reshape(bsz, seq, c)


def kernel(x, ln_g, ln_b, rw_mu, rw_w0, rw_w1, rw_w2, rw_a0, rw_a1, rw_a2, rw_g1, rw_g2, rw_k_k, rw_k_a, rw_r_k, rw_wr, rw_wk, rw_wv, rw_wo, rw_lnx_g, rw_lnx_b, s5_a_re, s5_a_im, s5_log_dt, s5_b_re, s5_b_im, s5_c_re, s5_c_im, s5_d, s5_w_glu, mlp_w1, mlp_w2):
    w1b, w2b = mlp_w1.astype(BF16), mlp_w2.astype(BF16)
    h = _rwkv_block(x, (ln_g[0], ln_b[0]), (ln_g[1], ln_b[1]), rw_mu[0], rw_w0[0], rw_w1[0], rw_w2[0],
                    rw_a0[0], rw_a1[0], rw_a2[0], rw_g1[0], rw_g2[0], rw_k_k[0], rw_k_a[0], rw_r_k[0],
                    rw_wr[0], rw_wk[0], rw_wv[0], rw_wo[0], rw_lnx_g[0], rw_lnx_b[0], w1b, w2b, 0)
    h = _s5_block(h, (ln_g[2], ln_b[2]), (ln_g[3], ln_b[3]), s5_a_re[0], s5_a_im[0], s5_log_dt[0],
                  s5_b_re[0], s5_b_im[0], s5_c_re[0], s5_c_im[0], s5_d[0], s5_w_glu[0], w1b, w2b, 1)
    return h
```

```python
import functools
import math

import jax
import jax.numpy as jnp
from jax import lax
from jax.experimental import pallas as pl
from jax.experimental.pallas import tpu as pltpu

F32 = jnp.float32
BF16 = jnp.bfloat16

DEPTH = 2
DN_ALPHA = (2.0 * DEPTH) ** 0.25
LN_EPS = 1e-5
GN_EPS = 64e-5

LANES = 128
SUBLANES = 8
_SUBLANE_STEPS = (1, 2, 4)
HEAD = 64
PAIR = 2 * HEAD
SLAB = 256
RW_CHUNK = 64
S5_CHUNK = 8
VMEM_LIMIT = 56 * 1024 * 1024


def _cparams(sem):
    return pltpu.CompilerParams(dimension_semantics=sem, vmem_limit_bytes=VMEM_LIMIT)


def _layer_norm(z, g, b):
    mu = jnp.mean(z, axis=-1, keepdims=True)
    d = z - mu
    var = jnp.mean(d * d, axis=-1, keepdims=True)
    return d * lax.rsqrt(var + LN_EPS) * g + b


def _dot(a, b):
    return jnp.dot(a.astype(BF16), b.astype(BF16), preferred_element_type=F32)


def _dot_t(a, b):
    return lax.dot_general(a.astype(BF16), b.astype(BF16), (((1,), (1,)), ((), ())),
                           preferred_element_type=F32)


def _tdot(a, b):
    return lax.dot_general(a.astype(BF16), b.astype(BF16), (((0,), (0,)), ((), ())),
                           preferred_element_type=F32)


MLP_FF_CHUNK = 1024


ROW_SPLIT = 2


def _mixer_out_mlp_body(mix_fn, x_ref, g0_ref, b0_ref, w1_ref, w2_ref, g1_ref, b1_ref, o_ref):
    tm = x_ref.shape[0]
    d_ff = w1_ref.shape[1]
    chunk = min(MLP_FF_CHUNK, d_ff)
    groups = [slice(i * tm // ROW_SPLIT, (i + 1) * tm // ROW_SPLIT) for i in range(ROW_SPLIT)]
    mix = [mix_fn(rows) for rows in groups]
    h = [_layer_norm(DN_ALPHA * x_ref[rows, :] + m, g0_ref[...], b0_ref[...]) for rows, m in zip(groups, mix)]
    hb = [v.astype(BF16) for v in h]
    acc = [jnp.zeros(v.shape, F32) for v in h]
    for c in range(d_ff // chunk):
        up = [jnp.dot(v, w1_ref[:, c * chunk:(c + 1) * chunk], preferred_element_type=F32) for v in hb]
        act = [jnp.square(jnp.maximum(u, 0.0)).astype(BF16) for u in up]
        acc = [s + jnp.dot(a, w2_ref[c * chunk:(c + 1) * chunk, :], preferred_element_type=F32)
               for s, a in zip(acc, act)]
    for rows, v, s in zip(groups, h, acc):
        o_ref[rows, :] = _layer_norm(DN_ALPHA * v + s, g1_ref[...], b1_ref[...])


def _gate_mlp_kernel(x_ref, y_ref, gate_ref, wo_ref, g0_ref, b0_ref, w1_ref, w2_ref, g1_ref, b1_ref, o_ref):
    def mix(rows):
        return _dot(y_ref[rows, :].astype(F32) * gate_ref[rows, :].astype(F32), wo_ref[...])

    _mixer_out_mlp_body(mix, x_ref, g0_ref, b0_ref, w1_ref, w2_ref, g1_ref, b1_ref, o_ref)


def _glu_mlp_kernel(x_ref, y_ref, wglu_ref, g0_ref, b0_ref, w1_ref, w2_ref, g1_ref, b1_ref, o_ref):
    c = x_ref.shape[1]

    def mix(rows):
        y = y_ref[rows, :].astype(F32)
        y = 0.5 * y * (1.0 + jnp.tanh(math.sqrt(2.0 / math.pi) * (y + 0.044715 * (y * y * y))))
        z = _dot(y, wglu_ref[...])
        return z[:, :c] * jax.nn.sigmoid(z[:, c:])

    _mixer_out_mlp_body(mix, x_ref, g0_ref, b0_ref, w1_ref, w2_ref, g1_ref, b1_ref, o_ref)


def _mixer_out_mlp(body, name, row_inputs, w_mix, ln0, mlp_w1, mlp_w2, layer, ln1, *, tm=512):
    n, c = row_inputs[0].shape
    tm = min(tm, n)
    tile = pl.BlockSpec((tm, c), lambda i: (i, 0))
    const = lambda i: (0, 0)
    vec = pl.BlockSpec((1, c), const)
    w_mix = w_mix.astype(BF16)

    def stacked(w):
        return pl.BlockSpec((None,) + w.shape[1:], lambda i: (layer, 0, 0), pipeline_mode=pl.Buffered(1))

    return pl.pallas_call(
        body,
        grid=(n // tm,),
        in_specs=[tile] * len(row_inputs) + [pl.BlockSpec(w_mix.shape, const, pipeline_mode=pl.Buffered(1)),
                                             vec, vec, stacked(mlp_w1), stacked(mlp_w2), vec, vec],
        out_specs=tile,
        out_shape=jax.ShapeDtypeStruct((n, c), F32),
        compiler_params=_cparams(("parallel",)),
        name=name,
    )(*row_inputs, w_mix, ln0[0].reshape(1, c), ln0[1].reshape(1, c), mlp_w1, mlp_w2,
      ln1[0].reshape(1, c), ln1[1].reshape(1, c))


_V_MU, _V_W0, _V_A0, _V_KK, _V_KA = 0, 6, 7, 8, 9


def _head_sum(x, j_ref):
    parts = [_dot(x[:, s:s + SLAB], j_ref[...]) for s in range(0, x.shape[1], SLAB)]
    return jnp.concatenate(parts, axis=1)


def _rw_proj_kernel(x_ref, xp_ref, vec_ref, wr_ref, wk_ref, wv_ref, w1_ref, w2_ref, a1_ref, a2_ref,
                    g1_ref, g2_ref, j_ref, r_ref, k_ref, v_ref, lw_ref, an_ref, bn_ref, g_ref):
    x = x_ref[0]
    tm = x.shape[0]
    row = lax.broadcasted_iota(jnp.int32, (tm, 1), 0)
    prev_last = jnp.where(pl.program_id(1) == 0, 0.0, xp_ref[0][7:8, :])
    xprev = jnp.where(row == 0, prev_last, pltpu.roll(x, 1, axis=0))
    xx = xprev - x

    def mix(i):
        return x + xx * vec_ref[_V_MU + i:_V_MU + i + 1, :]

    r = _dot(mix(0), wr_ref[...])
    w_pre = vec_ref[_V_W0:_V_W0 + 1, :] + _dot(jnp.tanh(_dot(mix(1), w1_ref[...])), w2_ref[...])
    k = _dot(mix(2), wk_ref[...])
    v = _dot(mix(3), wv_ref[...])
    a = jax.nn.sigmoid(vec_ref[_V_A0:_V_A0 + 1, :] + _dot(_dot(mix(4), a1_ref[...]), a2_ref[...]))
    g = _dot(jax.nn.sigmoid(_dot(mix(5), g1_ref[...])), g2_ref[...])

    kk = k * vec_ref[_V_KK:_V_KK + 1, :]
    kk = kk / jnp.maximum(jnp.sqrt(_head_sum(kk * kk, j_ref)), 1e-12)
    r_ref[0] = r.astype(r_ref.dtype)
    k_ref[0] = (k * (1.0 + (a - 1.0) * vec_ref[_V_KA:_V_KA + 1, :])).astype(k_ref.dtype)
    v_ref[0] = v.astype(v_ref.dtype)
    lw_ref[0] = (-math.exp(-0.5)) * jax.nn.sigmoid(w_pre)
    an_ref[0] = (-kk).astype(an_ref.dtype)
    bn_ref[0] = (kk * a).astype(bn_ref.dtype)
    g_ref[0] = g.astype(g_ref.dtype)


def _rw_proj(x, vecs, wr, wk, wv, w1, w2, a1, a2, g1, g2, jmat, *, tm=512):
    bsz, seq, c = x.shape
    tm = min(tm, seq)
    const = lambda b, i: (0, 0)
    tile = pl.BlockSpec((1, tm, c), lambda b, i: (b, i, 0))
    prev = pl.BlockSpec((1, 8, c), lambda b, i: (b, jnp.maximum(i * (tm // 8) - 1, 0), 0))

    def full(w):
        return pl.BlockSpec(w.shape, const)

    ws = [w.astype(BF16) for w in (wr, wk, wv, w1, w2, a1, a2, g1, g2)]
    out = lambda dt: jax.ShapeDtypeStruct((bsz, seq, c), dt)
    return pl.pallas_call(
        _rw_proj_kernel,
        grid=(bsz, seq // tm),
        in_specs=[tile, prev, full(vecs)] + [full(w) for w in ws] + [full(jmat)],
        out_specs=[tile] * 7,
        out_shape=[out(BF16), out(BF16), out(BF16), out(F32), out(BF16), out(BF16), out(BF16)],
        compiler_params=_cparams(("parallel", "arbitrary")),
        name="rwkv_proj",
    )(x, x, vecs, *ws, jmat)


def _rw_scan_kernel(r_ref, k_ref, v_ref, lw_ref, an_ref, bn_ref, vec_ref, y_ref, state_ref):
    nb, tm, c = r_ref.shape
    n_pairs = c // PAIR
    L = RW_CHUNK

    @pl.when(pl.program_id(1) == 0)
    def _():
        state_ref[...] = jnp.zeros_like(state_ref)

    sub3 = lax.broadcasted_iota(jnp.int32, (1, SUBLANES, 1), 1)

    def chunk_cumsum(x):
        x3 = x.reshape(L // SUBLANES, SUBLANES, x.shape[1])
        for d in _SUBLANE_STEPS:
            x3 = x3 + jnp.where(sub3 >= d, pltpu.roll(x3, d, axis=1), 0.0)
        blocks, run = [], None
        for kb in range(L // SUBLANES):
            blk = x3[kb] if run is None else x3[kb] + run
            blocks.append(blk)
            run = blk[SUBLANES - 1:SUBLANES, :]
        return jnp.concatenate(blocks, axis=0)

    ri = lax.broadcasted_iota(jnp.int32, (PAIR, PAIR), 0)
    ci = lax.broadcasted_iota(jnp.int32, (PAIR, PAIR), 1)
    same_head = (ri // L) == (ci // L)
    mask_strict = same_head & ((ci % L) < (ri % L))
    mask_incl = same_head & ((ci % L) <= (ri % L))
    eye = (ri == ci).astype(F32)
    lane = lax.broadcasted_iota(jnp.int32, (L, PAIR), 1)
    head0 = lane < HEAD
    ri_h = lax.broadcasted_iota(jnp.int32, (PAIR // 2, PAIR), 0)
    ci_h = lax.broadcasted_iota(jnp.int32, (PAIR // 2, PAIR), 1)

    def level_mask(dd):
        return same_head & ((ri // (2 * dd)) == (ci // (2 * dd))) & ((ri % (2 * dd)) >= dd) & ((ci % (2 * dd)) < dd)

    def sel(res):
        return jnp.where(head0, res[:L], res[L:])

    def dup(x):
        return jnp.concatenate([x, x], axis=0)

    def head_sum(x):
        s0 = jnp.sum(jnp.where(head0, x, 0.0), axis=1, keepdims=True)
        s1 = jnp.sum(jnp.where(head0, 0.0, x), axis=1, keepdims=True)
        return jnp.where(head0, s0, s1)

    def take_rows(x, dd, half):
        return jnp.concatenate([x[(2 * b + half) * dd:(2 * b + half + 1) * dd]
                                for b in range(PAIR // (2 * dd))], axis=0)

    def merge_rows(first, second, dd):
        parts = []
        for b in range(PAIR // (2 * dd)):
            parts += [first[b * dd:(b + 1) * dd], second[b * dd:(b + 1) * dd]]
        return jnp.concatenate(parts, axis=0)

    n_chunks = tm // L
    per_chunk = nb * n_pairs
    probs = range(n_chunks * per_chunk)
    rows = [slice((p // per_chunk) * L, (p // per_chunk + 1) * L) for p in probs]
    seq = [(p % per_chunk) // n_pairs for p in probs]
    lanes = [slice((p % n_pairs) * PAIR, (p % n_pairs + 1) * PAIR) for p in probs]

    def tile(ref, p):
        return ref[seq[p], rows[p], lanes[p]]

    rhs, ar, bk, vv, e_l = [], [], [], [], []
    for p in probs:
        r = tile(r_ref, p).astype(F32)
        k = tile(k_ref, p).astype(F32)
        a = tile(an_ref, p).astype(F32)
        b = tile(bn_ref, p).astype(F32)
        lw = tile(lw_ref, p)
        cs_c = chunk_cumsum(lw)
        cs_last = cs_c[L - 1:L, :]
        a_t = a * jnp.exp(cs_c - lw)
        r_t = r * jnp.exp(cs_c)
        e_m = jnp.exp(-cs_c)
        b_t = b * e_m
        k_t = k * e_m
        e_d = jnp.exp(cs_last - cs_c)
        rhs.append(jnp.concatenate([jnp.where(head0, b_t, 0.0), jnp.where(head0, 0.0, b_t),
                                    jnp.where(head0, k_t, 0.0), jnp.where(head0, 0.0, k_t)], axis=0).astype(BF16))
        ar.append(jnp.concatenate([a_t, r_t], axis=0).astype(BF16))
        bk.append(jnp.concatenate([b * e_d, k * e_d], axis=0).astype(BF16))
        vv.append(dup(tile(v_ref, p)))
        e_l.append(jnp.exp(cs_last))

    sc = [_dot_t(ar[p], rhs[p]) for p in probs]
    n_ab = [jnp.where(mask_strict, dup(sc[p][:L, :PAIR]), 0.0) for p in probs]
    n_ak = [jnp.where(mask_strict, dup(sc[p][:L, PAIR:]), 0.0).astype(BF16) for p in probs]
    n_r = [jnp.concatenate([jnp.where(mask_incl, dup(sc[p][L:, :PAIR]), 0.0),
                            jnp.where(mask_incl, dup(sc[p][L:, PAIR:]), 0.0)], axis=1).astype(BF16) for p in probs]
    w0 = [sel(_dot(n_ak[p], vv[p])) for p in probs]

    t_inv = [jnp.where(level_mask(1), n_ab[p], eye) for p in probs]
    n_ab_b = [n_ab[p].astype(BF16) for p in probs]
    dd = 2
    while dd < SUBLANES:
        lm = level_mask(dd)
        nt = [jnp.where(lm, _dot(n_ab_b[p], t_inv[p]), 0.0) for p in probs]
        t_inv = [t_inv[p] + _dot(t_inv[p], nt[p]) for p in probs]
        dd *= 2
    zero_half = jnp.zeros((PAIR // 2, PAIR), F32)
    while dd < L:
        lm_hi = ((ri_h // dd) == (ci_h // (2 * dd))) & ((ci_h % (2 * dd)) < dd)
        nt = [merge_rows(zero_half, jnp.where(lm_hi, _dot(take_rows(n_ab[p], dd, 1), t_inv[p]), 0.0), dd)
              for p in probs]
        t_hi = [take_rows(t_inv[p], dd, 1) for p in probs]
        t_inv = [merge_rows(take_rows(t_inv[p], dd, 0), t_hi[p] + _dot(t_hi[p], nt[p]), dd) for p in probs]
        dd *= 2
    t_inv = [t_inv[p].astype(BF16) for p in probs]

    for ch in range(n_chunks):
        cur = range(ch * per_chunk, (ch + 1) * per_chunk)
        slot = {p: p - ch * per_chunk for p in cur}
        s_prev = {p: state_ref[slot[p]] for p in cur}
        ars = {p: _dot_t(ar[p], s_prev[p]) for p in cur}
        u = {p: sel(_dot(t_inv[p], dup(ars[p][:L] + w0[p]))) for p in cur}
        o = {p: ars[p][L:] + sel(_dot(n_r[p], jnp.concatenate([dup(u[p]).astype(BF16), vv[p]], axis=0))) for p in cur}
        for p in cur:
            upd = _tdot(jnp.concatenate([u[p].astype(BF16), vv[p][:L]], axis=0), bk[p])
            state_ref[slot[p]] = s_prev[p] * e_l[p] + jnp.where(same_head, upd, 0.0)
        for p in cur:
            dlt = o[p] - head_sum(o[p]) * (1.0 / HEAD)
            var = head_sum(dlt * dlt) * (1.0 / HEAD)
            rk = tile(r_ref, p).astype(F32) * tile(k_ref, p).astype(F32)
            bonus = head_sum(rk * vec_ref[0:1, lanes[p]]) * tile(v_ref, p).astype(F32)
            y = dlt * lax.rsqrt(var + GN_EPS) * vec_ref[1:2, lanes[p]] + vec_ref[2:3, lanes[p]] + bonus
            y_ref[seq[p], rows[p], lanes[p]] = y.astype(y_ref.dtype)


def _rw_scan(r, k, v, lw, an, bn, vecs, *, tm=128, nb=4):
    bsz, seq, c = r.shape
    tm = min(tm, seq)
    nb = math.gcd(nb, bsz)
    tile = pl.BlockSpec((nb, tm, c), lambda b, i: (b, i, 0))
    const = lambda b, i: (0, 0)
    return pl.pallas_call(
        _rw_scan_kernel,
        grid=(bsz // nb, seq // tm),
        in_specs=[tile] * 6 + [pl.BlockSpec(vecs.shape, const)],
        out_specs=tile,
        out_shape=jax.ShapeDtypeStruct((bsz, seq, c), BF16),
        scratch_shapes=[pltpu.VMEM((nb * (c // PAIR), PAIR, PAIR), F32)],
        compiler_params=_cparams(("parallel", "arbitrary")),
        name="rwkv_scan",
    )(r, k, v, lw, an, bn, vecs)


def _s5_tables(a_re, a_im, log_dt, b_re, b_im, c_re, c_im):
    g, p = a_re.shape
    s = b_re.shape[-1]
    L = S5_CHUNK
    dt = jnp.exp(log_dt.astype(F32))[:, None]
    lam_re = jnp.minimum(a_re.astype(F32), -1e-4)
    lam_im = a_im.astype(F32)

    def apow(n):
        n = jnp.asarray(n, F32)[:, None, None]
        mag = jnp.exp(n * dt * lam_re)
        return mag * jnp.cos(n * dt * lam_im), mag * jnp.sin(n * dt * lam_im)

    ab_re, ab_im = apow([1])
    ab_re, ab_im = ab_re[0], ab_im[0]
    den = lam_re * lam_re + lam_im * lam_im
    nr, ni = ab_re - 1.0, ab_im
    coef_re = (nr * lam_re + ni * lam_im) / den
    coef_im = (ni * lam_re - nr * lam_im) / den
    bb_re = coef_re[..., None] * b_re - coef_im[..., None] * b_im
    bb_im = coef_re[..., None] * b_im + coef_im[..., None] * b_re
    cr, ci = c_re.astype(F32), c_im.astype(F32)

    pw_re, pw_im = apow(jnp.arange(L))
    cb_re = (jnp.einsum('gcp,tgp,gpd->gtcd', cr, pw_re, bb_re) - jnp.einsum('gcp,tgp,gpd->gtcd', cr, pw_im, bb_im)
             - jnp.einsum('gcp,tgp,gpd->gtcd', ci, pw_re, bb_im) - jnp.einsum('gcp,tgp,gpd->gtcd', ci, pw_im, bb_re))

    gs = SLAB // s
    n_slab = g // gs
    m_lag = cb_re.reshape(n_slab, gs, L, s, s).transpose(0, 2, 4, 1, 3).reshape(n_slab, L, s, SLAB)

    gt = LANES // s
    n_t = g // gt
    st = 2 * gt * p
    rv_re, rv_im = pw_re[::-1], pw_im[::-1]
    in_re = rv_re[..., None] * bb_re[None] - rv_im[..., None] * bb_im[None]
    in_im = rv_re[..., None] * bb_im[None] + rv_im[..., None] * bb_re[None]
    m_in = jnp.stack([in_re, in_im], axis=0).reshape(2, L, n_t, gt, p, s)
    m_in = m_in.transpose(2, 1, 5, 0, 3, 4).reshape(n_t, L, s, st)
    p1_re, p1_im = apow(jnp.arange(1, L + 1))
    out_re = cr[None] * p1_re[:, :, None, :] - ci[None] * p1_im[:, :, None, :]
    out_im = cr[None] * p1_im[:, :, None, :] + ci[None] * p1_re[:, :, None, :]
    m_out = jnp.stack([out_re, -out_im], axis=0).reshape(2, L, n_t, gt, s, p)
    m_out = m_out.transpose(2, 0, 5, 1, 3, 4).reshape(n_t, 2, p, L * LANES)

    def lay_vec(re, im):
        n = re.shape[0]
        f = lambda m: m.reshape(n, n_t, gt * p).transpose(1, 0, 2)
        return jnp.concatenate([f(re), f(im)], axis=-1)

    step_pow = lay_vec(*apow([L * d for d in _SUBLANE_STEPS]))
    carry_pow = lay_vec(*apow(L * jnp.arange(1, SUBLANES + 1)))
    return m_lag, m_in, m_out, step_pow, carry_pow


def _cmul(ar, ai, br, bi):
    return ar * br - ai * bi, ar * bi + ai * br


def _s5_kernel(u_ref, mlag_ref, min_ref, mout_ref, spow_ref, cpow_ref, dskip_ref, y_ref,
               carry_ref, ut_ref, zt_ref, wlag_ref, bcat_ref, ccat_ref):
    tm = u_ref.shape[1]
    L = S5_CHUNK
    assert L == SUBLANES
    n_rows = tm // L
    half = bcat_ref.shape[2] // 2
    tiles = range(SLAB // LANES)

    @pl.when(pl.program_id(2) == 0)
    def _():
        carry_ref[...] = jnp.zeros_like(carry_ref)

    @pl.when((pl.program_id(1) == 0) & (pl.program_id(2) == 0))
    def _():
        def expand(dst, table, n_groups, col_group_size):
            rows, cols = table.shape
            col_group = (lax.broadcasted_iota(jnp.int32, (1, cols), 1) // col_group_size) % n_groups
            for grp in range(n_groups):
                dst[pl.ds(grp * rows, rows), :] = jnp.where(col_group == grp, table, 0.0).astype(dst.dtype)

        s_ch, p = min_ref.shape[2], mout_ref.shape[2]
        for tau in range(L):
            n_grp = SLAB // s_ch
            expand(wlag_ref.at[pl.ds(tau * SLAB, SLAB), :], mlag_ref[0, tau], n_grp, s_ch)
        for t in tiles:
            n_grp = LANES // s_ch
            for s in range(L):
                expand(bcat_ref.at[t, pl.ds(s * LANES, LANES), :], min_ref[t, s], n_grp, p)
            for ri in range(2):
                expand(ccat_ref.at[t, pl.ds(ri * n_grp * p, n_grp * p), :], mout_ref[t, ri], n_grp, s_ch)

    u = u_ref[0]
    sub3 = lax.broadcasted_iota(jnp.int32, (1, SUBLANES, 1), 1)
    sub2 = lax.broadcasted_iota(jnp.int32, (SUBLANES, 1), 0)
    n_blk = n_rows // SUBLANES

    x = []
    for t in tiles:
        ut_ref[t] = u[:, t * LANES:(t + 1) * LANES]
        ucat = jnp.concatenate([ut_ref[t, pl.ds(s, n_rows, stride=L), :] for s in range(L)], axis=1)
        x.append(_dot(ucat, bcat_ref[t]))

    u3 = u.reshape(tm // SUBLANES, SUBLANES, SLAB)
    shifted = [u.astype(BF16)]
    for tau in range(1, L):
        sh = jnp.where(sub3 >= tau, pltpu.roll(u3, tau, axis=1), 0.0)
        shifted.append(sh.reshape(tm, SLAB).astype(BF16))
    y_lag = jnp.dot(jnp.concatenate(shifted, axis=1), wlag_ref[...], preferred_element_type=F32)

    for t in tiles:
        hr = x[t][:, :half].reshape(n_blk, SUBLANES, half)
        hi = x[t][:, half:].reshape(n_blk, SUBLANES, half)
        for i, d in enumerate(_SUBLANE_STEPS):
            pr, pi = spow_ref[t, i:i + 1, :half], spow_ref[t, i:i + 1, half:]
            sr, si = _cmul(pr, pi, pltpu.roll(hr, d, axis=1), pltpu.roll(hi, d, axis=1))
            hr = hr + jnp.where(sub3 >= d, sr, 0.0)
            hi = hi + jnp.where(sub3 >= d, si, 0.0)
        cr, ci = carry_ref[t, :, :half], carry_ref[t, :, half:]
        cp_r, cp_i = cpow_ref[t, :, :half], cpow_ref[t, :, half:]
        prev_r, prev_i = [], []
        for kb in range(n_blk):
            kr, ki = _cmul(cp_r, cp_i, cr, ci)
            br, bi = hr[kb] + kr, hi[kb] + ki
            prev_r.append(jnp.where(sub2 == 0, cr, pltpu.roll(br, 1, axis=0)))
            prev_i.append(jnp.where(sub2 == 0, ci, pltpu.roll(bi, 1, axis=0)))
            cr, ci = br[SUBLANES - 1:SUBLANES, :], bi[SUBLANES - 1:SUBLANES, :]
        carry_ref[t, :, :half] = cr
        carry_ref[t, :, half:] = ci
        prev = jnp.concatenate([jnp.concatenate(prev_r, axis=0), jnp.concatenate(prev_i, axis=0)], axis=1)
        z = _dot(prev, ccat_ref[t])
        for s in range(L):
            zt_ref[t, pl.ds(s, n_rows, stride=L), :] = z[:, s * LANES:(s + 1) * LANES]

    y_state = jnp.concatenate([zt_ref[t] for t in tiles], axis=1)
    y_ref[0] = (y_lag + y_state + u * dskip_ref[...]).astype(y_ref.dtype)


def _s5_conv(h, tables, d_skip, *, tm=1024):
    bsz, seq, c = h.shape
    tm = min(tm, seq)
    m_lag, m_in, m_out, step_pow, carry_pow = tables
    n_slab = c // SLAB
    tps = SLAB // LANES
    st = m_in.shape[-1]
    chunk_cols = m_out.shape[-1]
    return pl.pallas_call(
        _s5_kernel,
        grid=(n_slab, bsz, seq // tm),
        in_specs=[
            pl.BlockSpec((1, tm, SLAB), lambda s, b, i: (b, i, s)),
            pl.BlockSpec((1,) + m_lag.shape[1:], lambda s, b, i: (s, 0, 0, 0)),
            pl.BlockSpec((tps,) + m_in.shape[1:], lambda s, b, i: (s, 0, 0, 0)),
            pl.BlockSpec((tps,) + m_out.shape[1:], lambda s, b, i: (s, 0, 0, 0)),
            pl.BlockSpec((tps,) + step_pow.shape[1:], lambda s, b, i: (s, 0, 0)),
            pl.BlockSpec((tps,) + carry_pow.shape[1:], lambda s, b, i: (s, 0, 0)),
            pl.BlockSpec((1, SLAB), lambda s, b, i: (0, s)),
        ],
        out_specs=pl.BlockSpec((1, tm, SLAB), lambda s, b, i: (b, i, s)),
        out_shape=jax.ShapeDtypeStruct((bsz, seq, c), BF16),
        scratch_shapes=[pltpu.VMEM((tps, 1, st), F32), pltpu.VMEM((tps, tm, LANES), F32),
                        pltpu.VMEM((tps, tm, LANES), F32),
                        pltpu.VMEM((S5_CHUNK * SLAB, SLAB), BF16),
                        pltpu.VMEM((tps, chunk_cols, st), BF16),
                        pltpu.VMEM((tps, st, chunk_cols), BF16)],
        compiler_params=_cparams(("arbitrary", "arbitrary", "arbitrary")),
        name="s5_conv",
    )(h, m_lag, m_in, m_out, step_pow, carry_pow, d_skip.reshape(1, c))


def _rwkv_block(h, ln0, ln1, mu, w0, w1, w2, a0, a1, a2, g1, g2, k_k, k_a, r_k, wr, wk, wv, wo,
                lnx_g, lnx_b, mlp_w1, mlp_w2, layer):
    bsz, seq, c = h.shape
    zeros = jnp.zeros((6, c), F32)
    vecs = jnp.concatenate([mu, w0[None], a0[None], k_k[None], k_a[None], zeros], axis=0)
    scan_vecs = jnp.concatenate([r_k.reshape(1, c), lnx_g[None], lnx_b[None], zeros[:5]], axis=0)
    idx = jnp.arange(SLAB) // HEAD
    jmat = (idx[:, None] == idx[None, :]).astype(BF16)
    r, k, v, lw, an, bn, gate = _rw_proj(h, vecs, wr, wk, wv, w1, w2, a1, a2, g1, g2, jmat)
    y = _rw_scan(r, k, v, lw, an, bn, scan_vecs)
    n = bsz * seq
    out = _mixer_out_mlp(_gate_mlp_kernel, "rwkv_out_mlp", [h.reshape(n, c), y.reshape(n, c), gate.reshape(n, c)],
                         wo, ln0, mlp_w1, mlp_w2, layer, ln1)
    return out.reshape(bsz, seq, c)


def _s5_block(h, ln0, ln1, a_re, a_im, log_dt, b_re, b_im, c_re, c_im, d_skip, w_glu, mlp_w1, mlp_w2, layer):
    bsz, seq, c = h.shape
    tables = _s5_tables(a_re, a_im, log_dt, b_re, b_im, c_re, c_im)
    y = _s5_conv(h, tables, d_skip)
    n = bsz * seq
    out = _mixer_out_mlp(_glu_mlp_kernel, "s5_out_mlp", [h.reshape(n, c), y.reshape(n, c)],
                         w_glu, ln0, mlp_w1, mlp_w2, layer, ln1)
    return out.reshape(bsz, seq, c)


def kernel(x, ln_g, ln_b, rw_mu, rw_w0, rw_w1, rw_w2, rw_a0, rw_a1, rw_a2, rw_g1, rw_g2, rw_k_k, rw_k_a, rw_r_k, rw_wr, rw_wk, rw_wv, rw_wo, rw_lnx_g, rw_lnx_b, s5_a_re, s5_a_im, s5_log_dt, s5_b_re, s5_b_im, s5_c_re, s5_c_im, s5_d, s5_w_glu, mlp_w1, mlp_w2):
    w1b, w2b = mlp_w1.astype(BF16), mlp_w2.astype(BF16)
    h = _rwkv_block(x, (ln_g[0], ln_b[0]), (ln_g[1], ln_b[1]), rw_mu[0], rw_w0[0], rw_w1[0], rw_w2[0],
                    rw_a0[0], rw_a1[0], rw_a2[0], rw_g1[0], rw_g2[0], rw_k_k[0], rw_k_a[0], rw_r_k[0],
                    rw_wr[0], rw_wk[0], rw_wv[0], rw_wo[0], rw_lnx_g[0], rw_lnx_b[0], w1b, w2b, 0)
    h = _s5_block(h, (ln_g[2], ln_b[2]), (ln_g[3], ln_b[3]), s5_a_re[0], s5_a_im[0], s5_log_dt[0],
                  s5_b_re[0], s5_b_im[0], s5_c_re[0], s5_c_im[0], s5_d[0], s5_w_glu[0], w1b, w2b, 1)
    return h
```

```python
import functools
import math

import jax
import jax.numpy as jnp
from jax import lax
from jax.experimental import pallas as pl
from jax.experimental.pallas import tpu as pltpu

F32 = jnp.float32
BF16 = jnp.bfloat16

DEPTH = 2
DN_ALPHA = (2.0 * DEPTH) ** 0.25
LN_EPS = 1e-5
GN_EPS = 64e-5

LANES = 128
SUBLANES = 8
_SUBLANE_STEPS = (1, 2, 4)
HEAD = 64
PAIR = 2 * HEAD
SLAB = 256
RW_CHUNK = 64
S5_CHUNK = 8
VMEM_LIMIT = 56 * 1024 * 1024


def _cparams(sem):
    return pltpu.CompilerParams(dimension_semantics=sem, vmem_limit_bytes=VMEM_LIMIT)


def _layer_norm(z, g, b):
    mu = jnp.mean(z, axis=-1, keepdims=True)
    d = z - mu
    var = jnp.mean(d * d, axis=-1, keepdims=True)
    return d * lax.rsqrt(var + LN_EPS) * g + b


def _dot(a, b):
    return jnp.dot(a.astype(BF16), b.astype(BF16), preferred_element_type=F32)


def _dot_t(a, b):
    return lax.dot_general(a.astype(BF16), b.astype(BF16), (((1,), (1,)), ((), ())),
                           preferred_element_type=F32)


def _tdot(a, b):
    return lax.dot_general(a.astype(BF16), b.astype(BF16), (((0,), (0,)), ((), ())),
                           preferred_element_type=F32)


MLP_FF_CHUNK = 1024


ROW_SPLIT = 2


def _mixer_out_mlp_body(mix_fn, x_ref, g0_ref, b0_ref, w1_ref, w2_ref, g1_ref, b1_ref, o_ref):
    tm = x_ref.shape[0]
    d_ff = w1_ref.shape[1]
    chunk = min(MLP_FF_CHUNK, d_ff)
    groups = [slice(i * tm // ROW_SPLIT, (i + 1) * tm // ROW_SPLIT) for i in range(ROW_SPLIT)]
    mix = [mix_fn(rows) for rows in groups]
    h = [_layer_norm(DN_ALPHA * x_ref[rows, :] + m, g0_ref[...], b0_ref[...]) for rows, m in zip(groups, mix)]
    hb = [v.astype(BF16) for v in h]
    acc = [jnp.zeros(v.shape, F32) for v in h]
    for c in range(d_ff // chunk):
        up = [jnp.dot(v, w1_ref[:, c * chunk:(c + 1) * chunk], preferred_element_type=F32) for v in hb]
        act = [jnp.square(jnp.maximum(u, 0.0)).astype(BF16) for u in up]
        w2c = w2_ref[c * chunk:(c + 1) * chunk, :].astype(BF16)
        acc = [s + jnp.dot(a, w2c, preferred_element_type=F32) for s, a in zip(acc, act)]
    for rows, v, s in zip(groups, h, acc):
        o_ref[rows, :] = _layer_norm(DN_ALPHA * v + s, g1_ref[...], b1_ref[...])


def _gate_mlp_kernel(x_ref, y_ref, gate_ref, wo_ref, g0_ref, b0_ref, w1_ref, w2_ref, g1_ref, b1_ref, o_ref):
    def mix(rows):
        return _dot(y_ref[rows, :].astype(F32) * gate_ref[rows, :].astype(F32), wo_ref[...])

    _mixer_out_mlp_body(mix, x_ref, g0_ref, b0_ref, w1_ref, w2_ref, g1_ref, b1_ref, o_ref)


def _glu_mlp_kernel(x_ref, y_ref, wglu_ref, g0_ref, b0_ref, w1_ref, w2_ref, g1_ref, b1_ref, o_ref):
    c = x_ref.shape[1]

    def mix(rows):
        y = y_ref[rows, :].astype(F32)
        y = 0.5 * y * (1.0 + jnp.tanh(math.sqrt(2.0 / math.pi) * (y + 0.044715 * (y * y * y))))
        z = _dot(y, wglu_ref[...])
        return z[:, :c] * jax.nn.sigmoid(z[:, c:])

    _mixer_out_mlp_body(mix, x_ref, g0_ref, b0_ref, w1_ref, w2_ref, g1_ref, b1_ref, o_ref)


def _mixer_out_mlp(body, name, row_inputs, w_mix, ln0, mlp_w1, mlp_w2, layer, ln1, *, tm=512):
    n, c = row_inputs[0].shape
    tm = min(tm, n)
    tile = pl.BlockSpec((tm, c), lambda i: (i, 0))
    const = lambda i: (0, 0)
    vec = pl.BlockSpec((1, c), const)

    def stacked(w):
        return pl.BlockSpec((None,) + w.shape[1:], lambda i: (layer, 0, 0), pipeline_mode=pl.Buffered(1))

    return pl.pallas_call(
        body,
        grid=(n // tm,),
        in_specs=[tile] * len(row_inputs) + [pl.BlockSpec(w_mix.shape, const, pipeline_mode=pl.Buffered(1)),
                                             vec, vec, stacked(mlp_w1), stacked(mlp_w2), vec, vec],
        out_specs=tile,
        out_shape=jax.ShapeDtypeStruct((n, c), F32),
        compiler_params=_cparams(("parallel",)),
        name=name,
    )(*row_inputs, w_mix, ln0[0].reshape(1, c), ln0[1].reshape(1, c), mlp_w1, mlp_w2,
      ln1[0].reshape(1, c), ln1[1].reshape(1, c))


_V_MU, _V_W0, _V_A0, _V_KK, _V_KA = 0, 6, 7, 8, 9


def _head_sum(x, j_ref):
    parts = [_dot(x[:, s:s + SLAB], j_ref[...]) for s in range(0, x.shape[1], SLAB)]
    return jnp.concatenate(parts, axis=1)


def _rw_proj_kernel(x_ref, xp_ref, vec_ref, wr_ref, wk_ref, wv_ref, w1_ref, w2_ref, a1_ref, a2_ref,
                    g1_ref, g2_ref, j_ref, r_ref, k_ref, v_ref, lw_ref, an_ref, bn_ref, g_ref):
    x = x_ref[0]
    tm = x.shape[0]
    row = lax.broadcasted_iota(jnp.int32, (tm, 1), 0)
    prev_last = jnp.where(pl.program_id(1) == 0, 0.0, xp_ref[0][7:8, :])
    xprev = jnp.where(row == 0, prev_last, pltpu.roll(x, 1, axis=0))
    xx = xprev - x

    def mix(i):
        return x + xx * vec_ref[_V_MU + i:_V_MU + i + 1, :]

    r = _dot(mix(0), wr_ref[...])
    w_pre = vec_ref[_V_W0:_V_W0 + 1, :] + _dot(jnp.tanh(_dot(mix(1), w1_ref[...])), w2_ref[...])
    k = _dot(mix(2), wk_ref[...])
    v = _dot(mix(3), wv_ref[...])
    a = jax.nn.sigmoid(vec_ref[_V_A0:_V_A0 + 1, :] + _dot(_dot(mix(4), a1_ref[...]), a2_ref[...]))
    g = _dot(jax.nn.sigmoid(_dot(mix(5), g1_ref[...])), g2_ref[...])

    kk = k * vec_ref[_V_KK:_V_KK + 1, :]
    kk = kk / jnp.maximum(jnp.sqrt(_head_sum(kk * kk, j_ref)), 1e-12)
    r_ref[0] = r.astype(r_ref.dtype)
    k_ref[0] = (k * (1.0 + (a - 1.0) * vec_ref[_V_KA:_V_KA + 1, :])).astype(k_ref.dtype)
    v_ref[0] = v.astype(v_ref.dtype)
    lw_ref[0] = (-math.exp(-0.5)) * jax.nn.sigmoid(w_pre)
    an_ref[0] = (-kk).astype(an_ref.dtype)
    bn_ref[0] = (kk * a).astype(bn_ref.dtype)
    g_ref[0] = g.astype(g_ref.dtype)


def _rw_proj(x, vecs, wr, wk, wv, w1, w2, a1, a2, g1, g2, jmat, *, tm=512):
    bsz, seq, c = x.shape
    tm = min(tm, seq)
    const = lambda b, i: (0, 0)
    tile = pl.BlockSpec((1, tm, c), lambda b, i: (b, i, 0))
    prev = pl.BlockSpec((1, 8, c), lambda b, i: (b, jnp.maximum(i * (tm // 8) - 1, 0), 0))

    def full(w):
        return pl.BlockSpec(w.shape, const)

    ws = [w.astype(BF16) for w in (wr, wk, wv, w1, w2, a1, a2, g1, g2)]
    out = lambda dt: jax.ShapeDtypeStruct((bsz, seq, c), dt)
    return pl.pallas_call(
        _rw_proj_kernel,
        grid=(bsz, seq // tm),
        in_specs=[tile, prev, full(vecs)] + [full(w) for w in ws] + [full(jmat)],
        out_specs=[tile] * 7,
        out_shape=[out(BF16), out(BF16), out(BF16), out(F32), out(BF16), out(BF16), out(BF16)],
        compiler_params=_cparams(("parallel", "arbitrary")),
        name="rwkv_proj",
    )(x, x, vecs, *ws, jmat)


def _rw_scan_kernel(r_ref, k_ref, v_ref, lw_ref, an_ref, bn_ref, vec_ref, y_ref, state_ref):
    nb, tm, c = r_ref.shape
    n_pairs = c // PAIR
    L = RW_CHUNK

    @pl.when(pl.program_id(1) == 0)
    def _():
        state_ref[...] = jnp.zeros_like(state_ref)

    sub3 = lax.broadcasted_iota(jnp.int32, (1, SUBLANES, 1), 1)

    def chunk_cumsum(x):
        x3 = x.reshape(L // SUBLANES, SUBLANES, x.shape[1])
        for d in _SUBLANE_STEPS:
            x3 = x3 + jnp.where(sub3 >= d, pltpu.roll(x3, d, axis=1), 0.0)
        blocks, run = [], None
        for kb in range(L // SUBLANES):
            blk = x3[kb] if run is None else x3[kb] + run
            blocks.append(blk)
            run = blk[SUBLANES - 1:SUBLANES, :]
        return jnp.concatenate(blocks, axis=0)

    ri = lax.broadcasted_iota(jnp.int32, (PAIR, PAIR), 0)
    ci = lax.broadcasted_iota(jnp.int32, (PAIR, PAIR), 1)
    same_head = (ri // L) == (ci // L)
    mask_strict = same_head & ((ci % L) < (ri % L))
    mask_incl = same_head & ((ci % L) <= (ri % L))
    eye = (ri == ci).astype(F32)
    lane = lax.broadcasted_iota(jnp.int32, (L, PAIR), 1)
    head0 = lane < HEAD
    ri_h = lax.broadcasted_iota(jnp.int32, (PAIR // 2, PAIR), 0)
    ci_h = lax.broadcasted_iota(jnp.int32, (PAIR // 2, PAIR), 1)

    def level_mask(dd):
        return same_head & ((ri // (2 * dd)) == (ci // (2 * dd))) & ((ri % (2 * dd)) >= dd) & ((ci % (2 * dd)) < dd)

    def sel(res):
        return jnp.where(head0, res[:L], res[L:])

    def dup(x):
        return jnp.concatenate([x, x], axis=0)

    def head_sum(x):
        s0 = jnp.sum(jnp.where(head0, x, 0.0), axis=1, keepdims=True)
        s1 = jnp.sum(jnp.where(head0, 0.0, x), axis=1, keepdims=True)
        return jnp.where(head0, s0, s1)

    def take_rows(x, dd, half):
        return jnp.concatenate([x[(2 * b + half) * dd:(2 * b + half + 1) * dd]
                                for b in range(PAIR // (2 * dd))], axis=0)

    def merge_rows(first, second, dd):
        parts = []
        for b in range(PAIR // (2 * dd)):
            parts += [first[b * dd:(b + 1) * dd], second[b * dd:(b + 1) * dd]]
        return jnp.concatenate(parts, axis=0)

    n_chunks = tm // L
    per_chunk = nb * n_pairs
    probs = range(n_chunks * per_chunk)
    rows = [slice((p // per_chunk) * L, (p // per_chunk + 1) * L) for p in probs]
    seq = [(p % per_chunk) // n_pairs for p in probs]
    lanes = [slice((p % n_pairs) * PAIR, (p % n_pairs + 1) * PAIR) for p in probs]

    def tile(ref, p):
        return ref[seq[p], rows[p], lanes[p]]

    rhs, ar, bk, vv, e_l = [], [], [], [], []
    for p in probs:
        r = tile(r_ref, p).astype(F32)
        k = tile(k_ref, p).astype(F32)
        a = tile(an_ref, p).astype(F32)
        b = tile(bn_ref, p).astype(F32)
        lw = tile(lw_ref, p)
        cs_c = chunk_cumsum(lw)
        cs_last = cs_c[L - 1:L, :]
        a_t = a * jnp.exp(cs_c - lw)
        r_t = r * jnp.exp(cs_c)
        e_m = jnp.exp(-cs_c)
        b_t = b * e_m
        k_t = k * e_m
        e_d = jnp.exp(cs_last - cs_c)
        rhs.append(jnp.concatenate([jnp.where(head0, b_t, 0.0), jnp.where(head0, 0.0, b_t),
                                    jnp.where(head0, k_t, 0.0), jnp.where(head0, 0.0, k_t)], axis=0).astype(BF16))
        ar.append(jnp.concatenate([a_t, r_t], axis=0).astype(BF16))
        bk.append(jnp.concatenate([b * e_d, k * e_d], axis=0).astype(BF16))
        vv.append(dup(tile(v_ref, p)))
        e_l.append(jnp.exp(cs_last))

    sc = [_dot_t(ar[p], rhs[p]) for p in probs]
    n_ab = [jnp.where(mask_strict, dup(sc[p][:L, :PAIR]), 0.0) for p in probs]
    n_ak = [jnp.where(mask_strict, dup(sc[p][:L, PAIR:]), 0.0).astype(BF16) for p in probs]
    n_r = [jnp.concatenate([jnp.where(mask_incl, dup(sc[p][L:, :PAIR]), 0.0),
                            jnp.where(mask_incl, dup(sc[p][L:, PAIR:]), 0.0)], axis=1).astype(BF16) for p in probs]
    w0 = [sel(_dot(n_ak[p], vv[p])) for p in probs]

    t_inv = [jnp.where(level_mask(1), n_ab[p], eye) for p in probs]
    n_ab_b = [n_ab[p].astype(BF16) for p in probs]
    dd = 2
    while dd < SUBLANES:
        lm = level_mask(dd)
        nt = [jnp.where(lm, _dot(n_ab_b[p], t_inv[p]), 0.0) for p in probs]
        t_inv = [t_inv[p] + _dot(t_inv[p], nt[p]) for p in probs]
        dd *= 2
    zero_half = jnp.zeros((PAIR // 2, PAIR), F32)
    while dd < L:
        lm_hi = ((ri_h // dd) == (ci_h // (2 * dd))) & ((ci_h % (2 * dd)) < dd)
        nt = [merge_rows(zero_half, jnp.where(lm_hi, _dot(take_rows(n_ab[p], dd, 1), t_inv[p]), 0.0), dd)
              for p in probs]
        t_hi = [take_rows(t_inv[p], dd, 1) for p in probs]
        t_inv = [merge_rows(take_rows(t_inv[p], dd, 0), t_hi[p] + _dot(t_hi[p], nt[p]), dd) for p in probs]
        dd *= 2
    t_inv = [t_inv[p].astype(BF16) for p in probs]

    for ch in range(n_chunks):
        cur = range(ch * per_chunk, (ch + 1) * per_chunk)
        slot = {p: p - ch * per_chunk for p in cur}
        s_prev = {p: state_ref[slot[p]] for p in cur}
        ars = {p: _dot_t(ar[p], s_prev[p]) for p in cur}
        u = {p: sel(_dot(t_inv[p], dup(ars[p][:L] + w0[p]))) for p in cur}
        o = {p: ars[p][L:] + sel(_dot(n_r[p], jnp.concatenate([dup(u[p]).astype(BF16), vv[p]], axis=0))) for p in cur}
        for p in cur:
            upd = _tdot(jnp.concatenate([u[p].astype(BF16), vv[p][:L]], axis=0), bk[p])
            state_ref[slot[p]] = s_prev[p] * e_l[p] + jnp.where(same_head, upd, 0.0)
        for p in cur:
            dlt = o[p] - head_sum(o[p]) * (1.0 / HEAD)
            var = head_sum(dlt * dlt) * (1.0 / HEAD)
            rk = tile(r_ref, p).astype(F32) * tile(k_ref, p).astype(F32)
            bonus = head_sum(rk * vec_ref[0:1, lanes[p]]) * tile(v_ref, p).astype(F32)
            y = dlt * lax.rsqrt(var + GN_EPS) * vec_ref[1:2, lanes[p]] + vec_ref[2:3, lanes[p]] + bonus
            y_ref[seq[p], rows[p], lanes[p]] = y.astype(y_ref.dtype)


def _rw_scan(r, k, v, lw, an, bn, vecs, *, tm=128, nb=4):
    bsz, seq, c = r.shape
    tm = min(tm, seq)
    nb = math.gcd(nb, bsz)
    tile = pl.BlockSpec((nb, tm, c), lambda b, i: (b, i, 0))
    const = lambda b, i: (0, 0)
    return pl.pallas_call(
        _rw_scan_kernel,
        grid=(bsz // nb, seq // tm),
        in_specs=[tile] * 6 + [pl.BlockSpec(vecs.shape, const)],
        out_specs=tile,
        out_shape=jax.ShapeDtypeStruct((bsz, seq, c), BF16),
        scratch_shapes=[pltpu.VMEM((nb * (c // PAIR), PAIR, PAIR), F32)],
        compiler_params=_cparams(("parallel", "arbitrary")),
        name="rwkv_scan",
    )(r, k, v, lw, an, bn, vecs)


def _s5_tables(a_re, a_im, log_dt, b_re, b_im, c_re, c_im):
    g, p = a_re.shape
    s = b_re.shape[-1]
    L = S5_CHUNK
    dt = jnp.exp(log_dt.astype(F32))[:, None]
    lam_re = jnp.minimum(a_re.astype(F32), -1e-4)
    lam_im = a_im.astype(F32)

    def apow(n):
        n = jnp.asarray(n, F32)[:, None, None]
        mag = jnp.exp(n * dt * lam_re)
        return mag * jnp.cos(n * dt * lam_im), mag * jnp.sin(n * dt * lam_im)

    ab_re, ab_im = apow([1])
    ab_re, ab_im = ab_re[0], ab_im[0]
    den = lam_re * lam_re + lam_im * lam_im
    nr, ni = ab_re - 1.0, ab_im
    coef_re = (nr * lam_re + ni * lam_im) / den
    coef_im = (ni * lam_re - nr * lam_im) / den
    bb_re = coef_re[..., None] * b_re - coef_im[..., None] * b_im
    bb_im = coef_re[..., None] * b_im + coef_im[..., None] * b_re
    cr, ci = c_re.astype(F32), c_im.astype(F32)

    pw_re, pw_im = apow(jnp.arange(L))
    cb_re = (jnp.einsum('gcp,tgp,gpd->gtcd', cr, pw_re, bb_re) - jnp.einsum('gcp,tgp,gpd->gtcd', cr, pw_im, bb_im)
             - jnp.einsum('gcp,tgp,gpd->gtcd', ci, pw_re, bb_im) - jnp.einsum('gcp,tgp,gpd->gtcd', ci, pw_im, bb_re))

    gs = SLAB // s
    n_slab = g // gs
    m_lag = cb_re.reshape(n_slab, gs, L, s, s).transpose(0, 2, 4, 1, 3).reshape(n_slab, L, s, SLAB)

    gt = LANES // s
    n_t = g // gt
    st = 2 * gt * p
    rv_re, rv_im = pw_re[::-1], pw_im[::-1]
    in_re = rv_re[..., None] * bb_re[None] - rv_im[..., None] * bb_im[None]
    in_im = rv_re[..., None] * bb_im[None] + rv_im[..., None] * bb_re[None]
    m_in = jnp.stack([in_re, in_im], axis=0).reshape(2, L, n_t, gt, p, s)
    m_in = m_in.transpose(2, 1, 5, 0, 3, 4).reshape(n_t, L, s, st)
    p1_re, p1_im = apow(jnp.arange(1, L + 1))
    out_re = cr[None] * p1_re[:, :, None, :] - ci[None] * p1_im[:, :, None, :]
    out_im = cr[None] * p1_im[:, :, None, :] + ci[None] * p1_re[:, :, None, :]
    m_out = jnp.stack([out_re, -out_im], axis=0).reshape(2, L, n_t, gt, s, p)
    m_out = m_out.transpose(2, 0, 5, 1, 3, 4).reshape(n_t, 2, p, L * LANES)

    def lay_vec(re, im):
        n = re.shape[0]
        f = lambda m: m.reshape(n, n_t, gt * p).transpose(1, 0, 2)
        return jnp.concatenate([f(re), f(im)], axis=-1)

    step_pow = lay_vec(*apow([L * d for d in _SUBLANE_STEPS]))
    carry_pow = lay_vec(*apow(L * jnp.arange(1, SUBLANES + 1)))
    return m_lag, m_in, m_out, step_pow, carry_pow


def _cmul(ar, ai, br, bi):
    return ar * br - ai * bi, ar * bi + ai * br


def _s5_kernel(u_ref, mlag_ref, min_ref, mout_ref, spow_ref, cpow_ref, dskip_ref, y_ref,
               carry_ref, ut_ref, zt_ref, wlag_ref, bcat_ref, ccat_ref):
    tm = u_ref.shape[1]
    L = S5_CHUNK
    assert L == SUBLANES
    n_rows = tm // L
    half = bcat_ref.shape[2] // 2
    tiles = range(SLAB // LANES)

    @pl.when(pl.program_id(2) == 0)
    def _():
        carry_ref[...] = jnp.zeros_like(carry_ref)

    @pl.when((pl.program_id(1) == 0) & (pl.program_id(2) == 0))
    def _():
        def expand(dst, table, n_groups, col_group_size):
            rows, cols = table.shape
            col_group = (lax.broadcasted_iota(jnp.int32, (1, cols), 1) // col_group_size) % n_groups
            for grp in range(n_groups):
                dst[pl.ds(grp * rows, rows), :] = jnp.where(col_group == grp, table, 0.0).astype(dst.dtype)

        s_ch, p = min_ref.shape[2], mout_ref.shape[2]
        for tau in range(L):
            n_grp = SLAB // s_ch
            expand(wlag_ref.at[pl.ds(tau * SLAB, SLAB), :], mlag_ref[0, tau], n_grp, s_ch)
        for t in tiles:
            n_grp = LANES // s_ch
            for s in range(L):
                expand(bcat_ref.at[t, pl.ds(s * LANES, LANES), :], min_ref[t, s], n_grp, p)
            for ri in range(2):
                expand(ccat_ref.at[t, pl.ds(ri * n_grp * p, n_grp * p), :], mout_ref[t, ri], n_grp, s_ch)

    u = u_ref[0]
    sub3 = lax.broadcasted_iota(jnp.int32, (1, SUBLANES, 1), 1)
    sub2 = lax.broadcasted_iota(jnp.int32, (SUBLANES, 1), 0)
    n_blk = n_rows // SUBLANES

    x = []
    for t in tiles:
        ut_ref[t] = u[:, t * LANES:(t + 1) * LANES]
        ucat = jnp.concatenate([ut_ref[t, pl.ds(s, n_rows, stride=L), :] for s in range(L)], axis=1)
        x.append(_dot(ucat, bcat_ref[t]))

    u3 = u.reshape(tm // SUBLANES, SUBLANES, SLAB)
    shifted = [u.astype(BF16)]
    for tau in range(1, L):
        sh = jnp.where(sub3 >= tau, pltpu.roll(u3, tau, axis=1), 0.0)
        shifted.append(sh.reshape(tm, SLAB).astype(BF16))
    y_lag = jnp.dot(jnp.concatenate(shifted, axis=1), wlag_ref[...], preferred_element_type=F32)

    for t in tiles:
        hr = x[t][:, :half].reshape(n_blk, SUBLANES, half)
        hi = x[t][:, half:].reshape(n_blk, SUBLANES, half)
        for i, d in enumerate(_SUBLANE_STEPS):
            pr, pi = spow_ref[t, i:i + 1, :half], spow_ref[t, i:i + 1, half:]
            sr, si = _cmul(pr, pi, pltpu.roll(hr, d, axis=1), pltpu.roll(hi, d, axis=1))
            hr = hr + jnp.where(sub3 >= d, sr, 0.0)
            hi = hi + jnp.where(sub3 >= d, si, 0.0)
        cr, ci = carry_ref[t, :, :half], carry_ref[t, :, half:]
        cp_r, cp_i = cpow_ref[t, :, :half], cpow_ref[t, :, half:]
        prev_r, prev_i = [], []
        for kb in range(n_blk):
            kr, ki = _cmul(cp_r, cp_i, cr, ci)
            br, bi = hr[kb] + kr, hi[kb] + ki
            prev_r.append(jnp.where(sub2 == 0, cr, pltpu.roll(br, 1, axis=0)))
            prev_i.append(jnp.where(sub2 == 0, ci, pltpu.roll(bi, 1, axis=0)))
            cr, ci = br[SUBLANES - 1:SUBLANES, :], bi[SUBLANES - 1:SUBLANES, :]
        carry_ref[t, :, :half] = cr
        carry_ref[t, :, half:] = ci
        prev = jnp.concatenate([jnp.concatenate(prev_r, axis=0), jnp.concatenate(prev_i, axis=0)], axis=1)
        z = _dot(prev, ccat_ref[t])
        for s in range(L):
            zt_ref[t, pl.ds(s, n_rows, stride=L), :] = z[:, s * LANES:(s + 1) * LANES]

    y_state = jnp.concatenate([zt_ref[t] for t in tiles], axis=1)
    y_ref[0] = (y_lag + y_state + u * dskip_ref[...]).astype(y_ref.dtype)


def _s5_conv(h, tables, d_skip, *, tm=1024):
    bsz, seq, c = h.shape
    tm = min(tm, seq)
    m_lag, m_in, m_out, step_pow, carry_pow = tables
    n_slab = c // SLAB
    tps = SLAB // LANES
    st = m_in.shape[-1]
    chunk_cols = m_out.shape[-1]
    return pl.pallas_call(
        _s5_kernel,
        grid=(n_slab, bsz, seq // tm),
        in_specs=[
            pl.BlockSpec((1, tm, SLAB), lambda s, b, i: (b, i, s)),
            pl.BlockSpec((1,) + m_lag.shape[1:], lambda s, b, i: (s, 0, 0, 0)),
            pl.BlockSpec((tps,) + m_in.shape[1:], lambda s, b, i: (s, 0, 0, 0)),
            pl.BlockSpec((tps,) + m_out.shape[1:], lambda s, b, i: (s, 0, 0, 0)),
            pl.BlockSpec((tps,) + step_pow.shape[1:], lambda s, b, i: (s, 0, 0)),
            pl.BlockSpec((tps,) + carry_pow.shape[1:], lambda s, b, i: (s, 0, 0)),
            pl.BlockSpec((1, SLAB), lambda s, b, i: (0, s)),
        ],
        out_specs=pl.BlockSpec((1, tm, SLAB), lambda s, b, i: (b, i, s)),
        out_shape=jax.ShapeDtypeStruct((bsz, seq, c), BF16),
        scratch_shapes=[pltpu.VMEM((tps, 1, st), F32), pltpu.VMEM((tps, tm, LANES), F32),
                        pltpu.VMEM((tps, tm, LANES), F32),
                        pltpu.VMEM((S5_CHUNK * SLAB, SLAB), BF16),
                        pltpu.VMEM((tps, chunk_cols, st), BF16),
                        pltpu.VMEM((tps, st, chunk_cols), BF16)],
        compiler_params=_cparams(("arbitrary", "arbitrary", "arbitrary")),
        name="s5_conv",
    )(h, m_lag, m_in, m_out, step_pow, carry_pow, d_skip.reshape(1, c))


def _rwkv_block(h, ln0, ln1, mu, w0, w1, w2, a0, a1, a2, g1, g2, k_k, k_a, r_k, wr, wk, wv, wo,
                lnx_g, lnx_b, mlp_w1, mlp_w2, layer):
    bsz, seq, c = h.shape
    zeros = jnp.zeros((6, c), F32)
    vecs = jnp.concatenate([mu, w0[None], a0[None], k_k[None], k_a[None], zeros], axis=0)
    scan_vecs = jnp.concatenate([r_k.reshape(1, c), lnx_g[None], lnx_b[None], zeros[:5]], axis=0)
    idx = jnp.arange(SLAB) // HEAD
    jmat = (idx[:, None] == idx[None, :]).astype(BF16)
    r, k, v, lw, an, bn, gate = _rw_proj(h, vecs, wr, wk, wv, w1, w2, a1, a2, g1, g2, jmat)
    y = _rw_scan(r, k, v, lw, an, bn, scan_vecs)
    n = bsz * seq
    out = _mixer_out_mlp(_gate_mlp_kernel, "rwkv_out_mlp", [h.reshape(n, c), y.reshape(n, c), gate.reshape(n, c)],
                         wo, ln0, mlp_w1, mlp_w2, layer, ln1)
    return out.reshape(bsz, seq, c)


def _s5_block(h, ln0, ln1, a_re, a_im, log_dt, b_re, b_im, c_re, c_im, d_skip, w_glu, mlp_w1, mlp_w2, layer):
    bsz, seq, c = h.shape
    tables = _s5_tables(a_re, a_im, log_dt, b_re, b_im, c_re, c_im)
    y = _s5_conv(h, tables, d_skip)
    n = bsz * seq
    out = _mixer_out_mlp(_glu_mlp_kernel, "s5_out_mlp", [h.reshape(n, c), y.reshape(n, c)],
                         w_glu, ln0, mlp_w1, mlp_w2, layer, ln1)
    return out.reshape(bsz, seq, c)


def kernel(x, ln_g, ln_b, rw_mu, rw_w0, rw_w1, rw_w2, rw_a0, rw_a1, rw_a2, rw_g1, rw_g2, rw_k_k, rw_k_a, rw_r_k, rw_wr, rw_wk, rw_wv, rw_wo, rw_lnx_g, rw_lnx_b, s5_a_re, s5_a_im, s5_log_dt, s5_b_re, s5_b_im, s5_c_re, s5_c_im, s5_d, s5_w_glu, mlp_w1, mlp_w2):
    w1b, w2b = mlp_w1.astype(BF16), mlp_w2
    h = _rwkv_block(x, (ln_g[0], ln_b[0]), (ln_g[1], ln_b[1]), rw_mu[0], rw_w0[0], rw_w1[0], rw_w2[0],
                    rw_a0[0], rw_a1[0], rw_a2[0], rw_g1[0], rw_g2[0], rw_k_k[0], rw_k_a[0], rw_r_k[0],
                    rw_wr[0], rw_wk[0], rw_wv[0], rw_wo[0], rw_lnx_g[0], rw_lnx_b[0], w1b, w2b, 0)
    h = _s5_block(h, (ln_g[2], ln_b[2]), (ln_g[3], ln_b[3]), s5_a_re[0], s5_a_im[0], s5_log_dt[0],
                  s5_b_re[0], s5_b_im[0], s5_c_re[0], s5_c_im[0], s5_d[0], s5_w_glu[0], w1b, w2b, 1)
    return h
```

```python
import functools
import math

import jax
import jax.numpy as jnp
from jax import lax
from jax.experimental import pallas as pl
from jax.experimental.pallas import tpu as pltpu

F32 = jnp.float32
BF16 = jnp.bfloat16

DEPTH = 2
DN_ALPHA = (2.0 * DEPTH) ** 0.25
LN_EPS = 1e-5
GN_EPS = 64e-5

LANES = 128
SUBLANES = 8
_SUBLANE_STEPS = (1, 2, 4)
HEAD = 64
PAIR = 2 * HEAD
SLAB = 256
RW_CHUNK = 64
S5_CHUNK = 8
V7X_VMEM_BYTES = 64 * 1024 * 1024
VMEM_LIMIT = V7X_VMEM_BYTES * 7 // 8


def _cparams(sem):
    return pltpu.CompilerParams(dimension_semantics=sem, vmem_limit_bytes=VMEM_LIMIT)


def _layer_norm(z, g, b):
    mu = jnp.mean(z, axis=-1, keepdims=True)
    d = z - mu
    var = jnp.mean(d * d, axis=-1, keepdims=True)
    return d * lax.rsqrt(var + LN_EPS) * g + b


def _dot(a, b):
    return jnp.dot(a.astype(BF16), b.astype(BF16), preferred_element_type=F32)


def _dot_t(a, b):
    return lax.dot_general(a.astype(BF16), b.astype(BF16), (((1,), (1,)), ((), ())),
                           preferred_element_type=F32)


def _tdot(a, b):
    return lax.dot_general(a.astype(BF16), b.astype(BF16), (((0,), (0,)), ((), ())),
                           preferred_element_type=F32)


MLP_FF_CHUNK = 1024


ROW_SPLIT = 2


def _mixer_out_mlp_body(mix_fn, x_ref, g0_ref, b0_ref, w1_ref, w2_ref, g1_ref, b1_ref, o_ref):
    tm = x_ref.shape[0]
    d_ff = w1_ref.shape[1]
    chunk = min(MLP_FF_CHUNK, d_ff)
    groups = [slice(i * tm // ROW_SPLIT, (i + 1) * tm // ROW_SPLIT) for i in range(ROW_SPLIT)]
    mix = [mix_fn(rows) for rows in groups]
    h = [_layer_norm(DN_ALPHA * x_ref[rows, :] + m, g0_ref[...], b0_ref[...]) for rows, m in zip(groups, mix)]
    hb = [v.astype(BF16) for v in h]
    acc = [jnp.zeros(v.shape, F32) for v in h]
    for c in range(d_ff // chunk):
        up = [jnp.dot(v, w1_ref[:, c * chunk:(c + 1) * chunk], preferred_element_type=F32) for v in hb]
        act = [jnp.square(jnp.maximum(u, 0.0)).astype(BF16) for u in up]
        w2c = w2_ref[c * chunk:(c + 1) * chunk, :].astype(BF16)
        acc = [s + jnp.dot(a, w2c, preferred_element_type=F32) for s, a in zip(acc, act)]
    for rows, v, s in zip(groups, h, acc):
        o_ref[rows, :] = _layer_norm(DN_ALPHA * v + s, g1_ref[...], b1_ref[...])


def _gate_mlp_kernel(x_ref, y_ref, gate_ref, wo_ref, g0_ref, b0_ref, w1_ref, w2_ref, g1_ref, b1_ref, o_ref):
    def mix(rows):
        return _dot(y_ref[rows, :].astype(F32) * gate_ref[rows, :].astype(F32), wo_ref[...])

    _mixer_out_mlp_body(mix, x_ref, g0_ref, b0_ref, w1_ref, w2_ref, g1_ref, b1_ref, o_ref)


def _glu_mlp_kernel(x_ref, y_ref, wglu_ref, g0_ref, b0_ref, w1_ref, w2_ref, g1_ref, b1_ref, o_ref):
    c = x_ref.shape[1]

    def mix(rows):
        y = y_ref[rows, :].astype(F32)
        y = 0.5 * y * (1.0 + jnp.tanh(math.sqrt(2.0 / math.pi) * (y + 0.044715 * (y * y * y))))
        z = _dot(y, wglu_ref[...])
        return z[:, :c] * jax.nn.sigmoid(z[:, c:])

    _mixer_out_mlp_body(mix, x_ref, g0_ref, b0_ref, w1_ref, w2_ref, g1_ref, b1_ref, o_ref)


def _mixer_out_mlp(body, name, row_inputs, w_mix, ln0, mlp_w1, mlp_w2, layer, ln1, *, tm=512):
    n, c = row_inputs[0].shape
    tm = min(tm, n)
    tile = pl.BlockSpec((tm, c), lambda i: (i, 0))
    const = lambda i: (0, 0)
    vec = pl.BlockSpec((1, c), const)

    def stacked(w):
        return pl.BlockSpec((None,) + w.shape[1:], lambda i: (layer, 0, 0), pipeline_mode=pl.Buffered(1))

    return pl.pallas_call(
        body,
        grid=(n // tm,),
        in_specs=[tile] * len(row_inputs) + [pl.BlockSpec(w_mix.shape, const, pipeline_mode=pl.Buffered(1)),
                                             vec, vec, stacked(mlp_w1), stacked(mlp_w2), vec, vec],
        out_specs=tile,
        out_shape=jax.ShapeDtypeStruct((n, c), F32),
        compiler_params=_cparams(("parallel",)),
        name=name,
    )(*row_inputs, w_mix, ln0[0].reshape(1, c), ln0[1].reshape(1, c), mlp_w1, mlp_w2,
      ln1[0].reshape(1, c), ln1[1].reshape(1, c))


_V_MU, _V_W0, _V_A0, _V_KK, _V_KA = 0, 6, 7, 8, 9


def _head_sum(x, j_ref):
    parts = [_dot(x[:, s:s + SLAB], j_ref[...]) for s in range(0, x.shape[1], SLAB)]
    return jnp.concatenate(parts, axis=1)


def _rw_proj_kernel(x_ref, xp_ref, vec_ref, wr_ref, wk_ref, wv_ref, w1_ref, w2_ref, a1_ref, a2_ref,
                    g1_ref, g2_ref, j_ref, r_ref, k_ref, v_ref, lw_ref, an_ref, bn_ref, g_ref):
    x = x_ref[0]
    tm = x.shape[0]
    row = lax.broadcasted_iota(jnp.int32, (tm, 1), 0)
    prev_last = jnp.where(pl.program_id(1) == 0, 0.0, xp_ref[0][SUBLANES - 1:SUBLANES, :])
    xprev = jnp.where(row == 0, prev_last, pltpu.roll(x, 1, axis=0))
    xx = xprev - x

    def mix(i):
        return x + xx * vec_ref[_V_MU + i:_V_MU + i + 1, :]

    r = _dot(mix(0), wr_ref[...])
    w_pre = vec_ref[_V_W0:_V_W0 + 1, :] + _dot(jnp.tanh(_dot(mix(1), w1_ref[...])), w2_ref[...])
    k = _dot(mix(2), wk_ref[...])
    v = _dot(mix(3), wv_ref[...])
    a = jax.nn.sigmoid(vec_ref[_V_A0:_V_A0 + 1, :] + _dot(_dot(mix(4), a1_ref[...]), a2_ref[...]))
    g = _dot(jax.nn.sigmoid(_dot(mix(5), g1_ref[...])), g2_ref[...])

    kk = k * vec_ref[_V_KK:_V_KK + 1, :]
    kk = kk / jnp.maximum(jnp.sqrt(_head_sum(kk * kk, j_ref)), 1e-12)
    r_ref[0] = r.astype(r_ref.dtype)
    k_ref[0] = (k * (1.0 + (a - 1.0) * vec_ref[_V_KA:_V_KA + 1, :])).astype(k_ref.dtype)
    v_ref[0] = v.astype(v_ref.dtype)
    lw_ref[0] = (-math.exp(-0.5)) * jax.nn.sigmoid(w_pre)
    an_ref[0] = (-kk).astype(an_ref.dtype)
    bn_ref[0] = (kk * a).astype(bn_ref.dtype)
    g_ref[0] = g.astype(g_ref.dtype)


def _rw_proj(x, vecs, wr, wk, wv, w1, w2, a1, a2, g1, g2, jmat, *, tm=512):
    bsz, seq, c = x.shape
    tm = min(tm, seq)
    const = lambda b, i: (0, 0)
    tile = pl.BlockSpec((1, tm, c), lambda b, i: (b, i, 0))
    prev = pl.BlockSpec((1, SUBLANES, c), lambda b, i: (b, jnp.maximum(i * (tm // SUBLANES) - 1, 0), 0))

    def full(w):
        return pl.BlockSpec(w.shape, const)

    ws = [w.astype(BF16) for w in (wr, wk, wv, w1, w2, a1, a2, g1, g2)]
    out = lambda dt: jax.ShapeDtypeStruct((bsz, seq, c), dt)
    return pl.pallas_call(
        _rw_proj_kernel,
        grid=(bsz, seq // tm),
        in_specs=[tile, prev, full(vecs)] + [full(w) for w in ws] + [full(jmat)],
        out_specs=[tile] * 7,
        out_shape=[out(BF16), out(BF16), out(BF16), out(F32), out(BF16), out(BF16), out(BF16)],
        compiler_params=_cparams(("parallel", "arbitrary")),
        name="rwkv_proj",
    )(x, x, vecs, *ws, jmat)


def _rw_scan_kernel(r_ref, k_ref, v_ref, lw_ref, an_ref, bn_ref, vec_ref, y_ref, state_ref):
    nb, tm, c = r_ref.shape
    n_pairs = c // PAIR
    L = RW_CHUNK

    @pl.when(pl.program_id(1) == 0)
    def _():
        state_ref[...] = jnp.zeros_like(state_ref)

    sub3 = lax.broadcasted_iota(jnp.int32, (1, SUBLANES, 1), 1)

    def chunk_cumsum(x):
        x3 = x.reshape(L // SUBLANES, SUBLANES, x.shape[1])
        for d in _SUBLANE_STEPS:
            x3 = x3 + jnp.where(sub3 >= d, pltpu.roll(x3, d, axis=1), 0.0)
        blocks, run = [], None
        for kb in range(L // SUBLANES):
            blk = x3[kb] if run is None else x3[kb] + run
            blocks.append(blk)
            run = blk[SUBLANES - 1:SUBLANES, :]
        return jnp.concatenate(blocks, axis=0)

    ri = lax.broadcasted_iota(jnp.int32, (PAIR, PAIR), 0)
    ci = lax.broadcasted_iota(jnp.int32, (PAIR, PAIR), 1)
    same_head = (ri // L) == (ci // L)
    mask_strict = same_head & ((ci % L) < (ri % L))
    mask_incl = same_head & ((ci % L) <= (ri % L))
    eye = (ri == ci).astype(F32)
    lane = lax.broadcasted_iota(jnp.int32, (L, PAIR), 1)
    head0 = lane < HEAD
    ri_h = lax.broadcasted_iota(jnp.int32, (PAIR // 2, PAIR), 0)
    ci_h = lax.broadcasted_iota(jnp.int32, (PAIR // 2, PAIR), 1)

    def level_mask(dd):
        return same_head & ((ri // (2 * dd)) == (ci // (2 * dd))) & ((ri % (2 * dd)) >= dd) & ((ci % (2 * dd)) < dd)

    def sel(res):
        return jnp.where(head0, res[:L], res[L:])

    def dup(x):
        return jnp.concatenate([x, x], axis=0)

    def head_sum(x):
        s0 = jnp.sum(jnp.where(head0, x, 0.0), axis=1, keepdims=True)
        s1 = jnp.sum(jnp.where(head0, 0.0, x), axis=1, keepdims=True)
        return jnp.where(head0, s0, s1)

    def take_rows(x, dd, half):
        return jnp.concatenate([x[(2 * b + half) * dd:(2 * b + half + 1) * dd]
                                for b in range(PAIR // (2 * dd))], axis=0)

    def merge_rows(first, second, dd):
        parts = []
        for b in range(PAIR // (2 * dd)):
            parts += [first[b * dd:(b + 1) * dd], second[b * dd:(b + 1) * dd]]
        return jnp.concatenate(parts, axis=0)

    n_chunks = tm // L
    per_chunk = nb * n_pairs
    probs = range(n_chunks * per_chunk)
    rows = [slice((p // per_chunk) * L, (p // per_chunk + 1) * L) for p in probs]
    seq = [(p % per_chunk) // n_pairs for p in probs]
    lanes = [slice((p % n_pairs) * PAIR, (p % n_pairs + 1) * PAIR) for p in probs]

    def tile(ref, p):
        return ref[seq[p], rows[p], lanes[p]]

    rhs, ar, bk, vv, e_l = [], [], [], [], []
    for p in probs:
        r = tile(r_ref, p).astype(F32)
        k = tile(k_ref, p).astype(F32)
        a = tile(an_ref, p).astype(F32)
        b = tile(bn_ref, p).astype(F32)
        lw = tile(lw_ref, p)
        cs_c = chunk_cumsum(lw)
        cs_last = cs_c[L - 1:L, :]
        a_t = a * jnp.exp(cs_c - lw)
        r_t = r * jnp.exp(cs_c)
        e_m = jnp.exp(-cs_c)
        b_t = b * e_m
        k_t = k * e_m
        rhs.append(jnp.concatenate([jnp.where(head0, b_t, 0.0), jnp.where(head0, 0.0, b_t),
                                    jnp.where(head0, k_t, 0.0), jnp.where(head0, 0.0, k_t)], axis=0).astype(BF16))
        ar.append(jnp.concatenate([a_t, r_t], axis=0).astype(BF16))
        bk.append(jnp.concatenate([b_t, k_t], axis=0).astype(BF16))
        vv.append(dup(tile(v_ref, p)))
        e_l.append(jnp.exp(cs_last))

    sc = [_dot_t(ar[p], rhs[p]) for p in probs]
    n_ab = [jnp.where(mask_strict, dup(sc[p][:L, :PAIR]), 0.0) for p in probs]
    n_ak = [jnp.where(mask_strict, dup(sc[p][:L, PAIR:]), 0.0).astype(BF16) for p in probs]
    n_r = [jnp.concatenate([jnp.where(mask_incl, dup(sc[p][L:, :PAIR]), 0.0),
                            jnp.where(mask_incl, dup(sc[p][L:, PAIR:]), 0.0)], axis=1).astype(BF16) for p in probs]
    w0 = [sel(_dot(n_ak[p], vv[p])) for p in probs]

    t_inv = [jnp.where(level_mask(1), n_ab[p], eye) for p in probs]
    n_ab_b = [n_ab[p].astype(BF16) for p in probs]
    dd = 2
    while dd < SUBLANES:
        lm = level_mask(dd)
        nt = [jnp.where(lm, _dot(n_ab_b[p], t_inv[p]), 0.0) for p in probs]
        t_inv = [t_inv[p] + _dot(t_inv[p], nt[p]) for p in probs]
        dd *= 2
    zero_half = jnp.zeros((PAIR // 2, PAIR), F32)
    while dd < L:
        lm_hi = ((ri_h // dd) == (ci_h // (2 * dd))) & ((ci_h % (2 * dd)) < dd)
        nt = [merge_rows(zero_half, jnp.where(lm_hi, _dot(take_rows(n_ab[p], dd, 1), t_inv[p]), 0.0), dd)
              for p in probs]
        t_hi = [take_rows(t_inv[p], dd, 1) for p in probs]
        t_inv = [merge_rows(take_rows(t_inv[p], dd, 0), t_hi[p] + _dot(t_hi[p], nt[p]), dd) for p in probs]
        dd *= 2
    t_inv = [t_inv[p].astype(BF16) for p in probs]

    for ch in range(n_chunks):
        cur = range(ch * per_chunk, (ch + 1) * per_chunk)
        slot = {p: p - ch * per_chunk for p in cur}
        s_prev = {p: state_ref[slot[p]] for p in cur}
        ars = {p: _dot_t(ar[p], s_prev[p]) for p in cur}
        u = {p: sel(_dot(t_inv[p], dup(ars[p][:L] + w0[p]))) for p in cur}
        o = {p: ars[p][L:] + sel(_dot(n_r[p], jnp.concatenate([dup(u[p]).astype(BF16), vv[p]], axis=0))) for p in cur}
        for p in cur:
            upd = _tdot(jnp.concatenate([u[p].astype(BF16), vv[p][:L]], axis=0), bk[p])
            state_ref[slot[p]] = (s_prev[p] + jnp.where(same_head, upd, 0.0)) * e_l[p]
        for p in cur:
            dlt = o[p] - head_sum(o[p]) * (1.0 / HEAD)
            var = head_sum(dlt * dlt) * (1.0 / HEAD)
            rk = tile(r_ref, p).astype(F32) * tile(k_ref, p).astype(F32)
            bonus = head_sum(rk * vec_ref[0:1, lanes[p]]) * tile(v_ref, p).astype(F32)
            y = dlt * lax.rsqrt(var + GN_EPS) * vec_ref[1:2, lanes[p]] + vec_ref[2:3, lanes[p]] + bonus
            y_ref[seq[p], rows[p], lanes[p]] = y.astype(y_ref.dtype)


def _rw_scan(r, k, v, lw, an, bn, vecs, *, tm=128, nb=4):
    bsz, seq, c = r.shape
    tm = min(tm, seq)
    nb = math.gcd(nb, bsz)
    tile = pl.BlockSpec((nb, tm, c), lambda b, i: (b, i, 0))
    const = lambda b, i: (0, 0)
    return pl.pallas_call(
        _rw_scan_kernel,
        grid=(bsz // nb, seq // tm),
        in_specs=[tile] * 6 + [pl.BlockSpec(vecs.shape, const)],
        out_specs=tile,
        out_shape=jax.ShapeDtypeStruct((bsz, seq, c), BF16),
        scratch_shapes=[pltpu.VMEM((nb * (c // PAIR), PAIR, PAIR), F32)],
        compiler_params=_cparams(("parallel", "arbitrary")),
        name="rwkv_scan",
    )(r, k, v, lw, an, bn, vecs)


def _s5_tables(a_re, a_im, log_dt, b_re, b_im, c_re, c_im):
    g, p = a_re.shape
    s = b_re.shape[-1]
    L = S5_CHUNK
    dt = jnp.exp(log_dt.astype(F32))[:, None]
    lam_re = jnp.minimum(a_re.astype(F32), -1e-4)
    lam_im = a_im.astype(F32)

    def apow(n):
        n = jnp.asarray(n, F32)[:, None, None]
        mag = jnp.exp(n * dt * lam_re)
        return mag * jnp.cos(n * dt * lam_im), mag * jnp.sin(n * dt * lam_im)

    ab_re, ab_im = apow([1])
    ab_re, ab_im = ab_re[0], ab_im[0]
    den = lam_re * lam_re + lam_im * lam_im
    nr, ni = ab_re - 1.0, ab_im
    coef_re = (nr * lam_re + ni * lam_im) / den
    coef_im = (ni * lam_re - nr * lam_im) / den
    bb_re = coef_re[..., None] * b_re - coef_im[..., None] * b_im
    bb_im = coef_re[..., None] * b_im + coef_im[..., None] * b_re
    cr, ci = c_re.astype(F32), c_im.astype(F32)

    pw_re, pw_im = apow(jnp.arange(L))
    cb_re = (jnp.einsum('gcp,tgp,gpd->gtcd', cr, pw_re, bb_re) - jnp.einsum('gcp,tgp,gpd->gtcd', cr, pw_im, bb_im)
             - jnp.einsum('gcp,tgp,gpd->gtcd', ci, pw_re, bb_im) - jnp.einsum('gcp,tgp,gpd->gtcd', ci, pw_im, bb_re))

    gs = SLAB // s
    n_slab = g // gs
    m_lag = cb_re.reshape(n_slab, gs, L, s, s).transpose(0, 2, 4, 1, 3).reshape(n_slab, L, s, SLAB)

    gt = LANES // s
    n_t = g // gt
    st = 2 * gt * p
    rv_re, rv_im = pw_re[::-1], pw_im[::-1]
    in_re = rv_re[..., None] * bb_re[None] - rv_im[..., None] * bb_im[None]
    in_im = rv_re[..., None] * bb_im[None] + rv_im[..., None] * bb_re[None]
    m_in = jnp.stack([in_re, in_im], axis=0).reshape(2, L, n_t, gt, p, s)
    m_in = m_in.transpose(2, 1, 5, 0, 3, 4).reshape(n_t, L, s, st)
    p1_re, p1_im = apow(jnp.arange(1, L + 1))
    out_re = cr[None] * p1_re[:, :, None, :] - ci[None] * p1_im[:, :, None, :]
    out_im = cr[None] * p1_im[:, :, None, :] + ci[None] * p1_re[:, :, None, :]
    m_out = jnp.stack([out_re, -out_im], axis=0).reshape(2, L, n_t, gt, s, p)
    m_out = m_out.transpose(2, 0, 5, 1, 3, 4).reshape(n_t, 2, p, L * LANES)

    def lay_vec(re, im):
        n = re.shape[0]
        f = lambda m: m.reshape(n, n_t, gt * p).transpose(1, 0, 2)
        return jnp.concatenate([f(re), f(im)], axis=-1)

    step_pow = lay_vec(*apow([L * d for d in _SUBLANE_STEPS]))
    carry_pow = lay_vec(*apow(L * jnp.arange(1, SUBLANES + 1)))
    return m_lag, m_in, m_out, step_pow, carry_pow


def _cmul(ar, ai, br, bi):
    return ar * br - ai * bi, ar * bi + ai * br


def _s5_kernel(u_ref, mlag_ref, min_ref, mout_ref, spow_ref, cpow_ref, dskip_ref, y_ref,
               carry_ref, ut_ref, zt_ref, wlag_ref, bcat_ref, ccat_ref):
    tm = u_ref.shape[1]
    L = S5_CHUNK
    assert L == SUBLANES
    n_rows = tm // L
    half = bcat_ref.shape[2] // 2
    tiles = range(SLAB // LANES)

    @pl.when(pl.program_id(2) == 0)
    def _():
        carry_ref[...] = jnp.zeros_like(carry_ref)

    @pl.when((pl.program_id(1) == 0) & (pl.program_id(2) == 0))
    def _():
        def expand(dst, table, n_groups, col_group_size):
            rows, cols = table.shape
            col_group = (lax.broadcasted_iota(jnp.int32, (1, cols), 1) // col_group_size) % n_groups
            for grp in range(n_groups):
                dst[pl.ds(grp * rows, rows), :] = jnp.where(col_group == grp, table, 0.0).astype(dst.dtype)

        s_ch, p = min_ref.shape[2], mout_ref.shape[2]
        for tau in range(L):
            n_grp = SLAB // s_ch
            expand(wlag_ref.at[pl.ds(tau * SLAB, SLAB), :], mlag_ref[0, tau], n_grp, s_ch)
        for t in tiles:
            n_grp = LANES // s_ch
            for s in range(L):
                expand(bcat_ref.at[t, pl.ds(s * LANES, LANES), :], min_ref[t, s], n_grp, p)
            for ri in range(2):
                expand(ccat_ref.at[t, pl.ds(ri * n_grp * p, n_grp * p), :], mout_ref[t, ri], n_grp, s_ch)

    u = u_ref[0]
    sub3 = lax.broadcasted_iota(jnp.int32, (1, SUBLANES, 1), 1)
    sub2 = lax.broadcasted_iota(jnp.int32, (SUBLANES, 1), 0)
    n_blk = n_rows // SUBLANES

    x = []
    for t in tiles:
        ut_ref[t] = u[:, t * LANES:(t + 1) * LANES]
        ucat = jnp.concatenate([ut_ref[t, pl.ds(s, n_rows, stride=L), :] for s in range(L)], axis=1)
        x.append(_dot(ucat, bcat_ref[t]))

    u3 = u.reshape(tm // SUBLANES, SUBLANES, SLAB)
    shifted = [u.astype(BF16)]
    for tau in range(1, L):
        sh = jnp.where(sub3 >= tau, pltpu.roll(u3, tau, axis=1), 0.0)
        shifted.append(sh.reshape(tm, SLAB).astype(BF16))
    y_lag = jnp.dot(jnp.concatenate(shifted, axis=1), wlag_ref[...], preferred_element_type=F32)

    for t in tiles:
        hr = x[t][:, :half].reshape(n_blk, SUBLANES, half)
        hi = x[t][:, half:].reshape(n_blk, SUBLANES, half)
        for i, d in enumerate(_SUBLANE_STEPS):
            pr, pi = spow_ref[t, i:i + 1, :half], spow_ref[t, i:i + 1, half:]
            sr, si = _cmul(pr, pi, pltpu.roll(hr, d, axis=1), pltpu.roll(hi, d, axis=1))
            hr = hr + jnp.where(sub3 >= d, sr, 0.0)
            hi = hi + jnp.where(sub3 >= d, si, 0.0)
        cr, ci = carry_ref[t, :, :half], carry_ref[t, :, half:]
        cp_r, cp_i = cpow_ref[t, :, :half], cpow_ref[t, :, half:]
        prev_r, prev_i = [], []
        for kb in range(n_blk):
            kr, ki = _cmul(cp_r, cp_i, cr, ci)
            br, bi = hr[kb] + kr, hi[kb] + ki
            prev_r.append(jnp.where(sub2 == 0, cr, pltpu.roll(br, 1, axis=0)))
            prev_i.append(jnp.where(sub2 == 0, ci, pltpu.roll(bi, 1, axis=0)))
            cr, ci = br[SUBLANES - 1:SUBLANES, :], bi[SUBLANES - 1:SUBLANES, :]
        carry_ref[t, :, :half] = cr
        carry_ref[t, :, half:] = ci
        prev = jnp.concatenate([jnp.concatenate(prev_r, axis=0), jnp.concatenate(prev_i, axis=0)], axis=1)
        z = _dot(prev, ccat_ref[t])
        for s in range(L):
            zt_ref[t, pl.ds(s, n_rows, stride=L), :] = z[:, s * LANES:(s + 1) * LANES]

    y_state = jnp.concatenate([zt_ref[t] for t in tiles], axis=1)
    y_ref[0] = (y_lag + y_state + u * dskip_ref[...]).astype(y_ref.dtype)


def _s5_conv(h, tables, d_skip, *, tm=1024):
    bsz, seq, c = h.shape
    tm = min(tm, seq)
    m_lag, m_in, m_out, step_pow, carry_pow = tables
    n_slab = c // SLAB
    tps = SLAB // LANES
    st = m_in.shape[-1]
    chunk_cols = m_out.shape[-1]
    return pl.pallas_call(
        _s5_kernel,
        grid=(n_slab, bsz, seq // tm),
        in_specs=[
            pl.BlockSpec((1, tm, SLAB), lambda s, b, i: (b, i, s)),
            pl.BlockSpec((1,) + m_lag.shape[1:], lambda s, b, i: (s, 0, 0, 0)),
            pl.BlockSpec((tps,) + m_in.shape[1:], lambda s, b, i: (s, 0, 0, 0)),
            pl.BlockSpec((tps,) + m_out.shape[1:], lambda s, b, i: (s, 0, 0, 0)),
            pl.BlockSpec((tps,) + step_pow.shape[1:], lambda s, b, i: (s, 0, 0)),
            pl.BlockSpec((tps,) + carry_pow.shape[1:], lambda s, b, i: (s, 0, 0)),
            pl.BlockSpec((1, SLAB), lambda s, b, i: (0, s)),
        ],
        out_specs=pl.BlockSpec((1, tm, SLAB), lambda s, b, i: (b, i, s)),
        out_shape=jax.ShapeDtypeStruct((bsz, seq, c), BF16),
        scratch_shapes=[pltpu.VMEM((tps, 1, st), F32), pltpu.VMEM((tps, tm, LANES), F32),
                        pltpu.VMEM((tps, tm, LANES), F32),
                        pltpu.VMEM((S5_CHUNK * SLAB, SLAB), BF16),
                        pltpu.VMEM((tps, chunk_cols, st), BF16),
                        pltpu.VMEM((tps, st, chunk_cols), BF16)],
        compiler_params=_cparams(("arbitrary", "arbitrary", "arbitrary")),
        name="s5_conv",
    )(h, m_lag, m_in, m_out, step_pow, carry_pow, d_skip.reshape(1, c))


def _rwkv_block(h, ln0, ln1, mu, w0, w1, w2, a0, a1, a2, g1, g2, k_k, k_a, r_k, wr, wk, wv, wo,
                lnx_g, lnx_b, mlp_w1, mlp_w2, layer):
    bsz, seq, c = h.shape
    zeros = jnp.zeros((6, c), F32)
    vecs = jnp.concatenate([mu, w0[None], a0[None], k_k[None], k_a[None], zeros], axis=0)
    scan_vecs = jnp.concatenate([r_k.reshape(1, c), lnx_g[None], lnx_b[None], zeros[:5]], axis=0)
    idx = jnp.arange(SLAB) // HEAD
    jmat = (idx[:, None] == idx[None, :]).astype(BF16)
    r, k, v, lw, an, bn, gate = _rw_proj(h, vecs, wr, wk, wv, w1, w2, a1, a2, g1, g2, jmat)
    y = _rw_scan(r, k, v, lw, an, bn, scan_vecs)
    n = bsz * seq
    out = _mixer_out_mlp(_gate_mlp_kernel, "rwkv_out_mlp", [h.reshape(n, c), y.reshape(n, c), gate.reshape(n, c)],
                         wo, ln0, mlp_w1, mlp_w2, layer, ln1)
    return out.reshape(bsz, seq, c)


def _s5_block(h, ln0, ln1, a_re, a_im, log_dt, b_re, b_im, c_re, c_im, d_skip, w_glu, mlp_w1, mlp_w2, layer):
    bsz, seq, c = h.shape
    tables = _s5_tables(a_re, a_im, log_dt, b_re, b_im, c_re, c_im)
    y = _s5_conv(h, tables, d_skip)
    n = bsz * seq
    out = _mixer_out_mlp(_glu_mlp_kernel, "s5_out_mlp", [h.reshape(n, c), y.reshape(n, c)],
                         w_glu, ln0, mlp_w1, mlp_w2, layer, ln1)
    return out.reshape(bsz, seq, c)


def kernel(x, ln_g, ln_b, rw_mu, rw_w0, rw_w1, rw_w2, rw_a0, rw_a1, rw_a2, rw_g1, rw_g2, rw_k_k, rw_k_a, rw_r_k, rw_wr, rw_wk, rw_wv, rw_wo, rw_lnx_g, rw_lnx_b, s5_a_re, s5_a_im, s5_log_dt, s5_b_re, s5_b_im, s5_c_re, s5_c_im, s5_d, s5_w_glu, mlp_w1, mlp_w2):
    w1b, w2b = mlp_w1.astype(BF16), mlp_w2
    h = _rwkv_block(x, (ln_g[0], ln_b[0]), (ln_g[1], ln_b[1]), rw_mu[0], rw_w0[0], rw_w1[0], rw_w2[0],
                    rw_a0[0], rw_a1[0], rw_a2[0], rw_g1[0], rw_g2[0], rw_k_k[0], rw_k_a[0], rw_r_k[0],
                    rw_wr[0], rw_wk[0], rw_wv[0], rw_wo[0], rw_lnx_g[0], rw_lnx_b[0], w1b, w2b, 0)
    h = _s5_block(h, (ln_g[2], ln_b[2]), (ln_g[3], ln_b[3]), s5_a_re[0], s5_a_im[0], s5_log_dt[0],
                  s5_b_re[0], s5_b_im[0], s5_c_re[0], s5_c_im[0], s5_d[0], s5_w_glu[0], w1b, w2b, 1)
    return h
```

```python
import functools
import math

import jax
import jax.numpy as jnp
from jax import lax
from jax.experimental import pallas as pl
from jax.experimental.pallas import tpu as pltpu

F32 = jnp.float32
BF16 = jnp.bfloat16

DEPTH = 2
DN_ALPHA = (2.0 * DEPTH) ** 0.25
LN_EPS = 1e-5
GN_EPS = 64e-5

LANES = 128
SUBLANES = 8
_SUBLANE_STEPS = (1, 2, 4)
HEAD = 64
PAIR = 2 * HEAD
SLAB = 256
RW_CHUNK = 64
S5_CHUNK = 8
V7X_VMEM_BYTES = 64 * 1024 * 1024
VMEM_LIMIT = V7X_VMEM_BYTES * 7 // 8


def _cparams(sem):
    return pltpu.CompilerParams(dimension_semantics=sem, vmem_limit_bytes=VMEM_LIMIT)


def _layer_norm(z, g, b):
    mu = jnp.mean(z, axis=-1, keepdims=True)
    d = z - mu
    var = jnp.mean(d * d, axis=-1, keepdims=True)
    return d * lax.rsqrt(var + LN_EPS) * g + b


def _dot(a, b):
    return jnp.dot(a.astype(BF16), b.astype(BF16), preferred_element_type=F32)


def _dot_t(a, b):
    return lax.dot_general(a.astype(BF16), b.astype(BF16), (((1,), (1,)), ((), ())),
                           preferred_element_type=F32)


def _tdot(a, b):
    return lax.dot_general(a.astype(BF16), b.astype(BF16), (((0,), (0,)), ((), ())),
                           preferred_element_type=F32)


MLP_FF_CHUNK = 1024


ROW_SPLIT = 2


def _mixer_out_mlp_body(mix_fn, x_ref, g0_ref, b0_ref, w1_ref, w2_ref, g1_ref, b1_ref, o_ref):
    tm = x_ref.shape[0]
    d_ff = w1_ref.shape[1]
    chunk = min(MLP_FF_CHUNK, d_ff)
    groups = [slice(i * tm // ROW_SPLIT, (i + 1) * tm // ROW_SPLIT) for i in range(ROW_SPLIT)]
    mix = [mix_fn(rows) for rows in groups]
    h = [_layer_norm(DN_ALPHA * x_ref[rows, :] + m, g0_ref[...], b0_ref[...]) for rows, m in zip(groups, mix)]
    hb = [v.astype(BF16) for v in h]
    acc = [jnp.zeros(v.shape, F32) for v in h]
    for c in range(d_ff // chunk):
        up = [jnp.dot(v, w1_ref[:, c * chunk:(c + 1) * chunk], preferred_element_type=F32) for v in hb]
        act = [jnp.square(jnp.maximum(u, 0.0)).astype(BF16) for u in up]
        w2c = w2_ref[c * chunk:(c + 1) * chunk, :].astype(BF16)
        acc = [s + jnp.dot(a, w2c, preferred_element_type=F32) for s, a in zip(acc, act)]
    for rows, v, s in zip(groups, h, acc):
        o_ref[rows, :] = _layer_norm(DN_ALPHA * v + s, g1_ref[...], b1_ref[...])


def _gate_mlp_kernel(x_ref, y_ref, gate_ref, wo_ref, g0_ref, b0_ref, w1_ref, w2_ref, g1_ref, b1_ref, o_ref):
    def mix(rows):
        return _dot(y_ref[rows, :].astype(F32) * gate_ref[rows, :].astype(F32), wo_ref[...])

    _mixer_out_mlp_body(mix, x_ref, g0_ref, b0_ref, w1_ref, w2_ref, g1_ref, b1_ref, o_ref)


def _glu_mlp_kernel(x_ref, y_ref, wglu_ref, g0_ref, b0_ref, w1_ref, w2_ref, g1_ref, b1_ref, o_ref):
    c = x_ref.shape[1]

    def mix(rows):
        y = y_ref[rows, :].astype(F32)
        y = 0.5 * y * (1.0 + jnp.tanh(math.sqrt(2.0 / math.pi) * (y + 0.044715 * (y * y * y))))
        z = _dot(y, wglu_ref[...])
        return z[:, :c] * jax.nn.sigmoid(z[:, c:])

    _mixer_out_mlp_body(mix, x_ref, g0_ref, b0_ref, w1_ref, w2_ref, g1_ref, b1_ref, o_ref)


def _mixer_out_mlp(body, name, row_inputs, w_mix, ln0, mlp_w1, mlp_w2, layer, ln1, *, tm=512):
    n, c = row_inputs[0].shape
    tm = min(tm, n)
    tile = pl.BlockSpec((tm, c), lambda i: (i, 0))
    const = lambda i: (0, 0)
    vec = pl.BlockSpec((1, c), const)

    def stacked(w):
        return pl.BlockSpec((None,) + w.shape[1:], lambda i: (layer, 0, 0), pipeline_mode=pl.Buffered(1))

    return pl.pallas_call(
        body,
        grid=(n // tm,),
        in_specs=[tile] * len(row_inputs) + [pl.BlockSpec(w_mix.shape, const, pipeline_mode=pl.Buffered(1)),
                                             vec, vec, stacked(mlp_w1), stacked(mlp_w2), vec, vec],
        out_specs=tile,
        out_shape=jax.ShapeDtypeStruct((n, c), F32),
        compiler_params=_cparams(("parallel",)),
        name=name,
    )(*row_inputs, w_mix, ln0[0].reshape(1, c), ln0[1].reshape(1, c), mlp_w1, mlp_w2,
      ln1[0].reshape(1, c), ln1[1].reshape(1, c))


_V_MU, _V_W0, _V_A0, _V_KK, _V_KA = 0, 6, 7, 8, 9


def _head_sum(x, j_ref):
    parts = [_dot(x[:, s:s + SLAB], j_ref[...]) for s in range(0, x.shape[1], SLAB)]
    return jnp.concatenate(parts, axis=1)


def _rw_proj_kernel(x_ref, xp_ref, vec_ref, wr_ref, wk_ref, wv_ref, w1_ref, w2_ref, a1_ref, a2_ref,
                    g1_ref, g2_ref, j_ref, r_ref, k_ref, v_ref, lw_ref, an_ref, bn_ref, g_ref):
    x = x_ref[0]
    tm = x.shape[0]
    row = lax.broadcasted_iota(jnp.int32, (tm, 1), 0)
    prev_last = jnp.where(pl.program_id(1) == 0, 0.0, xp_ref[0][SUBLANES - 1:SUBLANES, :])
    xprev = jnp.where(row == 0, prev_last, pltpu.roll(x, 1, axis=0))
    xx = xprev - x

    def mix(i):
        return x + xx * vec_ref[_V_MU + i:_V_MU + i + 1, :]

    r = _dot(mix(0), wr_ref[...])
    w_pre = vec_ref[_V_W0:_V_W0 + 1, :] + _dot(jnp.tanh(_dot(mix(1), w1_ref[...])), w2_ref[...])
    k = _dot(mix(2), wk_ref[...])
    v = _dot(mix(3), wv_ref[...])
    a = jax.nn.sigmoid(vec_ref[_V_A0:_V_A0 + 1, :] + _dot(_dot(mix(4), a1_ref[...]), a2_ref[...]))
    g = _dot(jax.nn.sigmoid(_dot(mix(5), g1_ref[...])), g2_ref[...])

    kk = k * vec_ref[_V_KK:_V_KK + 1, :]
    kk = kk * lax.rsqrt(jnp.maximum(_head_sum(kk * kk, j_ref), 1e-24))
    r_ref[0] = r.astype(r_ref.dtype)
    k_ref[0] = (k * (1.0 + (a - 1.0) * vec_ref[_V_KA:_V_KA + 1, :])).astype(k_ref.dtype)
    v_ref[0] = v.astype(v_ref.dtype)
    lw_ref[0] = (-math.exp(-0.5)) * jax.nn.sigmoid(w_pre)
    an_ref[0] = (-kk).astype(an_ref.dtype)
    bn_ref[0] = (kk * a).astype(bn_ref.dtype)
    g_ref[0] = g.astype(g_ref.dtype)


def _rw_proj(x, vecs, wr, wk, wv, w1, w2, a1, a2, g1, g2, jmat, *, tm=512):
    bsz, seq, c = x.shape
    tm = min(tm, seq)
    const = lambda b, i: (0, 0)
    tile = pl.BlockSpec((1, tm, c), lambda b, i: (b, i, 0))
    prev = pl.BlockSpec((1, SUBLANES, c), lambda b, i: (b, jnp.maximum(i * (tm // SUBLANES) - 1, 0), 0))

    def full(w):
        return pl.BlockSpec(w.shape, const)

    ws = [w.astype(BF16) for w in (wr, wk, wv, w1, w2, a1, a2, g1, g2)]
    out = lambda dt: jax.ShapeDtypeStruct((bsz, seq, c), dt)
    return pl.pallas_call(
        _rw_proj_kernel,
        grid=(bsz, seq // tm),
        in_specs=[tile, prev, full(vecs)] + [full(w) for w in ws] + [full(jmat)],
        out_specs=[tile] * 7,
        out_shape=[out(BF16), out(BF16), out(BF16), out(F32), out(BF16), out(BF16), out(BF16)],
        compiler_params=_cparams(("parallel", "arbitrary")),
        name="rwkv_proj",
    )(x, x, vecs, *ws, jmat)


def _rw_scan_kernel(r_ref, k_ref, v_ref, lw_ref, an_ref, bn_ref, vec_ref, y_ref, state_ref):
    nb, tm, c = r_ref.shape
    n_pairs = c // PAIR
    L = RW_CHUNK

    @pl.when(pl.program_id(1) == 0)
    def _():
        state_ref[...] = jnp.zeros_like(state_ref)

    sub3 = lax.broadcasted_iota(jnp.int32, (1, SUBLANES, 1), 1)

    def chunk_cumsum(x):
        x3 = x.reshape(L // SUBLANES, SUBLANES, x.shape[1])
        for d in _SUBLANE_STEPS:
            x3 = x3 + jnp.where(sub3 >= d, pltpu.roll(x3, d, axis=1), 0.0)
        blocks, run = [], None
        for kb in range(L // SUBLANES):
            blk = x3[kb] if run is None else x3[kb] + run
            blocks.append(blk)
            run = blk[SUBLANES - 1:SUBLANES, :]
        return jnp.concatenate(blocks, axis=0)

    ri = lax.broadcasted_iota(jnp.int32, (PAIR, PAIR), 0)
    ci = lax.broadcasted_iota(jnp.int32, (PAIR, PAIR), 1)
    same_head = (ri // L) == (ci // L)
    mask_strict = same_head & ((ci % L) < (ri % L))
    mask_incl = same_head & ((ci % L) <= (ri % L))
    eye = (ri == ci).astype(F32)
    lane = lax.broadcasted_iota(jnp.int32, (L, PAIR), 1)
    head0 = lane < HEAD
    ri_h = lax.broadcasted_iota(jnp.int32, (PAIR // 2, PAIR), 0)
    ci_h = lax.broadcasted_iota(jnp.int32, (PAIR // 2, PAIR), 1)

    def level_mask(dd):
        return same_head & ((ri // (2 * dd)) == (ci // (2 * dd))) & ((ri % (2 * dd)) >= dd) & ((ci % (2 * dd)) < dd)

    def sel(res):
        return jnp.where(head0, res[:L], res[L:])

    def dup(x):
        return jnp.concatenate([x, x], axis=0)

    def head_sum(x):
        s0 = jnp.sum(jnp.where(head0, x, 0.0), axis=1, keepdims=True)
        s1 = jnp.sum(jnp.where(head0, 0.0, x), axis=1, keepdims=True)
        return jnp.where(head0, s0, s1)

    def take_rows(x, dd, half):
        return jnp.concatenate([x[(2 * b + half) * dd:(2 * b + half + 1) * dd]
                                for b in range(PAIR // (2 * dd))], axis=0)

    def merge_rows(first, second, dd):
        parts = []
        for b in range(PAIR // (2 * dd)):
            parts += [first[b * dd:(b + 1) * dd], second[b * dd:(b + 1) * dd]]
        return jnp.concatenate(parts, axis=0)

    n_chunks = tm // L
    per_chunk = nb * n_pairs
    probs = range(n_chunks * per_chunk)
    rows = [slice((p // per_chunk) * L, (p // per_chunk + 1) * L) for p in probs]
    seq = [(p % per_chunk) // n_pairs for p in probs]
    lanes = [slice((p % n_pairs) * PAIR, (p % n_pairs + 1) * PAIR) for p in probs]

    def tile(ref, p):
        return ref[seq[p], rows[p], lanes[p]]

    rhs, ar, bk, vv, e_l = [], [], [], [], []
    for p in probs:
        r = tile(r_ref, p).astype(F32)
        k = tile(k_ref, p).astype(F32)
        a = tile(an_ref, p).astype(F32)
        b = tile(bn_ref, p).astype(F32)
        lw = tile(lw_ref, p)
        cs_c = chunk_cumsum(lw)
        cs_last = cs_c[L - 1:L, :]
        a_t = a * jnp.exp(cs_c - lw)
        r_t = r * jnp.exp(cs_c)
        e_m = jnp.exp(-cs_c)
        b_t = b * e_m
        k_t = k * e_m
        rhs.append(jnp.concatenate([jnp.where(head0, b_t, 0.0), jnp.where(head0, 0.0, b_t),
                                    jnp.where(head0, k_t, 0.0), jnp.where(head0, 0.0, k_t)], axis=0).astype(BF16))
        ar.append(jnp.concatenate([a_t, r_t], axis=0).astype(BF16))
        bk.append(jnp.concatenate([b_t, k_t], axis=0).astype(BF16))
        vv.append(dup(tile(v_ref, p)))
        e_l.append(jnp.exp(cs_last))

    sc = [_dot_t(ar[p], rhs[p]) for p in probs]
    n_ab = [jnp.where(mask_strict, dup(sc[p][:L, :PAIR]), 0.0) for p in probs]
    n_ak = [jnp.where(mask_strict, dup(sc[p][:L, PAIR:]), 0.0).astype(BF16) for p in probs]
    n_r = [jnp.concatenate([jnp.where(mask_incl, dup(sc[p][L:, :PAIR]), 0.0),
                            jnp.where(mask_incl, dup(sc[p][L:, PAIR:]), 0.0)], axis=1).astype(BF16) for p in probs]
    w0 = [sel(_dot(n_ak[p], vv[p])) for p in probs]

    t_inv = [jnp.where(level_mask(1), n_ab[p], eye) for p in probs]
    n_ab_b = [n_ab[p].astype(BF16) for p in probs]
    dd = 2
    while dd < SUBLANES:
        lm = level_mask(dd)
        nt = [jnp.where(lm, _dot(n_ab_b[p], t_inv[p]), 0.0) for p in probs]
        t_inv = [t_inv[p] + _dot(t_inv[p], nt[p]) for p in probs]
        dd *= 2
    zero_half = jnp.zeros((PAIR // 2, PAIR), F32)
    while dd < L:
        lm_hi = ((ri_h // dd) == (ci_h // (2 * dd))) & ((ci_h % (2 * dd)) < dd)
        nt = [merge_rows(zero_half, jnp.where(lm_hi, _dot(take_rows(n_ab[p], dd, 1), t_inv[p]), 0.0), dd)
              for p in probs]
        t_hi = [take_rows(t_inv[p], dd, 1) for p in probs]
        t_inv = [merge_rows(take_rows(t_inv[p], dd, 0), t_hi[p] + _dot(t_hi[p], nt[p]), dd) for p in probs]
        dd *= 2
    t_inv = [t_inv[p].astype(BF16) for p in probs]

    for ch in range(n_chunks):
        cur = range(ch * per_chunk, (ch + 1) * per_chunk)
        slot = {p: p - ch * per_chunk for p in cur}
        s_prev = {p: state_ref[slot[p]] for p in cur}
        ars = {p: _dot_t(ar[p], s_prev[p]) for p in cur}
        u = {p: sel(_dot(t_inv[p], dup(ars[p][:L] + w0[p]))) for p in cur}
        o = {p: ars[p][L:] + sel(_dot(n_r[p], jnp.concatenate([dup(u[p]).astype(BF16), vv[p]], axis=0))) for p in cur}
        for p in cur:
            upd = _tdot(jnp.concatenate([u[p].astype(BF16), vv[p][:L]], axis=0), bk[p])
            state_ref[slot[p]] = (s_prev[p] + jnp.where(same_head, upd, 0.0)) * e_l[p]
        for p in cur:
            dlt = o[p] - head_sum(o[p]) * (1.0 / HEAD)
            var = head_sum(dlt * dlt) * (1.0 / HEAD)
            rk = tile(r_ref, p).astype(F32) * tile(k_ref, p).astype(F32)
            bonus = head_sum(rk * vec_ref[0:1, lanes[p]]) * tile(v_ref, p).astype(F32)
            y = dlt * lax.rsqrt(var + GN_EPS) * vec_ref[1:2, lanes[p]] + vec_ref[2:3, lanes[p]] + bonus
            y_ref[seq[p], rows[p], lanes[p]] = y.astype(y_ref.dtype)


def _rw_scan(r, k, v, lw, an, bn, vecs, *, tm=128, nb=4):
    bsz, seq, c = r.shape
    tm = min(tm, seq)
    nb = math.gcd(nb, bsz)
    tile = pl.BlockSpec((nb, tm, c), lambda b, i: (b, i, 0))
    const = lambda b, i: (0, 0)
    return pl.pallas_call(
        _rw_scan_kernel,
        grid=(bsz // nb, seq // tm),
        in_specs=[tile] * 6 + [pl.BlockSpec(vecs.shape, const)],
        out_specs=tile,
        out_shape=jax.ShapeDtypeStruct((bsz, seq, c), BF16),
        scratch_shapes=[pltpu.VMEM((nb * (c // PAIR), PAIR, PAIR), F32)],
        compiler_params=_cparams(("parallel", "arbitrary")),
        name="rwkv_scan",
    )(r, k, v, lw, an, bn, vecs)


def _s5_tables(a_re, a_im, log_dt, b_re, b_im, c_re, c_im):
    g, p = a_re.shape
    s = b_re.shape[-1]
    L = S5_CHUNK
    dt = jnp.exp(log_dt.astype(F32))[:, None]
    lam_re = jnp.minimum(a_re.astype(F32), -1e-4)
    lam_im = a_im.astype(F32)

    def apow(n):
        n = jnp.asarray(n, F32)[:, None, None]
        mag = jnp.exp(n * dt * lam_re)
        return mag * jnp.cos(n * dt * lam_im), mag * jnp.sin(n * dt * lam_im)

    ab_re, ab_im = apow([1])
    ab_re, ab_im = ab_re[0], ab_im[0]
    den = lam_re * lam_re + lam_im * lam_im
    nr, ni = ab_re - 1.0, ab_im
    coef_re = (nr * lam_re + ni * lam_im) / den
    coef_im = (ni * lam_re - nr * lam_im) / den
    bb_re = coef_re[..., None] * b_re - coef_im[..., None] * b_im
    bb_im = coef_re[..., None] * b_im + coef_im[..., None] * b_re
    cr, ci = c_re.astype(F32), c_im.astype(F32)

    pw_re, pw_im = apow(jnp.arange(L))
    cb_re = (jnp.einsum('gcp,tgp,gpd->gtcd', cr, pw_re, bb_re) - jnp.einsum('gcp,tgp,gpd->gtcd', cr, pw_im, bb_im)
             - jnp.einsum('gcp,tgp,gpd->gtcd', ci, pw_re, bb_im) - jnp.einsum('gcp,tgp,gpd->gtcd', ci, pw_im, bb_re))

    gs = SLAB // s
    n_slab = g // gs
    m_lag = cb_re.reshape(n_slab, gs, L, s, s).transpose(0, 2, 4, 1, 3).reshape(n_slab, L, s, SLAB)

    gt = LANES // s
    n_t = g // gt
    st = 2 * gt * p
    rv_re, rv_im = pw_re[::-1], pw_im[::-1]
    in_re = rv_re[..., None] * bb_re[None] - rv_im[..., None] * bb_im[None]
    in_im = rv_re[..., None] * bb_im[None] + rv_im[..., None] * bb_re[None]
    m_in = jnp.stack([in_re, in_im], axis=0).reshape(2, L, n_t, gt, p, s)
    m_in = m_in.transpose(2, 1, 5, 0, 3, 4).reshape(n_t, L, s, st)
    p1_re, p1_im = apow(jnp.arange(1, L + 1))
    out_re = cr[None] * p1_re[:, :, None, :] - ci[None] * p1_im[:, :, None, :]
    out_im = cr[None] * p1_im[:, :, None, :] + ci[None] * p1_re[:, :, None, :]
    m_out = jnp.stack([out_re, -out_im], axis=0).reshape(2, L, n_t, gt, s, p)
    m_out = m_out.transpose(2, 0, 5, 1, 3, 4).reshape(n_t, 2, p, L * LANES)

    def lay_vec(re, im):
        n = re.shape[0]
        f = lambda m: m.reshape(n, n_t, gt * p).transpose(1, 0, 2)
        return jnp.concatenate([f(re), f(im)], axis=-1)

    step_pow = lay_vec(*apow([L * d for d in _SUBLANE_STEPS]))
    carry_pow = lay_vec(*apow(L * jnp.arange(1, SUBLANES + 1)))
    return m_lag, m_in, m_out, step_pow, carry_pow


def _cmul(ar, ai, br, bi):
    return ar * br - ai * bi, ar * bi + ai * br


def _s5_kernel(u_ref, mlag_ref, min_ref, mout_ref, spow_ref, cpow_ref, dskip_ref, y_ref,
               carry_ref, ut_ref, zt_ref, wlag_ref, bcat_ref, ccat_ref):
    tm = u_ref.shape[1]
    L = S5_CHUNK
    assert L == SUBLANES
    n_rows = tm // L
    half = bcat_ref.shape[2] // 2
    tiles = range(SLAB // LANES)

    @pl.when(pl.program_id(2) == 0)
    def _():
        carry_ref[...] = jnp.zeros_like(carry_ref)

    @pl.when((pl.program_id(1) == 0) & (pl.program_id(2) == 0))
    def _():
        def expand(dst, table, n_groups, col_group_size):
            rows, cols = table.shape
            col_group = (lax.broadcasted_iota(jnp.int32, (1, cols), 1) // col_group_size) % n_groups
            for grp in range(n_groups):
                dst[pl.ds(grp * rows, rows), :] = jnp.where(col_group == grp, table, 0.0).astype(dst.dtype)

        s_ch, p = min_ref.shape[2], mout_ref.shape[2]
        for tau in range(L):
            n_grp = SLAB // s_ch
            expand(wlag_ref.at[pl.ds(tau * SLAB, SLAB), :], mlag_ref[0, tau], n_grp, s_ch)
        for t in tiles:
            n_grp = LANES // s_ch
            for s in range(L):
                expand(bcat_ref.at[t, pl.ds(s * LANES, LANES), :], min_ref[t, s], n_grp, p)
            for ri in range(2):
                expand(ccat_ref.at[t, pl.ds(ri * n_grp * p, n_grp * p), :], mout_ref[t, ri], n_grp, s_ch)

    u = u_ref[0]
    sub3 = lax.broadcasted_iota(jnp.int32, (1, SUBLANES, 1), 1)
    sub2 = lax.broadcasted_iota(jnp.int32, (SUBLANES, 1), 0)
    n_blk = n_rows // SUBLANES

    x = []
    for t in tiles:
        ut_ref[t] = u[:, t * LANES:(t + 1) * LANES]
        ucat = jnp.concatenate([ut_ref[t, pl.ds(s, n_rows, stride=L), :] for s in range(L)], axis=1)
        x.append(_dot(ucat, bcat_ref[t]))

    u3 = u.reshape(tm // SUBLANES, SUBLANES, SLAB)
    shifted = [u.astype(BF16)]
    for tau in range(1, L):
        sh = jnp.where(sub3 >= tau, pltpu.roll(u3, tau, axis=1), 0.0)
        shifted.append(sh.reshape(tm, SLAB).astype(BF16))
    y_lag = jnp.dot(jnp.concatenate(shifted, axis=1), wlag_ref[...], preferred_element_type=F32)

    for t in tiles:
        hr = x[t][:, :half].reshape(n_blk, SUBLANES, half)
        hi = x[t][:, half:].reshape(n_blk, SUBLANES, half)
        for i, d in enumerate(_SUBLANE_STEPS):
            pr, pi = spow_ref[t, i:i + 1, :half], spow_ref[t, i:i + 1, half:]
            sr, si = _cmul(pr, pi, pltpu.roll(hr, d, axis=1), pltpu.roll(hi, d, axis=1))
            hr = hr + jnp.where(sub3 >= d, sr, 0.0)
            hi = hi + jnp.where(sub3 >= d, si, 0.0)
        cr, ci = carry_ref[t, :, :half], carry_ref[t, :, half:]
        cp_r, cp_i = cpow_ref[t, :, :half], cpow_ref[t, :, half:]
        prev_r, prev_i = [], []
        for kb in range(n_blk):
            kr, ki = _cmul(cp_r, cp_i, cr, ci)
            br, bi = hr[kb] + kr, hi[kb] + ki
            prev_r.append(jnp.where(sub2 == 0, cr, pltpu.roll(br, 1, axis=0)))
            prev_i.append(jnp.where(sub2 == 0, ci, pltpu.roll(bi, 1, axis=0)))
            cr, ci = br[SUBLANES - 1:SUBLANES, :], bi[SUBLANES - 1:SUBLANES, :]
        carry_ref[t, :, :half] = cr
        carry_ref[t, :, half:] = ci
        prev = jnp.concatenate([jnp.concatenate(prev_r, axis=0), jnp.concatenate(prev_i, axis=0)], axis=1)
        z = _dot(prev, ccat_ref[t])
        for s in range(L):
            zt_ref[t, pl.ds(s, n_rows, stride=L), :] = z[:, s * LANES:(s + 1) * LANES]

    y_state = jnp.concatenate([zt_ref[t] for t in tiles], axis=1)
    y_ref[0] = (y_lag + y_state + u * dskip_ref[...]).astype(y_ref.dtype)


def _s5_conv(h, tables, d_skip, *, tm=1024):
    bsz, seq, c = h.shape
    tm = min(tm, seq)
    m_lag, m_in, m_out, step_pow, carry_pow = tables
    n_slab = c // SLAB
    tps = SLAB // LANES
    st = m_in.shape[-1]
    chunk_cols = m_out.shape[-1]
    return pl.pallas_call(
        _s5_kernel,
        grid=(n_slab, bsz, seq // tm),
        in_specs=[
            pl.BlockSpec((1, tm, SLAB), lambda s, b, i: (b, i, s)),
            pl.BlockSpec((1,) + m_lag.shape[1:], lambda s, b, i: (s, 0, 0, 0)),
            pl.BlockSpec((tps,) + m_in.shape[1:], lambda s, b, i: (s, 0, 0, 0)),
            pl.BlockSpec((tps,) + m_out.shape[1:], lambda s, b, i: (s, 0, 0, 0)),
            pl.BlockSpec((tps,) + step_pow.shape[1:], lambda s, b, i: (s, 0, 0)),
            pl.BlockSpec((tps,) + carry_pow.shape[1:], lambda s, b, i: (s, 0, 0)),
            pl.BlockSpec((1, SLAB), lambda s, b, i: (0, s)),
        ],
        out_specs=pl.BlockSpec((1, tm, SLAB), lambda s, b, i: (b, i, s)),
        out_shape=jax.ShapeDtypeStruct((bsz, seq, c), BF16),
        scratch_shapes=[pltpu.VMEM((tps, 1, st), F32), pltpu.VMEM((tps, tm, LANES), F32),
                        pltpu.VMEM((tps, tm, LANES), F32),
                        pltpu.VMEM((S5_CHUNK * SLAB, SLAB), BF16),
                        pltpu.VMEM((tps, chunk_cols, st), BF16),
                        pltpu.VMEM((tps, st, chunk_cols), BF16)],
        compiler_params=_cparams(("arbitrary", "arbitrary", "arbitrary")),
        name="s5_conv",
    )(h, m_lag, m_in, m_out, step_pow, carry_pow, d_skip.reshape(1, c))


def _rwkv_block(h, ln0, ln1, mu, w0, w1, w2, a0, a1, a2, g1, g2, k_k, k_a, r_k, wr, wk, wv, wo,
                lnx_g, lnx_b, mlp_w1, mlp_w2, layer):
    bsz, seq, c = h.shape
    zeros = jnp.zeros((6, c), F32)
    vecs = jnp.concatenate([mu, w0[None], a0[None], k_k[None], k_a[None], zeros], axis=0)
    scan_vecs = jnp.concatenate([r_k.reshape(1, c), lnx_g[None], lnx_b[None], zeros[:5]], axis=0)
    idx = jnp.arange(SLAB) // HEAD
    jmat = (idx[:, None] == idx[None, :]).astype(BF16)
    r, k, v, lw, an, bn, gate = _rw_proj(h, vecs, wr, wk, wv, w1, w2, a1, a2, g1, g2, jmat)
    y = _rw_scan(r, k, v, lw, an, bn, scan_vecs)
    n = bsz * seq
    out = _mixer_out_mlp(_gate_mlp_kernel, "rwkv_out_mlp", [h.reshape(n, c), y.reshape(n, c), gate.reshape(n, c)],
                         wo, ln0, mlp_w1, mlp_w2, layer, ln1)
    return out.reshape(bsz, seq, c)


def _s5_block(h, ln0, ln1, a_re, a_im, log_dt, b_re, b_im, c_re, c_im, d_skip, w_glu, mlp_w1, mlp_w2, layer):
    bsz, seq, c = h.shape
    tables = _s5_tables(a_re, a_im, log_dt, b_re, b_im, c_re, c_im)
    y = _s5_conv(h, tables, d_skip)
    n = bsz * seq
    out = _mixer_out_mlp(_glu_mlp_kernel, "s5_out_mlp", [h.reshape(n, c), y.reshape(n, c)],
                         w_glu, ln0, mlp_w1, mlp_w2, layer, ln1)
    return out.reshape(bsz, seq, c)


def kernel(x, ln_g, ln_b, rw_mu, rw_w0, rw_w1, rw_w2, rw_a0, rw_a1, rw_a2, rw_g1, rw_g2, rw_k_k, rw_k_a, rw_r_k, rw_wr, rw_wk, rw_wv, rw_wo, rw_lnx_g, rw_lnx_b, s5_a_re, s5_a_im, s5_log_dt, s5_b_re, s5_b_im, s5_c_re, s5_c_im, s5_d, s5_w_glu, mlp_w1, mlp_w2):
    w1b, w2b = mlp_w1.astype(BF16), mlp_w2
    h = _rwkv_block(x, (ln_g[0], ln_b[0]), (ln_g[1], ln_b[1]), rw_mu[0], rw_w0[0], rw_w1[0], rw_w2[0],
                    rw_a0[0], rw_a1[0], rw_a2[0], rw_g1[0], rw_g2[0], rw_k_k[0], rw_k_a[0], rw_r_k[0],
                    rw_wr[0], rw_wk[0], rw_wv[0], rw_wo[0], rw_lnx_g[0], rw_lnx_b[0], w1b, w2b, 0)
    h = _s5_block(h, (ln_g[2], ln_b[2]), (ln_g[3], ln_b[3]), s5_a_re[0], s5_a_im[0], s5_log_dt[0],
                  s5_b_re[0], s5_b_im[0], s5_c_re[0], s5_c_im[0], s5_d[0], s5_w_glu[0], w1b, w2b, 1)
    return h
```

```python
import functools
import math

import jax
import jax.numpy as jnp
from jax import lax
from jax.experimental import pallas as pl
from jax.experimental.pallas import tpu as pltpu

F32 = jnp.float32
BF16 = jnp.bfloat16

DEPTH = 2
DN_ALPHA = (2.0 * DEPTH) ** 0.25
LN_EPS = 1e-5
GN_EPS = 64e-5

LANES = 128
SUBLANES = 8
_SUBLANE_STEPS = (1, 2, 4)
HEAD = 64
PAIR = 2 * HEAD
SLAB = 256
RW_CHUNK = 64
S5_CHUNK = 8
V7X_VMEM_BYTES = 64 * 1024 * 1024
VMEM_LIMIT = V7X_VMEM_BYTES * 7 // 8


def _cparams(sem):
    return pltpu.CompilerParams(dimension_semantics=sem, vmem_limit_bytes=VMEM_LIMIT)


def _layer_norm(z, g, b):
    mu = jnp.mean(z, axis=-1, keepdims=True)
    d = z - mu
    var = jnp.mean(d * d, axis=-1, keepdims=True)
    return d * lax.rsqrt(var + LN_EPS) * g + b


def _dot(a, b):
    return jnp.dot(a.astype(BF16), b.astype(BF16), preferred_element_type=F32)


def _dot_t(a, b):
    return lax.dot_general(a.astype(BF16), b.astype(BF16), (((1,), (1,)), ((), ())),
                           preferred_element_type=F32)


def _tdot(a, b):
    return lax.dot_general(a.astype(BF16), b.astype(BF16), (((0,), (0,)), ((), ())),
                           preferred_element_type=F32)


MLP_FF_CHUNK = 1024


ROW_SPLIT = 2


def _mixer_out_mlp_body(mix_fn, x_ref, g0_ref, b0_ref, w1_ref, w2_ref, g1_ref, b1_ref, o_ref):
    tm = x_ref.shape[0]
    d_ff = w1_ref.shape[1]
    chunk = min(MLP_FF_CHUNK, d_ff)
    groups = [slice(i * tm // ROW_SPLIT, (i + 1) * tm // ROW_SPLIT) for i in range(ROW_SPLIT)]
    mix = [mix_fn(rows) for rows in groups]
    h = [_layer_norm(DN_ALPHA * x_ref[rows, :] + m, g0_ref[...], b0_ref[...]) for rows, m in zip(groups, mix)]
    hb = [v.astype(BF16) for v in h]
    acc = [jnp.zeros(v.shape, F32) for v in h]
    for c in range(d_ff // chunk):
        up = [jnp.dot(v, w1_ref[:, c * chunk:(c + 1) * chunk], preferred_element_type=F32) for v in hb]
        act = [jnp.square(jnp.maximum(u, 0.0)).astype(BF16) for u in up]
        w2c = w2_ref[c * chunk:(c + 1) * chunk, :].astype(BF16)
        acc = [s + jnp.dot(a, w2c, preferred_element_type=F32) for s, a in zip(acc, act)]
    for rows, v, s in zip(groups, h, acc):
        o_ref[rows, :] = _layer_norm(DN_ALPHA * v + s, g1_ref[...], b1_ref[...])


def _gate_mlp_kernel(x_ref, y_ref, gate_ref, wo_ref, g0_ref, b0_ref, w1_ref, w2_ref, g1_ref, b1_ref, o_ref):
    def mix(rows):
        return _dot(y_ref[rows, :].astype(F32) * gate_ref[rows, :].astype(F32), wo_ref[...])

    _mixer_out_mlp_body(mix, x_ref, g0_ref, b0_ref, w1_ref, w2_ref, g1_ref, b1_ref, o_ref)


def _glu_mlp_kernel(x_ref, y_ref, wglu_ref, g0_ref, b0_ref, w1_ref, w2_ref, g1_ref, b1_ref, o_ref):
    c = x_ref.shape[1]

    def mix(rows):
        y = y_ref[rows, :].astype(F32)
        y = 0.5 * y * (1.0 + jnp.tanh(math.sqrt(2.0 / math.pi) * (y + 0.044715 * (y * y * y))))
        z = _dot(y, wglu_ref[...])
        return z[:, :c] * jax.nn.sigmoid(z[:, c:])

    _mixer_out_mlp_body(mix, x_ref, g0_ref, b0_ref, w1_ref, w2_ref, g1_ref, b1_ref, o_ref)


def _mixer_out_mlp(body, name, row_inputs, w_mix, ln0, mlp_w1, mlp_w2, layer, ln1, *, tm=512):
    n, c = row_inputs[0].shape
    tm = min(tm, n)
    tile = pl.BlockSpec((tm, c), lambda i: (i, 0))
    const = lambda i: (0, 0)
    vec = pl.BlockSpec((1, c), const)

    def stacked(w):
        return pl.BlockSpec((None,) + w.shape[1:], lambda i: (layer, 0, 0), pipeline_mode=pl.Buffered(1))

    return pl.pallas_call(
        body,
        grid=(n // tm,),
        in_specs=[tile] * len(row_inputs) + [pl.BlockSpec(w_mix.shape, const, pipeline_mode=pl.Buffered(1)),
                                             vec, vec, stacked(mlp_w1), stacked(mlp_w2), vec, vec],
        out_specs=tile,
        out_shape=jax.ShapeDtypeStruct((n, c), F32),
        compiler_params=_cparams(("parallel",)),
        name=name,
    )(*row_inputs, w_mix, ln0[0].reshape(1, c), ln0[1].reshape(1, c), mlp_w1, mlp_w2,
      ln1[0].reshape(1, c), ln1[1].reshape(1, c))


_V_MU, _V_W0, _V_A0, _V_KK, _V_KA = 0, 6, 7, 8, 9


def _head_sum(x, j_ref):
    parts = [_dot(x[:, s:s + SLAB], j_ref[...]) for s in range(0, x.shape[1], SLAB)]
    return jnp.concatenate(parts, axis=1)


def _rw_proj_kernel(x_ref, xp_ref, vec_ref, wr_ref, wk_ref, wv_ref, w1_ref, w2_ref, a1_ref, a2_ref,
                    g1_ref, g2_ref, j_ref, r_ref, k_ref, v_ref, lw_ref, an_ref, bn_ref, g_ref):
    x = x_ref[0]
    tm = x.shape[0]
    row = lax.broadcasted_iota(jnp.int32, (tm, 1), 0)
    prev_last = jnp.where(pl.program_id(1) == 0, 0.0, xp_ref[0][SUBLANES - 1:SUBLANES, :])
    xprev = jnp.where(row == 0, prev_last, pltpu.roll(x, 1, axis=0))
    xx = xprev - x

    def mix(i):
        return x + xx * vec_ref[_V_MU + i:_V_MU + i + 1, :]

    r = _dot(mix(0), wr_ref[...])
    w_pre = vec_ref[_V_W0:_V_W0 + 1, :] + _dot(jnp.tanh(_dot(mix(1), w1_ref[...])), w2_ref[...])
    k = _dot(mix(2), wk_ref[...])
    v = _dot(mix(3), wv_ref[...])
    a = jax.nn.sigmoid(vec_ref[_V_A0:_V_A0 + 1, :] + _dot(_dot(mix(4), a1_ref[...]), a2_ref[...]))
    g = _dot(jax.nn.sigmoid(_dot(mix(5), g1_ref[...])), g2_ref[...])

    kk = k * vec_ref[_V_KK:_V_KK + 1, :]
    kk = kk * lax.rsqrt(jnp.maximum(_head_sum(kk * kk, j_ref), 1e-24))
    r_ref[0] = r.astype(r_ref.dtype)
    k_ref[0] = (k * (1.0 + (a - 1.0) * vec_ref[_V_KA:_V_KA + 1, :])).astype(k_ref.dtype)
    v_ref[0] = v.astype(v_ref.dtype)
    lw_ref[0] = (-math.exp(-0.5)) * jax.nn.sigmoid(w_pre)
    an_ref[0] = (-kk).astype(an_ref.dtype)
    bn_ref[0] = (kk * a).astype(bn_ref.dtype)
    g_ref[0] = g.astype(g_ref.dtype)


def _rw_proj(x, vecs, wr, wk, wv, w1, w2, a1, a2, g1, g2, jmat, *, tm=512):
    bsz, seq, c = x.shape
    tm = min(tm, seq)
    const = lambda b, i: (0, 0)
    tile = pl.BlockSpec((1, tm, c), lambda b, i: (b, i, 0))
    prev = pl.BlockSpec((1, SUBLANES, c), lambda b, i: (b, jnp.maximum(i * (tm // SUBLANES) - 1, 0), 0))

    def full(w):
        return pl.BlockSpec(w.shape, const)

    ws = [w.astype(BF16) for w in (wr, wk, wv, w1, w2, a1, a2, g1, g2)]
    out = lambda dt: jax.ShapeDtypeStruct((bsz, seq, c), dt)
    return pl.pallas_call(
        _rw_proj_kernel,
        grid=(bsz, seq // tm),
        in_specs=[tile, prev, full(vecs)] + [full(w) for w in ws] + [full(jmat)],
        out_specs=[tile] * 7,
        out_shape=[out(BF16), out(BF16), out(BF16), out(F32), out(BF16), out(BF16), out(BF16)],
        compiler_params=_cparams(("parallel", "arbitrary")),
        name="rwkv_proj",
    )(x, x, vecs, *ws, jmat)


def _rw_scan_kernel(r_ref, k_ref, v_ref, lw_ref, an_ref, bn_ref, vec_ref, y_ref, state_ref):
    nb, tm, c = r_ref.shape
    n_pairs = c // PAIR
    L = RW_CHUNK

    @pl.when(pl.program_id(1) == 0)
    def _():
        state_ref[...] = jnp.zeros_like(state_ref)

    sub3 = lax.broadcasted_iota(jnp.int32, (1, SUBLANES, 1), 1)

    def chunk_cumsum(x):
        x3 = x.reshape(L // SUBLANES, SUBLANES, x.shape[1])
        for d in _SUBLANE_STEPS:
            x3 = x3 + jnp.where(sub3 >= d, pltpu.roll(x3, d, axis=1), 0.0)
        blocks, run = [], None
        for kb in range(L // SUBLANES):
            blk = x3[kb] if run is None else x3[kb] + run
            blocks.append(blk)
            run = blk[SUBLANES - 1:SUBLANES, :]
        return jnp.concatenate(blocks, axis=0)

    ri = lax.broadcasted_iota(jnp.int32, (PAIR, PAIR), 0)
    ci = lax.broadcasted_iota(jnp.int32, (PAIR, PAIR), 1)
    same_head = (ri // L) == (ci // L)
    mask_strict = same_head & ((ci % L) < (ri % L))
    mask_incl = same_head & ((ci % L) <= (ri % L))
    eye = (ri == ci).astype(F32)
    lane = lax.broadcasted_iota(jnp.int32, (L, PAIR), 1)
    head0 = lane < HEAD
    ri_h = lax.broadcasted_iota(jnp.int32, (PAIR // 2, PAIR), 0)
    ci_h = lax.broadcasted_iota(jnp.int32, (PAIR // 2, PAIR), 1)

    def level_mask(dd):
        return same_head & ((ri // (2 * dd)) == (ci // (2 * dd))) & ((ri % (2 * dd)) >= dd) & ((ci % (2 * dd)) < dd)

    def sel(res):
        return jnp.where(head0, res[:L], res[L:])

    def dup(x):
        return jnp.concatenate([x, x], axis=0)

    def head_sum(x):
        s0 = jnp.sum(jnp.where(head0, x, 0.0), axis=1, keepdims=True)
        s1 = jnp.sum(jnp.where(head0, 0.0, x), axis=1, keepdims=True)
        return jnp.where(head0, s0, s1)

    def take_rows(x, dd, half):
        return jnp.concatenate([x[(2 * b + half) * dd:(2 * b + half + 1) * dd]
                                for b in range(PAIR // (2 * dd))], axis=0)

    def merge_rows(first, second, dd):
        parts = []
        for b in range(PAIR // (2 * dd)):
            parts += [first[b * dd:(b + 1) * dd], second[b * dd:(b + 1) * dd]]
        return jnp.concatenate(parts, axis=0)

    n_chunks = tm // L
    per_chunk = nb * n_pairs
    probs = range(n_chunks * per_chunk)
    rows = [slice((p // per_chunk) * L, (p // per_chunk + 1) * L) for p in probs]
    seq = [(p % per_chunk) // n_pairs for p in probs]
    lanes = [slice((p % n_pairs) * PAIR, (p % n_pairs + 1) * PAIR) for p in probs]

    def tile(ref, p):
        return ref[seq[p], rows[p], lanes[p]]

    rhs, ar, bk, vv, e_l = [], [], [], [], []
    for p in probs:
        r = tile(r_ref, p).astype(F32)
        k = tile(k_ref, p).astype(F32)
        a = tile(an_ref, p).astype(F32)
        b = tile(bn_ref, p).astype(F32)
        lw = tile(lw_ref, p)
        cs_c = chunk_cumsum(lw)
        cs_last = cs_c[L - 1:L, :]
        a_t = a * jnp.exp(cs_c - lw)
        r_t = r * jnp.exp(cs_c)
        e_m = jnp.exp(-cs_c)
        b_t = b * e_m
        k_t = k * e_m
        rhs.append(jnp.concatenate([jnp.where(head0, b_t, 0.0), jnp.where(head0, 0.0, b_t),
                                    jnp.where(head0, k_t, 0.0), jnp.where(head0, 0.0, k_t)], axis=0).astype(BF16))
        ar.append(jnp.concatenate([a_t, r_t], axis=0).astype(BF16))
        bk.append(jnp.concatenate([b_t, k_t], axis=0).astype(BF16))
        vv.append(dup(tile(v_ref, p)))
        e_l.append(jnp.exp(cs_last))

    sc = [_dot_t(ar[p], rhs[p]) for p in probs]
    n_ab = [jnp.where(mask_strict, dup(sc[p][:L, :PAIR]), 0.0) for p in probs]
    n_ak = [jnp.where(mask_strict, dup(sc[p][:L, PAIR:]), 0.0).astype(BF16) for p in probs]
    n_r = [jnp.concatenate([jnp.where(mask_incl, dup(sc[p][L:, :PAIR]), 0.0),
                            jnp.where(mask_incl, dup(sc[p][L:, PAIR:]), 0.0)], axis=1).astype(BF16) for p in probs]
    w0 = [sel(_dot(n_ak[p], vv[p])) for p in probs]

    t_inv = [jnp.where(level_mask(1), n_ab[p], eye) for p in probs]
    n_ab_b = [n_ab[p].astype(BF16) for p in probs]
    dd = 2
    while dd < SUBLANES:
        lm = level_mask(dd)
        nt = [jnp.where(lm, _dot(n_ab_b[p], t_inv[p]), 0.0) for p in probs]
        t_inv = [t_inv[p] + _dot(t_inv[p], nt[p]) for p in probs]
        dd *= 2
    zero_half = jnp.zeros((PAIR // 2, PAIR), F32)
    while dd < L:
        lm_hi = ((ri_h // dd) == (ci_h // (2 * dd))) & ((ci_h % (2 * dd)) < dd)
        nt = [merge_rows(zero_half, jnp.where(lm_hi, _dot(take_rows(n_ab[p], dd, 1), t_inv[p]), 0.0), dd)
              for p in probs]
        t_hi = [take_rows(t_inv[p], dd, 1) for p in probs]
        t_inv = [merge_rows(take_rows(t_inv[p], dd, 0), t_hi[p] + _dot(t_hi[p], nt[p]), dd) for p in probs]
        dd *= 2
    t_inv = [t_inv[p].astype(BF16) for p in probs]

    for ch in range(n_chunks):
        cur = range(ch * per_chunk, (ch + 1) * per_chunk)
        slot = {p: p - ch * per_chunk for p in cur}
        s_prev = {p: state_ref[slot[p]] for p in cur}
        ars = {p: _dot_t(ar[p], s_prev[p]) for p in cur}
        u = {p: sel(_dot(t_inv[p], dup(ars[p][:L] + w0[p]))) for p in cur}
        o = {p: ars[p][L:] + sel(_dot(n_r[p], jnp.concatenate([dup(u[p]).astype(BF16), vv[p]], axis=0))) for p in cur}
        for p in cur:
            upd = _tdot(jnp.concatenate([u[p].astype(BF16), vv[p][:L]], axis=0), bk[p])
            state_ref[slot[p]] = (s_prev[p] + jnp.where(same_head, upd, 0.0)) * e_l[p]
        for p in cur:
            dlt = o[p] - head_sum(o[p]) * (1.0 / HEAD)
            var = head_sum(dlt * dlt) * (1.0 / HEAD)
            rk = tile(r_ref, p).astype(F32) * tile(k_ref, p).astype(F32)
            bonus = head_sum(rk * vec_ref[0:1, lanes[p]]) * tile(v_ref, p).astype(F32)
            y = dlt * lax.rsqrt(var + GN_EPS) * vec_ref[1:2, lanes[p]] + vec_ref[2:3, lanes[p]] + bonus
            y_ref[seq[p], rows[p], lanes[p]] = y.astype(y_ref.dtype)


def _rw_scan(r, k, v, lw, an, bn, vecs, *, tm=128, nb=4):
    bsz, seq, c = r.shape
    tm = min(tm, seq)
    nb = math.gcd(nb, bsz)
    tile = pl.BlockSpec((nb, tm, c), lambda b, i: (b, i, 0))
    const = lambda b, i: (0, 0)
    return pl.pallas_call(
        _rw_scan_kernel,
        grid=(bsz // nb, seq // tm),
        in_specs=[tile] * 6 + [pl.BlockSpec(vecs.shape, const)],
        out_specs=tile,
        out_shape=jax.ShapeDtypeStruct((bsz, seq, c), BF16),
        scratch_shapes=[pltpu.VMEM((nb * (c // PAIR), PAIR, PAIR), F32)],
        compiler_params=_cparams(("parallel", "arbitrary")),
        name="rwkv_scan",
    )(r, k, v, lw, an, bn, vecs)


def _s5_tables(a_re, a_im, log_dt, b_re, b_im, c_re, c_im):
    g, p = a_re.shape
    s = b_re.shape[-1]
    L = S5_CHUNK
    dt = jnp.exp(log_dt.astype(F32))[:, None]
    lam_re = jnp.minimum(a_re.astype(F32), -1e-4)
    lam_im = a_im.astype(F32)

    def apow(n):
        n = jnp.asarray(n, F32)[:, None, None]
        mag = jnp.exp(n * dt * lam_re)
        return mag * jnp.cos(n * dt * lam_im), mag * jnp.sin(n * dt * lam_im)

    ab_re, ab_im = apow([1])
    ab_re, ab_im = ab_re[0], ab_im[0]
    den = lam_re * lam_re + lam_im * lam_im
    nr, ni = ab_re - 1.0, ab_im
    coef_re = (nr * lam_re + ni * lam_im) / den
    coef_im = (ni * lam_re - nr * lam_im) / den
    bb_re = coef_re[..., None] * b_re - coef_im[..., None] * b_im
    bb_im = coef_re[..., None] * b_im + coef_im[..., None] * b_re
    cr, ci = c_re.astype(F32), c_im.astype(F32)

    pw_re, pw_im = apow(jnp.arange(L))
    cb_re = (jnp.einsum('gcp,tgp,gpd->gtcd', cr, pw_re, bb_re) - jnp.einsum('gcp,tgp,gpd->gtcd', cr, pw_im, bb_im)
             - jnp.einsum('gcp,tgp,gpd->gtcd', ci, pw_re, bb_im) - jnp.einsum('gcp,tgp,gpd->gtcd', ci, pw_im, bb_re))

    gs = SLAB // s
    n_slab = g // gs
    m_lag = cb_re.reshape(n_slab, gs, L, s, s).transpose(0, 2, 4, 1, 3).reshape(n_slab, L, s, SLAB)

    gt = LANES // s
    n_t = g // gt
    st = 2 * gt * p
    rv_re, rv_im = pw_re[::-1], pw_im[::-1]
    in_re = rv_re[..., None] * bb_re[None] - rv_im[..., None] * bb_im[None]
    in_im = rv_re[..., None] * bb_im[None] + rv_im[..., None] * bb_re[None]
    m_in = jnp.stack([in_re, in_im], axis=0).reshape(2, L, n_t, gt, p, s)
    m_in = m_in.transpose(2, 1, 5, 0, 3, 4).reshape(n_t, L, s, st)
    p1_re, p1_im = apow(jnp.arange(1, L + 1))
    out_re = cr[None] * p1_re[:, :, None, :] - ci[None] * p1_im[:, :, None, :]
    out_im = cr[None] * p1_im[:, :, None, :] + ci[None] * p1_re[:, :, None, :]
    m_out = jnp.stack([out_re, -out_im], axis=0).reshape(2, L, n_t, gt, s, p)
    m_out = m_out.transpose(2, 0, 5, 1, 3, 4).reshape(n_t, 2, p, L * LANES)

    def lay_vec(re, im):
        n = re.shape[0]
        f = lambda m: m.reshape(n, n_t, gt * p).transpose(1, 0, 2)
        return jnp.concatenate([f(re), f(im)], axis=-1)

    step_pow = lay_vec(*apow([L * d for d in _SUBLANE_STEPS]))
    carry_pow = lay_vec(*apow(L * jnp.arange(1, SUBLANES + 1)))
    return m_lag, m_in, m_out, step_pow, carry_pow


def _cmul(ar, ai, br, bi):
    return ar * br - ai * bi, ar * bi + ai * br


def _s5_kernel(u_ref, mlag_ref, min_ref, mout_ref, spow_ref, cpow_ref, dskip_ref, y_ref,
               carry_ref, ut_ref, zt_ref, wlag_ref, bcat_ref, ccat_ref):
    tm = u_ref.shape[1]
    L = S5_CHUNK
    assert L == SUBLANES
    n_rows = tm // L
    half = bcat_ref.shape[2] // 2
    tiles = range(SLAB // LANES)

    @pl.when(pl.program_id(2) == 0)
    def _():
        carry_ref[...] = jnp.zeros_like(carry_ref)

    @pl.when((pl.program_id(1) == 0) & (pl.program_id(2) == 0))
    def _():
        def expand(dst, table, n_groups, col_group_size):
            rows, cols = table.shape
            col_group = (lax.broadcasted_iota(jnp.int32, (1, cols), 1) // col_group_size) % n_groups
            for grp in range(n_groups):
                dst[pl.ds(grp * rows, rows), :] = jnp.where(col_group == grp, table, 0.0).astype(dst.dtype)

        s_ch, p = min_ref.shape[2], mout_ref.shape[2]
        for tau in range(L):
            n_grp = SLAB // s_ch
            expand(wlag_ref.at[pl.ds(tau * SLAB, SLAB), :], mlag_ref[0, tau], n_grp, s_ch)
        for t in tiles:
            n_grp = LANES // s_ch
            for s in range(L):
                expand(bcat_ref.at[t, pl.ds(s * LANES, LANES), :], min_ref[t, s], n_grp, p)
            for ri in range(2):
                expand(ccat_ref.at[t, pl.ds(ri * n_grp * p, n_grp * p), :], mout_ref[t, ri], n_grp, s_ch)

    u = u_ref[0]
    sub3 = lax.broadcasted_iota(jnp.int32, (1, SUBLANES, 1), 1)
    sub2 = lax.broadcasted_iota(jnp.int32, (SUBLANES, 1), 0)
    n_blk = n_rows // SUBLANES

    x = []
    for t in tiles:
        ut_ref[t] = u[:, t * LANES:(t + 1) * LANES]
        ucat = jnp.concatenate([ut_ref[t, pl.ds(s, n_rows, stride=L), :] for s in range(L)], axis=1)
        x.append(_dot(ucat, bcat_ref[t]))

    u3 = u.reshape(tm // SUBLANES, SUBLANES, SLAB)
    shifted = [u.astype(BF16)]
    for tau in range(1, L):
        sh = jnp.where(sub3 >= tau, pltpu.roll(u3, tau, axis=1), 0.0)
        shifted.append(sh.reshape(tm, SLAB).astype(BF16))
    y_lag = jnp.dot(jnp.concatenate(shifted, axis=1), wlag_ref[...], preferred_element_type=F32)

    for t in tiles:
        hr = x[t][:, :half].reshape(n_blk, SUBLANES, half)
        hi = x[t][:, half:].reshape(n_blk, SUBLANES, half)
        for i, d in enumerate(_SUBLANE_STEPS):
            pr, pi = spow_ref[t, i:i + 1, :half], spow_ref[t, i:i + 1, half:]
            sr, si = _cmul(pr, pi, pltpu.roll(hr, d, axis=1), pltpu.roll(hi, d, axis=1))
            hr = hr + jnp.where(sub3 >= d, sr, 0.0)
            hi = hi + jnp.where(sub3 >= d, si, 0.0)
        cr, ci = carry_ref[t, :, :half], carry_ref[t, :, half:]
        cp_r, cp_i = cpow_ref[t, :, :half], cpow_ref[t, :, half:]
        prev_r, prev_i = [], []
        for kb in range(n_blk):
            kr, ki = _cmul(cp_r, cp_i, cr, ci)
            br, bi = hr[kb] + kr, hi[kb] + ki
            prev_r.append(jnp.where(sub2 == 0, cr, pltpu.roll(br, 1, axis=0)))
            prev_i.append(jnp.where(sub2 == 0, ci, pltpu.roll(bi, 1, axis=0)))
            cr, ci = br[SUBLANES - 1:SUBLANES, :], bi[SUBLANES - 1:SUBLANES, :]
        carry_ref[t, :, :half] = cr
        carry_ref[t, :, half:] = ci
        prev = jnp.concatenate([jnp.concatenate(prev_r, axis=0), jnp.concatenate(prev_i, axis=0)], axis=1)
        z = _dot(prev, ccat_ref[t])
        for s in range(L):
            zt_ref[t, pl.ds(s, n_rows, stride=L), :] = z[:, s * LANES:(s + 1) * LANES]

    y_state = jnp.concatenate([zt_ref[t] for t in tiles], axis=1)
    y_ref[0] = (y_lag + y_state + u * dskip_ref[...]).astype(y_ref.dtype)


def _s5_conv(h, tables, d_skip, *, tm=2048):
    bsz, seq, c = h.shape
    tm = min(tm, seq)
    m_lag, m_in, m_out, step_pow, carry_pow = tables
    n_slab = c // SLAB
    tps = SLAB // LANES
    st = m_in.shape[-1]
    chunk_cols = m_out.shape[-1]
    return pl.pallas_call(
        _s5_kernel,
        grid=(n_slab, bsz, seq // tm),
        in_specs=[
            pl.BlockSpec((1, tm, SLAB), lambda s, b, i: (b, i, s)),
            pl.BlockSpec((1,) + m_lag.shape[1:], lambda s, b, i: (s, 0, 0, 0)),
            pl.BlockSpec((tps,) + m_in.shape[1:], lambda s, b, i: (s, 0, 0, 0)),
            pl.BlockSpec((tps,) + m_out.shape[1:], lambda s, b, i: (s, 0, 0, 0)),
            pl.BlockSpec((tps,) + step_pow.shape[1:], lambda s, b, i: (s, 0, 0)),
            pl.BlockSpec((tps,) + carry_pow.shape[1:], lambda s, b, i: (s, 0, 0)),
            pl.BlockSpec((1, SLAB), lambda s, b, i: (0, s)),
        ],
        out_specs=pl.BlockSpec((1, tm, SLAB), lambda s, b, i: (b, i, s)),
        out_shape=jax.ShapeDtypeStruct((bsz, seq, c), BF16),
        scratch_shapes=[pltpu.VMEM((tps, 1, st), F32), pltpu.VMEM((tps, tm, LANES), F32),
                        pltpu.VMEM((tps, tm, LANES), F32),
                        pltpu.VMEM((S5_CHUNK * SLAB, SLAB), BF16),
                        pltpu.VMEM((tps, chunk_cols, st), BF16),
                        pltpu.VMEM((tps, st, chunk_cols), BF16)],
        compiler_params=_cparams(("arbitrary", "arbitrary", "arbitrary")),
        name="s5_conv",
    )(h, m_lag, m_in, m_out, step_pow, carry_pow, d_skip.reshape(1, c))


def _rwkv_block(h, ln0, ln1, mu, w0, w1, w2, a0, a1, a2, g1, g2, k_k, k_a, r_k, wr, wk, wv, wo,
                lnx_g, lnx_b, mlp_w1, mlp_w2, layer):
    bsz, seq, c = h.shape
    zeros = jnp.zeros((6, c), F32)
    vecs = jnp.concatenate([mu, w0[None], a0[None], k_k[None], k_a[None], zeros], axis=0)
    scan_vecs = jnp.concatenate([r_k.reshape(1, c), lnx_g[None], lnx_b[None], zeros[:5]], axis=0)
    idx = jnp.arange(SLAB) // HEAD
    jmat = (idx[:, None] == idx[None, :]).astype(BF16)
    r, k, v, lw, an, bn, gate = _rw_proj(h, vecs, wr, wk, wv, w1, w2, a1, a2, g1, g2, jmat)
    y = _rw_scan(r, k, v, lw, an, bn, scan_vecs)
    n = bsz * seq
    out = _mixer_out_mlp(_gate_mlp_kernel, "rwkv_out_mlp", [h.reshape(n, c), y.reshape(n, c), gate.reshape(n, c)],
                         wo, ln0, mlp_w1, mlp_w2, layer, ln1)
    return out.reshape(bsz, seq, c)


def _s5_block(h, ln0, ln1, a_re, a_im, log_dt, b_re, b_im, c_re, c_im, d_skip, w_glu, mlp_w1, mlp_w2, layer):
    bsz, seq, c = h.shape
    tables = _s5_tables(a_re, a_im, log_dt, b_re, b_im, c_re, c_im)
    y = _s5_conv(h, tables, d_skip)
    n = bsz * seq
    out = _mixer_out_mlp(_glu_mlp_kernel, "s5_out_mlp", [h.reshape(n, c), y.reshape(n, c)],
                         w_glu, ln0, mlp_w1, mlp_w2, layer, ln1)
    return out.reshape(bsz, seq, c)


def kernel(x, ln_g, ln_b, rw_mu, rw_w0, rw_w1, rw_w2, rw_a0, rw_a1, rw_a2, rw_g1, rw_g2, rw_k_k, rw_k_a, rw_r_k, rw_wr, rw_wk, rw_wv, rw_wo, rw_lnx_g, rw_lnx_b, s5_a_re, s5_a_im, s5_log_dt, s5_b_re, s5_b_im, s5_c_re, s5_c_im, s5_d, s5_w_glu, mlp_w1, mlp_w2):
    w1b, w2b = mlp_w1.astype(BF16), mlp_w2
    h = _rwkv_block(x, (ln_g[0], ln_b[0]), (ln_g[1], ln_b[1]), rw_mu[0], rw_w0[0], rw_w1[0], rw_w2[0],
                    rw_a0[0], rw_a1[0], rw_a2[0], rw_g1[0], rw_g2[0], rw_k_k[0], rw_k_a[0], rw_r_k[0],
                    rw_wr[0], rw_wk[0], rw_wv[0], rw_wo[0], rw_lnx_g[0], rw_lnx_b[0], w1b, w2b, 0)
    h = _s5_block(h, (ln_g[2], ln_b[2]), (ln_g[3], ln_b[3]), s5_a_re[0], s5_a_im[0], s5_log_dt[0],
                  s5_b_re[0], s5_b_im[0], s5_c_re[0], s5_c_im[0], s5_d[0], s5_w_glu[0], w1b, w2b, 1)
    return h
```

```python
import functools
import math

import jax
import jax.numpy as jnp
from jax import lax
from jax.experimental import pallas as pl
from jax.experimental.pallas import tpu as pltpu

F32 = jnp.float32
BF16 = jnp.bfloat16

DEPTH = 2
DN_ALPHA = (2.0 * DEPTH) ** 0.25
LN_EPS = 1e-5
GN_EPS = 64e-5

LANES = 128
SUBLANES = 8
_SUBLANE_STEPS = (1, 2, 4)
HEAD = 64
PAIR = 2 * HEAD
SLAB = 256
RW_CHUNK = 64
S5_CHUNK = 8
V7X_VMEM_BYTES = 64 * 1024 * 1024
VMEM_LIMIT = V7X_VMEM_BYTES * 7 // 8


def _cparams(sem):
    return pltpu.CompilerParams(dimension_semantics=sem, vmem_limit_bytes=VMEM_LIMIT)


def _layer_norm(z, g, b):
    mu = jnp.mean(z, axis=-1, keepdims=True)
    d = z - mu
    var = jnp.mean(d * d, axis=-1, keepdims=True)
    return d * lax.rsqrt(var + LN_EPS) * g + b


def _dot(a, b):
    return jnp.dot(a.astype(BF16), b.astype(BF16), preferred_element_type=F32)


def _dot_t(a, b):
    return lax.dot_general(a.astype(BF16), b.astype(BF16), (((1,), (1,)), ((), ())),
                           preferred_element_type=F32)


def _tdot(a, b):
    return lax.dot_general(a.astype(BF16), b.astype(BF16), (((0,), (0,)), ((), ())),
                           preferred_element_type=F32)


MLP_FF_CHUNK = 1024


ROW_SPLIT = 2


def _mixer_out_mlp_body(mix_fn, x_ref, g0_ref, b0_ref, w1_ref, w2_ref, g1_ref, b1_ref, o_ref):
    tm = x_ref.shape[0]
    d_ff = w1_ref.shape[1]
    chunk = min(MLP_FF_CHUNK, d_ff)
    groups = [slice(i * tm // ROW_SPLIT, (i + 1) * tm // ROW_SPLIT) for i in range(ROW_SPLIT)]
    mix = [mix_fn(rows) for rows in groups]
    h = [_layer_norm(DN_ALPHA * x_ref[rows, :] + m, g0_ref[...], b0_ref[...]) for rows, m in zip(groups, mix)]
    hb = [v.astype(BF16) for v in h]
    acc = [jnp.zeros(v.shape, F32) for v in h]
    for c in range(d_ff // chunk):
        up = [jnp.dot(v, w1_ref[:, c * chunk:(c + 1) * chunk], preferred_element_type=F32) for v in hb]
        act = [jnp.square(jnp.maximum(u, 0.0)).astype(BF16) for u in up]
        w2c = w2_ref[c * chunk:(c + 1) * chunk, :].astype(BF16)
        acc = [s + jnp.dot(a, w2c, preferred_element_type=F32) for s, a in zip(acc, act)]
    for rows, v, s in zip(groups, h, acc):
        o_ref[rows, :] = _layer_norm(DN_ALPHA * v + s, g1_ref[...], b1_ref[...])


def _gate_mlp_kernel(x_ref, y_ref, gate_ref, wo_ref, g0_ref, b0_ref, w1_ref, w2_ref, g1_ref, b1_ref, o_ref):
    def mix(rows):
        return _dot(y_ref[rows, :].astype(F32) * gate_ref[rows, :].astype(F32), wo_ref[...])

    _mixer_out_mlp_body(mix, x_ref, g0_ref, b0_ref, w1_ref, w2_ref, g1_ref, b1_ref, o_ref)


def _glu_mlp_kernel(x_ref, y_ref, wglu_ref, g0_ref, b0_ref, w1_ref, w2_ref, g1_ref, b1_ref, o_ref):
    c = x_ref.shape[1]

    def mix(rows):
        y = y_ref[rows, :].astype(F32)
        y = 0.5 * y * (1.0 + jnp.tanh(math.sqrt(2.0 / math.pi) * (y + 0.044715 * (y * y * y))))
        z = _dot(y, wglu_ref[...])
        return z[:, :c] * jax.nn.sigmoid(z[:, c:])

    _mixer_out_mlp_body(mix, x_ref, g0_ref, b0_ref, w1_ref, w2_ref, g1_ref, b1_ref, o_ref)


def _mixer_out_mlp(body, name, row_inputs, w_mix, ln0, mlp_w1, mlp_w2, layer, ln1, *, tm=512):
    n, c = row_inputs[0].shape
    tm = min(tm, n)
    tile = pl.BlockSpec((tm, c), lambda i: (i, 0))
    const = lambda i: (0, 0)
    vec = pl.BlockSpec((1, c), const)

    def stacked(w):
        return pl.BlockSpec((None,) + w.shape[1:], lambda i: (layer, 0, 0), pipeline_mode=pl.Buffered(1))

    return pl.pallas_call(
        body,
        grid=(n // tm,),
        in_specs=[tile] * len(row_inputs) + [pl.BlockSpec(w_mix.shape, const, pipeline_mode=pl.Buffered(1)),
                                             vec, vec, stacked(mlp_w1), stacked(mlp_w2), vec, vec],
        out_specs=tile,
        out_shape=jax.ShapeDtypeStruct((n, c), F32),
        compiler_params=_cparams(("parallel",)),
        name=name,
    )(*row_inputs, w_mix, ln0[0].reshape(1, c), ln0[1].reshape(1, c), mlp_w1, mlp_w2,
      ln1[0].reshape(1, c), ln1[1].reshape(1, c))


_V_MU, _V_W0, _V_A0, _V_KK, _V_KA = 0, 6, 7, 8, 9


def _head_sum(x, j_ref):
    parts = [_dot(x[:, s:s + SLAB], j_ref[...]) for s in range(0, x.shape[1], SLAB)]
    return jnp.concatenate(parts, axis=1)


def _rw_proj_kernel(x_ref, xp_ref, vec_ref, wr_ref, wk_ref, wv_ref, w1_ref, w2_ref, a1_ref, a2_ref,
                    g1_ref, g2_ref, j_ref, r_ref, k_ref, v_ref, lw_ref, an_ref, bn_ref, g_ref, xx_ref):
    x = x_ref[0]
    tm = x.shape[0]
    row = lax.broadcasted_iota(jnp.int32, (tm, 1), 0)
    prev_last = jnp.where(pl.program_id(1) == 0, 0.0, xp_ref[0][SUBLANES - 1:SUBLANES, :])
    xprev = jnp.where(row == 0, prev_last, pltpu.roll(x, 1, axis=0))
    xx_ref[...] = xprev - x

    def mix(i):
        return x_ref[0] + xx_ref[...] * vec_ref[_V_MU + i:_V_MU + i + 1, :]

    r = _dot(mix(0), wr_ref[...])
    w_pre = vec_ref[_V_W0:_V_W0 + 1, :] + _dot(jnp.tanh(_dot(mix(1), w1_ref[...])), w2_ref[...])
    k = _dot(mix(2), wk_ref[...])
    v = _dot(mix(3), wv_ref[...])
    a = jax.nn.sigmoid(vec_ref[_V_A0:_V_A0 + 1, :] + _dot(_dot(mix(4), a1_ref[...]), a2_ref[...]))
    g = _dot(jax.nn.sigmoid(_dot(mix(5), g1_ref[...])), g2_ref[...])

    kk = k * vec_ref[_V_KK:_V_KK + 1, :]
    kk = kk * lax.rsqrt(jnp.maximum(_head_sum(kk * kk, j_ref), 1e-24))
    r_ref[0] = r.astype(r_ref.dtype)
    k_ref[0] = (k * (1.0 + (a - 1.0) * vec_ref[_V_KA:_V_KA + 1, :])).astype(k_ref.dtype)
    v_ref[0] = v.astype(v_ref.dtype)
    lw_ref[0] = (-math.exp(-0.5)) * jax.nn.sigmoid(w_pre)
    an_ref[0] = (-kk).astype(an_ref.dtype)
    bn_ref[0] = (kk * a).astype(bn_ref.dtype)
    g_ref[0] = g.astype(g_ref.dtype)


def _rw_proj(x, vecs, wr, wk, wv, w1, w2, a1, a2, g1, g2, jmat, *, tm=512):
    bsz, seq, c = x.shape
    tm = min(tm, seq)
    const = lambda b, i: (0, 0)
    tile = pl.BlockSpec((1, tm, c), lambda b, i: (b, i, 0))
    prev = pl.BlockSpec((1, SUBLANES, c), lambda b, i: (b, jnp.maximum(i * (tm // SUBLANES) - 1, 0), 0))

    def full(w):
        return pl.BlockSpec(w.shape, const)

    ws = [w.astype(BF16) for w in (wr, wk, wv, w1, w2, a1, a2, g1, g2)]
    out = lambda dt: jax.ShapeDtypeStruct((bsz, seq, c), dt)
    return pl.pallas_call(
        _rw_proj_kernel,
        grid=(bsz, seq // tm),
        in_specs=[tile, prev, full(vecs)] + [full(w) for w in ws] + [full(jmat)],
        out_specs=[tile] * 7,
        out_shape=[out(BF16), out(BF16), out(BF16), out(F32), out(BF16), out(BF16), out(BF16)],
        scratch_shapes=[pltpu.VMEM((tm, c), F32)],
        compiler_params=_cparams(("parallel", "arbitrary")),
        name="rwkv_proj",
    )(x, x, vecs, *ws, jmat)


def _rw_scan_kernel(r_ref, k_ref, v_ref, lw_ref, an_ref, bn_ref, vec_ref, y_ref, state_ref):
    nb, tm, c = r_ref.shape
    n_pairs = c // PAIR
    L = RW_CHUNK

    @pl.when(pl.program_id(1) == 0)
    def _():
        state_ref[...] = jnp.zeros_like(state_ref)

    sub3 = lax.broadcasted_iota(jnp.int32, (1, SUBLANES, 1), 1)

    def chunk_cumsum(x):
        x3 = x.reshape(L // SUBLANES, SUBLANES, x.shape[1])
        for d in _SUBLANE_STEPS:
            x3 = x3 + jnp.where(sub3 >= d, pltpu.roll(x3, d, axis=1), 0.0)
        blocks, run = [], None
        for kb in range(L // SUBLANES):
            blk = x3[kb] if run is None else x3[kb] + run
            blocks.append(blk)
            run = blk[SUBLANES - 1:SUBLANES, :]
        return jnp.concatenate(blocks, axis=0)

    ri = lax.broadcasted_iota(jnp.int32, (PAIR, PAIR), 0)
    ci = lax.broadcasted_iota(jnp.int32, (PAIR, PAIR), 1)
    same_head = (ri // L) == (ci // L)
    mask_strict = same_head & ((ci % L) < (ri % L))
    mask_incl = same_head & ((ci % L) <= (ri % L))
    eye = (ri == ci).astype(F32)
    lane = lax.broadcasted_iota(jnp.int32, (L, PAIR), 1)
    head0 = lane < HEAD
    ri_h = lax.broadcasted_iota(jnp.int32, (PAIR // 2, PAIR), 0)
    ci_h = lax.broadcasted_iota(jnp.int32, (PAIR // 2, PAIR), 1)

    def level_mask(dd):
        return same_head & ((ri // (2 * dd)) == (ci // (2 * dd))) & ((ri % (2 * dd)) >= dd) & ((ci % (2 * dd)) < dd)

    def sel(res):
        return jnp.where(head0, res[:L], res[L:])

    def dup(x):
        return jnp.concatenate([x, x], axis=0)

    def head_sum(x):
        s0 = jnp.sum(jnp.where(head0, x, 0.0), axis=1, keepdims=True)
        s1 = jnp.sum(jnp.where(head0, 0.0, x), axis=1, keepdims=True)
        return jnp.where(head0, s0, s1)

    def take_rows(x, dd, half):
        return jnp.concatenate([x[(2 * b + half) * dd:(2 * b + half + 1) * dd]
                                for b in range(PAIR // (2 * dd))], axis=0)

    def merge_rows(first, second, dd):
        parts = []
        for b in range(PAIR // (2 * dd)):
            parts += [first[b * dd:(b + 1) * dd], second[b * dd:(b + 1) * dd]]
        return jnp.concatenate(parts, axis=0)

    n_chunks = tm // L
    per_chunk = nb * n_pairs
    probs = range(n_chunks * per_chunk)
    rows = [slice((p // per_chunk) * L, (p // per_chunk + 1) * L) for p in probs]
    seq = [(p % per_chunk) // n_pairs for p in probs]
    lanes = [slice((p % n_pairs) * PAIR, (p % n_pairs + 1) * PAIR) for p in probs]

    def tile(ref, p):
        return ref[seq[p], rows[p], lanes[p]]

    rhs, ar, bk, vv, e_l = [], [], [], [], []
    for p in probs:
        r = tile(r_ref, p).astype(F32)
        k = tile(k_ref, p).astype(F32)
        a = tile(an_ref, p).astype(F32)
        b = tile(bn_ref, p).astype(F32)
        lw = tile(lw_ref, p)
        cs_c = chunk_cumsum(lw)
        cs_last = cs_c[L - 1:L, :]
        a_t = a * jnp.exp(cs_c - lw)
        r_t = r * jnp.exp(cs_c)
        e_m = jnp.exp(-cs_c)
        b_t = b * e_m
        k_t = k * e_m
        rhs.append(jnp.concatenate([jnp.where(head0, b_t, 0.0), jnp.where(head0, 0.0, b_t),
                                    jnp.where(head0, k_t, 0.0), jnp.where(head0, 0.0, k_t)], axis=0).astype(BF16))
        ar.append(jnp.concatenate([a_t, r_t], axis=0).astype(BF16))
        bk.append(jnp.concatenate([b_t, k_t], axis=0).astype(BF16))
        vv.append(dup(tile(v_ref, p)))
        e_l.append(jnp.exp(cs_last))

    sc = [_dot_t(ar[p], rhs[p]) for p in probs]
    n_ab = [jnp.where(mask_strict, dup(sc[p][:L, :PAIR]), 0.0) for p in probs]
    n_ak = [jnp.where(mask_strict, dup(sc[p][:L, PAIR:]), 0.0).astype(BF16) for p in probs]
    n_r = [jnp.concatenate([jnp.where(mask_incl, dup(sc[p][L:, :PAIR]), 0.0),
                            jnp.where(mask_incl, dup(sc[p][L:, PAIR:]), 0.0)], axis=1).astype(BF16) for p in probs]
    w0 = [sel(_dot(n_ak[p], vv[p])) for p in probs]

    t_inv = [jnp.where(level_mask(1), n_ab[p], eye) for p in probs]
    n_ab_b = [n_ab[p].astype(BF16) for p in probs]
    dd = 2
    while dd < SUBLANES:
        lm = level_mask(dd)
        nt = [jnp.where(lm, _dot(n_ab_b[p], t_inv[p]), 0.0) for p in probs]
        t_inv = [t_inv[p] + _dot(t_inv[p], nt[p]) for p in probs]
        dd *= 2
    zero_half = jnp.zeros((PAIR // 2, PAIR), F32)
    while dd < L:
        lm_hi = ((ri_h // dd) == (ci_h // (2 * dd))) & ((ci_h % (2 * dd)) < dd)
        nt = [merge_rows(zero_half, jnp.where(lm_hi, _dot(take_rows(n_ab[p], dd, 1), t_inv[p]), 0.0), dd)
              for p in probs]
        t_hi = [take_rows(t_inv[p], dd, 1) for p in probs]
        t_inv = [merge_rows(take_rows(t_inv[p], dd, 0), t_hi[p] + _dot(t_hi[p], nt[p]), dd) for p in probs]
        dd *= 2
    t_inv = [t_inv[p].astype(BF16) for p in probs]

    for ch in range(n_chunks):
        cur = range(ch * per_chunk, (ch + 1) * per_chunk)
        slot = {p: p - ch * per_chunk for p in cur}
        s_prev = {p: state_ref[slot[p]] for p in cur}
        ars = {p: _dot_t(ar[p], s_prev[p]) for p in cur}
        u = {p: sel(_dot(t_inv[p], dup(ars[p][:L] + w0[p]))) for p in cur}
        o = {p: ars[p][L:] + sel(_dot(n_r[p], jnp.concatenate([dup(u[p]).astype(BF16), vv[p]], axis=0))) for p in cur}
        for p in cur:
            upd = _tdot(jnp.concatenate([u[p].astype(BF16), vv[p][:L]], axis=0), bk[p])
            state_ref[slot[p]] = (s_prev[p] + jnp.where(same_head, upd, 0.0)) * e_l[p]
        for p in cur:
            dlt = o[p] - head_sum(o[p]) * (1.0 / HEAD)
            var = head_sum(dlt * dlt) * (1.0 / HEAD)
            rk = tile(r_ref, p).astype(F32) * tile(k_ref, p).astype(F32)
            bonus = head_sum(rk * vec_ref[0:1, lanes[p]]) * tile(v_ref, p).astype(F32)
            y = dlt * lax.rsqrt(var + GN_EPS) * vec_ref[1:2, lanes[p]] + vec_ref[2:3, lanes[p]] + bonus
            y_ref[seq[p], rows[p], lanes[p]] = y.astype(y_ref.dtype)


def _rw_scan(r, k, v, lw, an, bn, vecs, *, tm=128, nb=4):
    bsz, seq, c = r.shape
    tm = min(tm, seq)
    nb = math.gcd(nb, bsz)
    tile = pl.BlockSpec((nb, tm, c), lambda b, i: (b, i, 0))
    const = lambda b, i: (0, 0)
    return pl.pallas_call(
        _rw_scan_kernel,
        grid=(bsz // nb, seq // tm),
        in_specs=[tile] * 6 + [pl.BlockSpec(vecs.shape, const)],
        out_specs=tile,
        out_shape=jax.ShapeDtypeStruct((bsz, seq, c), BF16),
        scratch_shapes=[pltpu.VMEM((nb * (c // PAIR), PAIR, PAIR), F32)],
        compiler_params=_cparams(("parallel", "arbitrary")),
        name="rwkv_scan",
    )(r, k, v, lw, an, bn, vecs)


def _s5_tables(a_re, a_im, log_dt, b_re, b_im, c_re, c_im):
    g, p = a_re.shape
    s = b_re.shape[-1]
    L = S5_CHUNK
    dt = jnp.exp(log_dt.astype(F32))[:, None]
    lam_re = jnp.minimum(a_re.astype(F32), -1e-4)
    lam_im = a_im.astype(F32)

    def apow(n):
        n = jnp.asarray(n, F32)[:, None, None]
        mag = jnp.exp(n * dt * lam_re)
        return mag * jnp.cos(n * dt * lam_im), mag * jnp.sin(n * dt * lam_im)

    ab_re, ab_im = apow([1])
    ab_re, ab_im = ab_re[0], ab_im[0]
    den = lam_re * lam_re + lam_im * lam_im
    nr, ni = ab_re - 1.0, ab_im
    coef_re = (nr * lam_re + ni * lam_im) / den
    coef_im = (ni * lam_re - nr * lam_im) / den
    bb_re = coef_re[..., None] * b_re - coef_im[..., None] * b_im
    bb_im = coef_re[..., None] * b_im + coef_im[..., None] * b_re
    cr, ci = c_re.astype(F32), c_im.astype(F32)

    pw_re, pw_im = apow(jnp.arange(L))
    cb_re = (jnp.einsum('gcp,tgp,gpd->gtcd', cr, pw_re, bb_re) - jnp.einsum('gcp,tgp,gpd->gtcd', cr, pw_im, bb_im)
             - jnp.einsum('gcp,tgp,gpd->gtcd', ci, pw_re, bb_im) - jnp.einsum('gcp,tgp,gpd->gtcd', ci, pw_im, bb_re))

    gs = SLAB // s
    n_slab = g // gs
    m_lag = cb_re.reshape(n_slab, gs, L, s, s).transpose(0, 2, 4, 1, 3).reshape(n_slab, L, s, SLAB)

    gt = LANES // s
    n_t = g // gt
    st = 2 * gt * p
    rv_re, rv_im = pw_re[::-1], pw_im[::-1]
    in_re = rv_re[..., None] * bb_re[None] - rv_im[..., None] * bb_im[None]
    in_im = rv_re[..., None] * bb_im[None] + rv_im[..., None] * bb_re[None]
    m_in = jnp.stack([in_re, in_im], axis=0).reshape(2, L, n_t, gt, p, s)
    m_in = m_in.transpose(2, 1, 5, 0, 3, 4).reshape(n_t, L, s, st)
    p1_re, p1_im = apow(jnp.arange(1, L + 1))
    out_re = cr[None] * p1_re[:, :, None, :] - ci[None] * p1_im[:, :, None, :]
    out_im = cr[None] * p1_im[:, :, None, :] + ci[None] * p1_re[:, :, None, :]
    m_out = jnp.stack([out_re, -out_im], axis=0).reshape(2, L, n_t, gt, s, p)
    m_out = m_out.transpose(2, 0, 5, 1, 3, 4).reshape(n_t, 2, p, L * LANES)

    def lay_vec(re, im):
        n = re.shape[0]
        f = lambda m: m.reshape(n, n_t, gt * p).transpose(1, 0, 2)
        return jnp.concatenate([f(re), f(im)], axis=-1)

    step_pow = lay_vec(*apow([L * d for d in _SUBLANE_STEPS]))
    keep = (jnp.arange(SUBLANES)[None, :] >= jnp.asarray(_SUBLANE_STEPS)[:, None]).astype(F32)
    step_pow = (step_pow[:, :, None, :] * keep[None, :, :, None]).reshape(n_t, len(_SUBLANE_STEPS) * SUBLANES, st)
    carry_pow = lay_vec(*apow(L * jnp.arange(1, SUBLANES + 1)))
    return m_lag, m_in, m_out, step_pow, carry_pow


def _cmul(ar, ai, br, bi):
    return ar * br - ai * bi, ar * bi + ai * br


def _s5_kernel(u_ref, mlag_ref, min_ref, mout_ref, spow_ref, cpow_ref, dskip_ref, y_ref,
               carry_ref, ut_ref, zt_ref, wlag_ref, bcat_ref, ccat_ref):
    tm = u_ref.shape[1]
    L = S5_CHUNK
    assert L == SUBLANES
    n_rows = tm // L
    half = bcat_ref.shape[2] // 2
    tiles = range(SLAB // LANES)

    @pl.when(pl.program_id(2) == 0)
    def _():
        carry_ref[...] = jnp.zeros_like(carry_ref)

    @pl.when((pl.program_id(1) == 0) & (pl.program_id(2) == 0))
    def _():
        def expand(dst, table, n_groups, col_group_size):
            rows, cols = table.shape
            col_group = (lax.broadcasted_iota(jnp.int32, (1, cols), 1) // col_group_size) % n_groups
            for grp in range(n_groups):
                dst[pl.ds(grp * rows, rows), :] = jnp.where(col_group == grp, table, 0.0).astype(dst.dtype)

        s_ch, p = min_ref.shape[2], mout_ref.shape[2]
        for tau in range(L):
            n_grp = SLAB // s_ch
            expand(wlag_ref.at[pl.ds(tau * SLAB, SLAB), :], mlag_ref[0, tau], n_grp, s_ch)
        for t in tiles:
            n_grp = LANES // s_ch
            for s in range(L):
                expand(bcat_ref.at[t, pl.ds(s * LANES, LANES), :], min_ref[t, s], n_grp, p)
            for ri in range(2):
                expand(ccat_ref.at[t, pl.ds(ri * n_grp * p, n_grp * p), :], mout_ref[t, ri], n_grp, s_ch)

    u = u_ref[0]
    sub3 = lax.broadcasted_iota(jnp.int32, (1, SUBLANES, 1), 1)
    sub2 = lax.broadcasted_iota(jnp.int32, (SUBLANES, 1), 0)
    n_blk = n_rows // SUBLANES

    x = []
    for t in tiles:
        ut_ref[t] = u[:, t * LANES:(t + 1) * LANES]
        ucat = jnp.concatenate([ut_ref[t, pl.ds(s, n_rows, stride=L), :] for s in range(L)], axis=1)
        x.append(_dot(ucat, bcat_ref[t]))

    u3 = u.reshape(tm // SUBLANES, SUBLANES, SLAB)
    shifted = [u.astype(BF16)]
    for tau in range(1, L):
        sh = jnp.where(sub3 >= tau, pltpu.roll(u3, tau, axis=1), 0.0)
        shifted.append(sh.reshape(tm, SLAB).astype(BF16))
    y_lag = jnp.dot(jnp.concatenate(shifted, axis=1), wlag_ref[...], preferred_element_type=F32)

    for t in tiles:
        hr = x[t][:, :half].reshape(n_blk, SUBLANES, half)
        hi = x[t][:, half:].reshape(n_blk, SUBLANES, half)
        for i, d in enumerate(_SUBLANE_STEPS):
            step = pl.ds(i * SUBLANES, SUBLANES)
            sr, si = _cmul(spow_ref[t, step, :half][None], spow_ref[t, step, half:][None],
                           pltpu.roll(hr, d, axis=1), pltpu.roll(hi, d, axis=1))
            hr = hr + sr
            hi = hi + si
        cr, ci = carry_ref[t, :, :half], carry_ref[t, :, half:]
        cp_r, cp_i = cpow_ref[t, :, :half], cpow_ref[t, :, half:]
        prev_r, prev_i = [], []
        for kb in range(n_blk):
            kr, ki = _cmul(cp_r, cp_i, cr, ci)
            br, bi = hr[kb] + kr, hi[kb] + ki
            prev_r.append(jnp.where(sub2 == 0, cr, pltpu.roll(br, 1, axis=0)))
            prev_i.append(jnp.where(sub2 == 0, ci, pltpu.roll(bi, 1, axis=0)))
            cr, ci = br[SUBLANES - 1:SUBLANES, :], bi[SUBLANES - 1:SUBLANES, :]
        carry_ref[t, :, :half] = cr
        carry_ref[t, :, half:] = ci
        prev = jnp.concatenate([jnp.concatenate(prev_r, axis=0), jnp.concatenate(prev_i, axis=0)], axis=1)
        z = _dot(prev, ccat_ref[t])
        for s in range(L):
            zt_ref[t, pl.ds(s, n_rows, stride=L), :] = z[:, s * LANES:(s + 1) * LANES]

    y_state = jnp.concatenate([zt_ref[t] for t in tiles], axis=1)
    y_ref[0] = (y_lag + y_state + u * dskip_ref[...]).astype(y_ref.dtype)


def _s5_conv(h, tables, d_skip, *, tm=2048):
    bsz, seq, c = h.shape
    tm = min(tm, seq)
    m_lag, m_in, m_out, step_pow, carry_pow = tables
    n_slab = c // SLAB
    tps = SLAB // LANES
    st = m_in.shape[-1]
    chunk_cols = m_out.shape[-1]
    return pl.pallas_call(
        _s5_kernel,
        grid=(n_slab, bsz, seq // tm),
        in_specs=[
            pl.BlockSpec((1, tm, SLAB), lambda s, b, i: (b, i, s)),
            pl.BlockSpec((1,) + m_lag.shape[1:], lambda s, b, i: (s, 0, 0, 0)),
            pl.BlockSpec((tps,) + m_in.shape[1:], lambda s, b, i: (s, 0, 0, 0)),
            pl.BlockSpec((tps,) + m_out.shape[1:], lambda s, b, i: (s, 0, 0, 0)),
            pl.BlockSpec((tps,) + step_pow.shape[1:], lambda s, b, i: (s, 0, 0)),
            pl.BlockSpec((tps,) + carry_pow.shape[1:], lambda s, b, i: (s, 0, 0)),
            pl.BlockSpec((1, SLAB), lambda s, b, i: (0, s)),
        ],
        out_specs=pl.BlockSpec((1, tm, SLAB), lambda s, b, i: (b, i, s)),
        out_shape=jax.ShapeDtypeStruct((bsz, seq, c), BF16),
        scratch_shapes=[pltpu.VMEM((tps, 1, st), F32), pltpu.VMEM((tps, tm, LANES), F32),
                        pltpu.VMEM((tps, tm, LANES), F32),
                        pltpu.VMEM((S5_CHUNK * SLAB, SLAB), BF16),
                        pltpu.VMEM((tps, chunk_cols, st), BF16),
                        pltpu.VMEM((tps, st, chunk_cols), BF16)],
        compiler_params=_cparams(("arbitrary", "arbitrary", "arbitrary")),
        name="s5_conv",
    )(h, m_lag, m_in, m_out, step_pow, carry_pow, d_skip.reshape(1, c))


def _rwkv_block(h, ln0, ln1, mu, w0, w1, w2, a0, a1, a2, g1, g2, k_k, k_a, r_k, wr, wk, wv, wo,
                lnx_g, lnx_b, mlp_w1, mlp_w2, layer):
    bsz, seq, c = h.shape
    zeros = jnp.zeros((6, c), F32)
    vecs = jnp.concatenate([mu, w0[None], a0[None], k_k[None], k_a[None], zeros], axis=0)
    scan_vecs = jnp.concatenate([r_k.reshape(1, c), lnx_g[None], lnx_b[None], zeros[:5]], axis=0)
    idx = jnp.arange(SLAB) // HEAD
    jmat = (idx[:, None] == idx[None, :]).astype(BF16)
    r, k, v, lw, an, bn, gate = _rw_proj(h, vecs, wr, wk, wv, w1, w2, a1, a2, g1, g2, jmat)
    y = _rw_scan(r, k, v, lw, an, bn, scan_vecs)
    n = bsz * seq
    out = _mixer_out_mlp(_gate_mlp_kernel, "rwkv_out_mlp", [h.reshape(n, c), y.reshape(n, c), gate.reshape(n, c)],
                         wo, ln0, mlp_w1, mlp_w2, layer, ln1)
    return out.reshape(bsz, seq, c)


def _s5_block(h, ln0, ln1, a_re, a_im, log_dt, b_re, b_im, c_re, c_im, d_skip, w_glu, mlp_w1, mlp_w2, layer):
    bsz, seq, c = h.shape
    tables = _s5_tables(a_re, a_im, log_dt, b_re, b_im, c_re, c_im)
    y = _s5_conv(h, tables, d_skip)
    n = bsz * seq
    out = _mixer_out_mlp(_glu_mlp_kernel, "s5_out_mlp", [h.reshape(n, c), y.reshape(n, c)],
                         w_glu, ln0, mlp_w1, mlp_w2, layer, ln1)
    return out.reshape(bsz, seq, c)


def kernel(x, ln_g, ln_b, rw_mu, rw_w0, rw_w1, rw_w2, rw_a0, rw_a1, rw_a2, rw_g1, rw_g2, rw_k_k, rw_k_a, rw_r_k, rw_wr, rw_wk, rw_wv, rw_wo, rw_lnx_g, rw_lnx_b, s5_a_re, s5_a_im, s5_log_dt, s5_b_re, s5_b_im, s5_c_re, s5_c_im, s5_d, s5_w_glu, mlp_w1, mlp_w2):
    w1b, w2b = mlp_w1.astype(BF16), mlp_w2
    h = _rwkv_block(x, (ln_g[0], ln_b[0]), (ln_g[1], ln_b[1]), rw_mu[0], rw_w0[0], rw_w1[0], rw_w2[0],
                    rw_a0[0], rw_a1[0], rw_a2[0], rw_g1[0], rw_g2[0], rw_k_k[0], rw_k_a[0], rw_r_k[0],
                    rw_wr[0], rw_wk[0], rw_wv[0], rw_wo[0], rw_lnx_g[0], rw_lnx_b[0], w1b, w2b, 0)
    h = _s5_block(h, (ln_g[2], ln_b[2]), (ln_g[3], ln_b[3]), s5_a_re[0], s5_a_im[0], s5_log_dt[0],
                  s5_b_re[0], s5_b_im[0], s5_c_re[0], s5_c_im[0], s5_d[0], s5_w_glu[0], w1b, w2b, 1)
    return h
```
